```python
import math
import numpy as np
import jax
import jax.numpy as jnp
from jax import lax

D_MODEL = 2048
BATCH = 4
SEQ = 4096
DEPTH = 2

HEAD_DIM = 128
Q_BLOCK = 128
ROPE_THETA = 10000.0
EPS = 1e-6

NSA_HEADS = D_MODEL // (2 * HEAD_DIM)
NSA_KV_HEADS = NSA_HEADS // 4
NSA_CMP_LEN = 32
NSA_CMP_STRIDE = 16
NSA_CMP_HIDDEN = 2 * HEAD_DIM
NSA_SLC_LEN = 64
NSA_SLC_TOPK = 8
NSA_WINDOW = 512
DIFF_VDIM = 2 * HEAD_DIM
DIFF_HEADS = D_MODEL // (2 * DIFF_VDIM)
NSA_WIDTH = NSA_HEADS * HEAD_DIM
DIFF_WIDTH = DIFF_HEADS * DIFF_VDIM
MIX_WIDTH = NSA_WIDTH + DIFF_WIDTH
EVEN_SPLITS = (NSA_WIDTH, 6 * NSA_KV_HEADS * HEAD_DIM, 3 * NSA_HEADS,
               2 * DIFF_HEADS * HEAD_DIM, 2 * DIFF_HEADS * HEAD_DIM, DIFF_WIDTH)
EVEN_IN = sum(EVEN_SPLITS)

FOX_HEADS = D_MODEL // HEAD_DIM
FOX_WIDTH = FOX_HEADS * HEAD_DIM
ODD_IN = 3 * FOX_WIDTH + FOX_HEADS

MEM_LEN = 256
MEM_HEADS = 4
MEM_WIDTH = MEM_HEADS * HEAD_DIM

N_GROUPS = 4
EXPERTS_PER_GROUP = 8
N_EXPERTS = N_GROUPS * EXPERTS_PER_GROUP
EXPERT_TOPK = 2
EXPERT_FF = D_MODEL // 4
MOE_BLOCK = 128

kernel_name = 'hybrid_nsa_diff_fox_hmoe'


def _split_points(sizes):
    return [int(v) for v in np.cumsum(sizes)[:-1]]


def rmsnorm(x, g):
    xf = x.astype(jnp.float32)
    y = xf * lax.rsqrt(jnp.mean(xf * xf, axis=-1, keepdims=True) + EPS)
    return (y * g.astype(jnp.float32)).astype(x.dtype)


def rope(x, pos):
    half = HEAD_DIM // 2
    inv = ROPE_THETA ** (-jnp.arange(half, dtype=jnp.float32) / half)
    ang = jnp.asarray(pos).astype(jnp.float32)[:, None] * inv[None, :]
    cos, sin = jnp.cos(ang), jnp.sin(ang)
    xf = x.astype(jnp.float32)
    x1, x2 = xf[..., :half], xf[..., half:]
    return jnp.concatenate([x1 * cos - x2 * sin, x2 * cos + x1 * sin], axis=-1).astype(x.dtype)


def masked_softmax(s, mask):
    s = jnp.where(mask, s, -jnp.inf)
    m = jnp.max(s, axis=-1, keepdims=True)
    m = jnp.where(jnp.isfinite(m), m, 0.0)
    e = jnp.where(mask, jnp.exp(s - m), 0.0)
    return e / jnp.maximum(jnp.sum(e, axis=-1, keepdims=True), 1e-30)


def nsa_attention(q, k_cmp, v_cmp, k_slc, v_slc, k_win, v_win, gates, g_kc, cmp_pe, cmp_w1, cmp_w2):
    B, G, R, T, dh = q.shape
    dt = q.dtype
    scale = dh ** -0.5
    L, S, Ls, W = NSA_CMP_LEN, NSA_CMP_STRIDE, NSA_SLC_LEN, NSA_WINDOW
    n_cmp = (T - L) // S + 1
    n_slc = T // Ls
    topk = min(NSA_SLC_TOPK, n_slc)
    cmp_idx = np.arange(n_cmp)[:, None] * S + np.arange(L)[None, :]
    cmp_end = cmp_idx[:, -1]
    starts = np.arange(n_cmp) * S
    sb = np.arange(n_slc) * Ls
    overlap = np.clip(np.minimum(starts[:, None] + L, sb[None, :] + Ls)
                      - np.maximum(starts[:, None], sb[None, :]), 0, None) / L
    overlap = jnp.asarray(overlap, jnp.float32)
    cmp_end_j = jnp.asarray(cmp_end)

    def compress(t, i):
        blocks = t[:, :, cmp_idx] + cmp_pe[i]
        hid = jax.nn.gelu(blocks.reshape(B, G, n_cmp, L * dh) @ cmp_w1[i])
        return hid @ cmp_w2[i]

    kc = rope(rmsnorm(compress(k_cmp, 0), g_kc), cmp_end)
    vc = compress(v_cmp, 1)
    ks_blk = k_slc.reshape(B, G, n_slc, Ls, dh)
    vs_blk = v_slc.reshape(B, G, n_slc, Ls, dh)
    kw_pad = jnp.pad(k_win, ((0, 0), (0, 0), (W, 0), (0, 0)))
    vw_pad = jnp.pad(v_win, ((0, 0), (0, 0), (W, 0), (0, 0)))
    bi = jnp.arange(B)[:, None, None, None]
    gi = jnp.arange(G)[None, :, None, None]
    jblk = jnp.arange(n_slc)[None, :]

    def block(b):
        q0 = b * Q_BLOCK
        qpos = q0 + jnp.arange(Q_BLOCK)
        qb = lax.dynamic_slice_in_dim(q, q0, Q_BLOCK, axis=3)
        gb = lax.dynamic_slice_in_dim(gates, q0, Q_BLOCK, axis=3)
        s_c = jnp.einsum('bgrqd,bgnd->bgrqn', qb, kc).astype(jnp.float32) * scale
        p_c = masked_softmax(s_c, cmp_end_j[None, :] <= qpos[:, None])
        o_c = jnp.einsum('bgrqn,bgnd->bgrqd', p_c.astype(dt), vc)
        imp = jnp.einsum('bgrqn,nj->bgqj', p_c, overlap)
        qblk = (qpos // Ls)[:, None]
        forced = (jblk == 0) | (jblk == qblk) | (jblk == qblk - 1)
        imp = jnp.where(jblk > qblk, -jnp.inf, jnp.where(forced, jnp.inf, imp))
        _, sel = lax.top_k(imp, topk)
        k_sel = ks_blk[bi, gi, sel]
        v_sel = vs_blk[bi, gi, sel]
        kpos = sel[..., None] * Ls + jnp.arange(Ls)
        mask_s = (kpos <= qpos[:, None, None]).reshape(B, G, 1, Q_BLOCK, topk * Ls)
        s_s = jnp.einsum('bgrqd,bgqnld->bgrqnl', qb, k_sel).astype(jnp.float32) * scale
        p_s = masked_softmax(s_s.reshape(B, G, R, Q_BLOCK, topk * Ls), mask_s)
        p_s = p_s.reshape(B, G, R, Q_BLOCK, topk, Ls)
        o_s = jnp.einsum('bgrqnl,bgqnld->bgrqd', p_s.astype(dt), v_sel)
        kw = lax.dynamic_slice_in_dim(kw_pad, q0, Q_BLOCK + W, axis=2)
        vw = lax.dynamic_slice_in_dim(vw_pad, q0, Q_BLOCK + W, axis=2)
        wpos = q0 - W + jnp.arange(Q_BLOCK + W)
        rel = qpos[:, None] - wpos[None, :]
        mask_w = (rel >= 0) & (rel < W) & (wpos[None, :] >= 0)
        s_w = jnp.einsum('bgrqd,bgkd->bgrqk', qb, kw).astype(jnp.float32) * scale
        o_w = jnp.einsum('bgrqk,bgkd->bgrqd', masked_softmax(s_w, mask_w).astype(dt), vw)
        return gb[..., 0:1] * o_c + gb[..., 1:2] * o_s + gb[..., 2:3] * o_w

    out = lax.map(block, jnp.arange(T // Q_BLOCK))
    return out.transpose(1, 2, 3, 0, 4, 5).reshape(B, G * R, T, dh)


def diff_attention(q, k, v, lam_p, sub_g, layer):
    B, H, _, T, dh = q.shape
    dt = v.dtype
    scale = dh ** -0.5
    lam_init = 0.8 - 0.6 * math.exp(-0.3 * layer)
    lp = lam_p.astype(jnp.float32)
    lam = jnp.exp(jnp.sum(lp[0] * lp[1])) - jnp.exp(jnp.sum(lp[2] * lp[3])) + lam_init
    outs = []
    for b in range(T // Q_BLOCK):
        lo, hi = b * Q_BLOCK, (b + 1) * Q_BLOCK
        mask = jnp.arange(hi)[None, :] <= jnp.arange(lo, hi)[:, None]
        s = jnp.einsum('bhmqd,bhmkd->bhmqk', q[:, :, :, lo:hi], k[:, :, :, :hi]).astype(jnp.float32) * scale
        p = masked_softmax(s, mask)
        a = p[:, :, 0] - lam * p[:, :, 1]
        outs.append(jnp.einsum('bhqk,bhkd->bhqd', a.astype(dt), v[:, :, :hi]))
    o = jnp.concatenate(outs, axis=2)
    return rmsnorm(o, sub_g) * (1.0 - lam_init)


def forgetting_attention(q, k, v, c):
    B, H, T, dh = q.shape
    dt = v.dtype
    scale = dh ** -0.5
    outs = []
    for b in range(T // Q_BLOCK):
        lo, hi = b * Q_BLOCK, (b + 1) * Q_BLOCK
        mask = jnp.arange(hi)[None, :] <= jnp.arange(lo, hi)[:, None]
        s = (jnp.einsum('bhqd,bhkd->bhqk', q[:, :, lo:hi], k[:, :, :hi]).astype(jnp.float32) * scale
             + c[:, :, lo:hi, None] - c[:, :, None, :hi])
        p = masked_softmax(s, mask)
        outs.append(jnp.einsum('bhqk,bhkd->bhqd', p.astype(dt), v[:, :, :hi]))
    return jnp.concatenate(outs, axis=2)


def even_mixer(h, w_in, w_out, nsa_qk_g, cmp_pe, cmp_w1, cmp_w2, diff_qk_g, diff_lam, diff_sub_g, layer):
    B, T, _ = h.shape
    G, R = NSA_KV_HEADS, NSA_HEADS // NSA_KV_HEADS
    pos = jnp.arange(T)
    zq, zkv, zg, zdq, zdk, zdv = jnp.split(h @ w_in, _split_points(EVEN_SPLITS), axis=-1)
    q = rope(rmsnorm(zq.reshape(B, T, NSA_HEADS, HEAD_DIM), nsa_qk_g[0]).transpose(0, 2, 1, 3), pos)
    q = q.reshape(B, G, R, T, HEAD_DIM)
    kv = zkv.reshape(B, T, 3, 2, G, HEAD_DIM).transpose(2, 3, 0, 4, 1, 5)
    k_slc = rope(rmsnorm(kv[1, 0], nsa_qk_g[2]), pos)
    k_win = rope(rmsnorm(kv[2, 0], nsa_qk_g[3]), pos)
    gates = jax.nn.sigmoid(zg.reshape(B, T, G, R, 3)).transpose(0, 2, 3, 1, 4)
    o_nsa = nsa_attention(q, kv[0, 0], kv[0, 1], k_slc, kv[1, 1], k_win, kv[2, 1], gates,
                          nsa_qk_g[1], cmp_pe, cmp_w1, cmp_w2)
    o_nsa = o_nsa.transpose(0, 2, 1, 3).reshape(B, T, NSA_WIDTH)
    dq = rope(rmsnorm(zdq.reshape(B, T, DIFF_HEADS, 2, HEAD_DIM), diff_qk_g[0]).transpose(0, 2, 3, 1, 4), pos)
    dk = rope(rmsnorm(zdk.reshape(B, T, DIFF_HEADS, 2, HEAD_DIM), diff_qk_g[1]).transpose(0, 2, 3, 1, 4), pos)
    dv = zdv.reshape(B, T, DIFF_HEADS, DIFF_VDIM).transpose(0, 2, 1, 3)
    o_diff = diff_attention(dq, dk, dv, diff_lam, diff_sub_g, layer)
    o_diff = o_diff.transpose(0, 2, 1, 3).reshape(B, T, DIFF_WIDTH)
    return jnp.concatenate([o_nsa, o_diff], axis=-1) @ w_out


def odd_mixer(h, w_in, w_out, f_b, qk_g):
    B, T, _ = h.shape
    zq, zk, zv, zf = jnp.split(h @ w_in, [FOX_WIDTH, 2 * FOX_WIDTH, 3 * FOX_WIDTH], axis=-1)
    q = rmsnorm(zq.reshape(B, T, FOX_HEADS, HEAD_DIM), qk_g[0]).transpose(0, 2, 1, 3)
    k = rmsnorm(zk.reshape(B, T, FOX_HEADS, HEAD_DIM), qk_g[1]).transpose(0, 2, 1, 3)
    v = zv.reshape(B, T, FOX_HEADS, HEAD_DIM).transpose(0, 2, 1, 3)
    log_f = jax.nn.log_sigmoid(zf.astype(jnp.float32) + f_b.astype(jnp.float32)).transpose(0, 2, 1)
    c = jnp.cumsum(log_f, axis=-1)
    o = forgetting_attention(q, k, v, c)
    return o.transpose(0, 2, 1, 3).reshape(B, T, FOX_WIDTH) @ w_out


def memory_cross_attention(h, m, wq, wkv, qk_g, wo):
    B, T, _ = h.shape
    M = m.shape[1]
    q = rmsnorm((h @ wq).reshape(B, T, MEM_HEADS, HEAD_DIM), qk_g[0])
    kv = (m @ wkv).reshape(B, M, 2, MEM_HEADS, HEAD_DIM)
    k = rmsnorm(kv[:, :, 0], qk_g[1])
    s = jnp.einsum('bthd,bmhd->bhtm', q, k).astype(jnp.float32) * (HEAD_DIM ** -0.5)
    p = jax.nn.softmax(s, axis=-1)
    o = jnp.einsum('bhtm,bmhd->bthd', p.astype(h.dtype), kv[:, :, 1]).reshape(B, T, MEM_WIDTH)
    return o @ wo


def hier_moe(h, wg, bg, we, be, w1, w3, w2):
    B, T, D = h.shape
    N = B * T
    dt = h.dtype
    hf = h.reshape(N, D)
    p_grp = jax.nn.softmax((hf @ wg).astype(jnp.float32) + bg.astype(jnp.float32), axis=-1)
    grp = jnp.argmax(p_grp, axis=-1)
    p_top_grp = jnp.max(p_grp, axis=-1)
    le = ((hf @ we).astype(jnp.float32) + be.astype(jnp.float32)).reshape(N, N_GROUPS, EXPERTS_PER_GROUP)
    le = jnp.take_along_axis(le, grp[:, None, None], axis=1)[:, 0]
    p_in, j_in = lax.top_k(jax.nn.softmax(le, axis=-1), EXPERT_TOPK)
    wts = p_top_grp[:, None] * p_in / jnp.sum(p_in, axis=-1, keepdims=True)
    eid = (grp[:, None] * EXPERTS_PER_GROUP + j_in).reshape(-1).astype(jnp.int32)
    tok = jnp.repeat(jnp.arange(N, dtype=jnp.int32), EXPERT_TOPK)
    wflat = wts.reshape(-1)
    M = N * EXPERT_TOPK
    order = jnp.argsort(eid)
    e_s, tok_s, w_s = eid[order], tok[order], wflat[order]
    counts = jnp.zeros((N_EXPERTS,), jnp.int32).at[eid].add(1)
    start = jnp.cumsum(counts) - counts
    pcounts = (counts + MOE_BLOCK - 1) // MOE_BLOCK * MOE_BLOCK
    pend = jnp.cumsum(pcounts)
    pstart = pend - pcounts
    dest = pstart[e_s] + jnp.arange(M, dtype=jnp.int32) - start[e_s]
    n_blk = -(-M // MOE_BLOCK) + N_EXPERTS
    P = n_blk * MOE_BLOCK
    buf_tok = jnp.full((P,), N, jnp.int32).at[dest].set(tok_s)
    buf_w = jnp.zeros((P,), jnp.float32).at[dest].set(w_s)
    blk_e = jnp.minimum(jnp.searchsorted(pend, jnp.arange(n_blk, dtype=jnp.int32) * MOE_BLOCK, side='right'),
                        N_EXPERTS - 1)
    xb = jnp.concatenate([hf, jnp.zeros((1, D), dt)], axis=0)[buf_tok].reshape(n_blk, MOE_BLOCK, D)

    def expert_block(args):
        xi, e = args
        return (jax.nn.silu(xi @ w1[e]) * (xi @ w3[e])) @ w2[e]

    yb = lax.map(expert_block, (xb, blk_e)).reshape(P, D)
    out = jnp.zeros((N + 1, D), dt).at[buf_tok].add((yb.astype(jnp.float32) * buf_w[:, None]).astype(dt))
    return out[:N].reshape(B, T, D)


def setup_inputs(seed: int = 0) -> dict:
    key = jax.random.key(seed)
    keys = jax.random.split(key, 32)
    ki = iter(range(32))
    D, dh = D_MODEL, HEAD_DIM
    NE, NO = (DEPTH + 1) // 2, DEPTH // 2

    def normal(shape, scale):
        return scale * jax.random.normal(keys[next(ki)], shape, jnp.float32)

    def gain(shape):
        return 1.0 + 0.05 * jax.random.normal(keys[next(ki)], shape, jnp.float32)

    x = normal((BATCH, SEQ, D), 1.0)
    mem = normal((BATCH, MEM_LEN, D), 1.0)
    norm_g = gain((DEPTH, 4, D))
    ev_w_in = normal((NE, D, EVEN_IN), D ** -0.5)
    ev_w_out = normal((NE, MIX_WIDTH, D), MIX_WIDTH ** -0.5)
    nsa_qk_g = gain((NE, 4, dh))
    nsa_cmp_pe = normal((NE, 2, NSA_CMP_LEN, dh), 0.2)
    nsa_cmp_w1 = normal((NE, 2, NSA_CMP_LEN * dh, NSA_CMP_HIDDEN), (NSA_CMP_LEN * dh) ** -0.5)
    nsa_cmp_w2 = normal((NE, 2, NSA_CMP_HIDDEN, dh), NSA_CMP_HIDDEN ** -0.5)
    diff_qk_g = gain((NE, 2, dh))
    diff_lam = normal((NE, 4, dh), 0.1)
    diff_sub_g = gain((NE, DIFF_VDIM))
    od_w_in = jnp.concatenate([normal((NO, D, 3 * FOX_WIDTH), D ** -0.5),
                               normal((NO, D, FOX_HEADS), 0.1 * D ** -0.5)], axis=-1)
    od_w_out = normal((NO, FOX_WIDTH, D), FOX_WIDTH ** -0.5)
    fox_f_b = jax.random.uniform(keys[next(ki)], (NO, FOX_HEADS), jnp.float32, 1.0, 5.0)
    fox_qk_g = gain((NO, 2, dh))
    ca_wq = normal((DEPTH, D, MEM_WIDTH), D ** -0.5)
    ca_wkv = normal((DEPTH, D, 2 * MEM_WIDTH), D ** -0.5)
    ca_qk_g = gain((DEPTH, 2, dh))
    ca_wo = normal((DEPTH, MEM_WIDTH, D), MEM_WIDTH ** -0.5)
    moe_wg = normal((DEPTH, D, N_GROUPS), D ** -0.5)
    moe_bg = normal((DEPTH, N_GROUPS), 0.01)
    moe_we = normal((DEPTH, D, N_EXPERTS), D ** -0.5)
    moe_be = normal((DEPTH, N_EXPERTS), 0.01)
    moe_w1 = normal((DEPTH, N_EXPERTS, D, EXPERT_FF), D ** -0.5)
    moe_w3 = normal((DEPTH, N_EXPERTS, D, EXPERT_FF), D ** -0.5)
    moe_w2 = normal((DEPTH, N_EXPERTS, EXPERT_FF, D), EXPERT_FF ** -0.5)
    return {'x': x, 'mem': mem, 'norm_g': norm_g, 'ev_w_in': ev_w_in, 'ev_w_out': ev_w_out,
            'nsa_qk_g': nsa_qk_g, 'nsa_cmp_pe': nsa_cmp_pe, 'nsa_cmp_w1': nsa_cmp_w1,
            'nsa_cmp_w2': nsa_cmp_w2, 'diff_qk_g': diff_qk_g, 'diff_lam': diff_lam,
            'diff_sub_g': diff_sub_g, 'od_w_in': od_w_in, 'od_w_out': od_w_out, 'fox_f_b': fox_f_b,
            'fox_qk_g': fox_qk_g, 'ca_wq': ca_wq, 'ca_wkv': ca_wkv, 'ca_qk_g': ca_qk_g, 'ca_wo': ca_wo,
            'moe_wg': moe_wg, 'moe_bg': moe_bg, 'moe_we': moe_we, 'moe_be': moe_be,
            'moe_w1': moe_w1, 'moe_w3': moe_w3, 'moe_w2': moe_w2}


def reference(x, mem, norm_g, ev_w_in, ev_w_out, nsa_qk_g, nsa_cmp_pe, nsa_cmp_w1, nsa_cmp_w2,
              diff_qk_g, diff_lam, diff_sub_g, od_w_in, od_w_out, fox_f_b, fox_qk_g,
              ca_wq, ca_wkv, ca_qk_g, ca_wo, moe_wg, moe_bg, moe_we, moe_be, moe_w1, moe_w3, moe_w2):
    for layer in range(DEPTH):
        i = layer // 2
        h = rmsnorm(x, norm_g[layer, 0])
        if layer % 2 == 0:
            x = x + even_mixer(h, ev_w_in[i], ev_w_out[i], nsa_qk_g[i], nsa_cmp_pe[i], nsa_cmp_w1[i],
                               nsa_cmp_w2[i], diff_qk_g[i], diff_lam[i], diff_sub_g[i], layer)
        else:
            x = x + odd_mixer(h, od_w_in[i], od_w_out[i], fox_f_b[i], fox_qk_g[i])
        x = x + memory_cross_attention(rmsnorm(x, norm_g[layer, 1]), rmsnorm(mem, norm_g[layer, 2]),
                                       ca_wq[layer], ca_wkv[layer], ca_qk_g[layer], ca_wo[layer])
        x = x + hier_moe(rmsnorm(x, norm_g[layer, 3]), moe_wg[layer], moe_bg[layer], moe_we[layer],
                         moe_be[layer], moe_w1[layer], moe_w3[layer], moe_w2[layer])
    return x
```

```python
import functools
import math

import numpy as np
import jax
import jax.numpy as jnp
from jax import lax
from jax.experimental import pallas as pl
from jax.experimental.pallas import tpu as pltpu

F32 = jnp.float32
BF16 = jnp.bfloat16

HEAD_DIM = 128
ROPE_THETA = 10000.0
EPS = 1e-6
NEG = -1e30

NSA_HEADS = 8
NSA_GROUPS = 2
NSA_REP = NSA_HEADS // NSA_GROUPS
NSA_CMP_LEN = 32
NSA_CMP_STRIDE = 16
NSA_CMP_HIDDEN = 256
NSA_SLC_LEN = 64
NSA_SLC_TOPK = 8
NSA_WINDOW = 512
NSA_QBLOCK = 128
DIFF_HEADS = 4
FOX_HEADS = 16
MEM_HEADS = 4
N_GROUPS = 4
EXPERTS_PER_GROUP = 8
N_EXPERTS = N_GROUPS * EXPERTS_PER_GROUP
MOE_ROWS = 256
LANES = 128
VMEM_LIMIT = 56 * 1024 * 1024


def _cparams(sem):
    return pltpu.CompilerParams(dimension_semantics=sem, vmem_limit_bytes=VMEM_LIMIT)


def _dot(a, b):
    return jnp.dot(a, b, preferred_element_type=F32)


def _dot_nt(a, b):
    return lax.dot_general(a, b, (((1,), (1,)), ((), ())), preferred_element_type=F32)


def _rms(x, g):
    ms = jnp.mean(x * x, axis=-1, keepdims=True)
    return x * lax.rsqrt(ms + EPS) * g


def _rope(y, cos, sin_signed):
    return y * cos + pltpu.roll(y, HEAD_DIM // 2, 1) * sin_signed


def _rms_matmul_kernel(x_ref, g_ref, w_ref, o_ref, xn_ref):
    @pl.when(pl.program_id(1) == 0)
    def _():
        xn_ref[...] = _rms(x_ref[...], g_ref[...]).astype(BF16)

    o_ref[...] = _dot(xn_ref[...], w_ref[...])


def rms_matmul(x, g, w, *, tm=1024, tn=512, name="rms_matmul"):
    M, K = x.shape
    N = w.shape[1]
    tm, tn = min(tm, M), min(tn, N)
    assert M % tm == 0 and N % tn == 0
    return pl.pallas_call(
        _rms_matmul_kernel,
        out_shape=jax.ShapeDtypeStruct((M, N), F32),
        grid=(M // tm, N // tn),
        in_specs=[pl.BlockSpec((tm, K), lambda i, j: (i, 0)),
                  pl.BlockSpec((1, K), lambda i, j: (0, 0)),
                  pl.BlockSpec((K, tn), lambda i, j: (0, j))],
        out_specs=pl.BlockSpec((tm, tn), lambda i, j: (i, j)),
        scratch_shapes=[pltpu.VMEM((tm, K), BF16)],
        compiler_params=_cparams(("arbitrary", "arbitrary")),
        name=name,
    )(x, g.reshape(1, K), w)


def _matmul_res_kernel(*refs):
    *x_refs, w_ref, r_ref, o_ref = refs
    acc = r_ref[...]
    k0 = 0
    for x_ref in x_refs:
        k = x_ref.shape[1]
        acc = acc + _dot(x_ref[...], w_ref[k0:k0 + k, :])
        k0 += k
    o_ref[...] = acc


def matmul_residual(xs, w, res, *, tm=1024, tn=512, name="matmul_residual"):
    M = xs[0].shape[0]
    N = w.shape[1]
    tm, tn = min(tm, M), min(tn, N)
    assert M % tm == 0 and N % tn == 0 and sum(x.shape[1] for x in xs) == w.shape[0]
    return pl.pallas_call(
        _matmul_res_kernel,
        out_shape=jax.ShapeDtypeStruct((M, N), F32),
        grid=(M // tm, N // tn),
        in_specs=[pl.BlockSpec((tm, x.shape[1]), lambda i, j: (i, 0)) for x in xs]
        + [pl.BlockSpec((w.shape[0], tn), lambda i, j: (0, j)),
           pl.BlockSpec((tm, tn), lambda i, j: (i, j))],
        out_specs=pl.BlockSpec((tm, tn), lambda i, j: (i, j)),
        compiler_params=_cparams(("arbitrary", "arbitrary")),
        name=name,
    )(*xs, w, res)


def _head_prep_kernel(gidx_ref, norm_ref, rope_ref, z_ref, gains_ref, cos_ref, sin_ref, o_ref):
    j = pl.program_id(1)
    x = z_ref[...]
    g = gains_ref[pl.ds(gidx_ref[j], 1), :]
    y = jnp.where(norm_ref[j] > 0, _rms(x, g), x)
    y = jnp.where(rope_ref[j] > 0, _rope(y, cos_ref[...], sin_ref[...]), y)
    o_ref[...] = y.astype(BF16)


def head_prep(z, n_blocks, gidx, norm, rope, gains, cos, sin, seq, *, tm=512, name="head_prep"):
    M = z.shape[0]
    tm = min(tm, seq)
    assert seq % tm == 0 and M % tm == 0
    nt = seq // tm
    grid_spec = pltpu.PrefetchScalarGridSpec(
        num_scalar_prefetch=3,
        grid=(M // tm, n_blocks),
        in_specs=[pl.BlockSpec((tm, LANES), lambda i, j, *_: (i, j)),
                  pl.BlockSpec(gains.shape, lambda i, j, *_: (0, 0)),
                  pl.BlockSpec((tm, LANES), lambda i, j, *_: (i % nt, 0)),
                  pl.BlockSpec((tm, LANES), lambda i, j, *_: (i % nt, 0))],
        out_specs=pl.BlockSpec((tm, LANES), lambda i, j, *_: (i, j)),
    )
    return pl.pallas_call(
        _head_prep_kernel,
        out_shape=jax.ShapeDtypeStruct((M, n_blocks * LANES), BF16),
        grid_spec=grid_spec,
        compiler_params=_cparams(("arbitrary", "arbitrary")),
        name=name,
    )(jnp.asarray(gidx, jnp.int32), jnp.asarray(norm, jnp.int32), jnp.asarray(rope, jnp.int32),
      z, gains, cos, sin)


def rope_tables(pos):
    half = HEAD_DIM // 2
    inv = ROPE_THETA ** (-jnp.arange(half, dtype=F32) / half)
    ang = jnp.asarray(pos).astype(F32)[:, None] * inv[None, :]
    cos, sin = jnp.cos(ang), jnp.sin(ang)
    return jnp.concatenate([cos, cos], axis=-1), jnp.concatenate([-sin, sin], axis=-1)


def _nsa_compress_kernel(t_ref, pe_ref, w1_ref, w2_ref, g_ref, cos_ref, sin_ref, o_ref, *, n_rows):
    kv = pl.program_id(2)
    half = NSA_CMP_LEN // 2
    hid = w1_ref.shape[-1]
    lo = jnp.zeros((n_rows, hid), F32)
    hi = jnp.zeros((n_rows, hid), F32)
    for l in range(NSA_CMP_LEN):
        rows = t_ref[pl.ds(l % half, n_rows, stride=NSA_CMP_STRIDE), :]
        a = (rows + pe_ref[l:l + 1, :]).astype(BF16)
        part = _dot(a, w1_ref[l * HEAD_DIM:(l + 1) * HEAD_DIM, :])
        if l < half:
            lo = lo + part
        else:
            hi = hi + part
    pre = lo + pltpu.roll(hi, n_rows - 1, 0)
    out = _dot(jax.nn.gelu(pre).astype(BF16), w2_ref[...])
    keyed = _rope(_rms(out, g_ref[...]), cos_ref[...], sin_ref[...])
    o_ref[...] = jnp.where(kv == 0, keyed, out).astype(BF16)


def nsa_compress(z, col0, batch, seq, pe, w1, w2, g_kc, cos_c, sin_c):
    n_rows = seq // NSA_CMP_STRIDE
    G = NSA_GROUPS
    kern = functools.partial(_nsa_compress_kernel, n_rows=n_rows)
    return pl.pallas_call(
        kern,
        out_shape=jax.ShapeDtypeStruct((batch, G, 2, n_rows, HEAD_DIM), BF16),
        grid=(batch, G, 2),
        in_specs=[pl.BlockSpec((seq, LANES), lambda b, g, kv: (b, col0 + kv * G + g)),
                  pl.BlockSpec((None, NSA_CMP_LEN, HEAD_DIM), lambda b, g, kv: (kv, 0, 0)),
                  pl.BlockSpec((None,) + w1.shape[1:], lambda b, g, kv: (kv, 0, 0)),
                  pl.BlockSpec((None,) + w2.shape[1:], lambda b, g, kv: (kv, 0, 0)),
                  pl.BlockSpec((1, HEAD_DIM), lambda b, g, kv: (0, 0)),
                  pl.BlockSpec((n_rows, HEAD_DIM), lambda b, g, kv: (0, 0)),
                  pl.BlockSpec((n_rows, HEAD_DIM), lambda b, g, kv: (0, 0))],
        out_specs=pl.BlockSpec((None, None, None, n_rows, HEAD_DIM), lambda b, g, kv: (b, g, kv, 0, 0)),
        compiler_params=_cparams(("arbitrary", "arbitrary", "arbitrary")),
        name="nsa_compress",
    )(z, pe, w1, w2, g_kc.reshape(1, HEAD_DIM), cos_c, sin_c)


def _softmax_rows(s):
    m = jnp.max(s, axis=1, keepdims=True)
    e = jnp.exp(s - m)
    return e / jnp.maximum(jnp.sum(e, axis=1, keepdims=True), 1e-30)


def _nsa_attn_kernel(q_ref, kc_ref, vc_ref, ks_ref, vs_ref, kw_ref, vw_ref, gate_ref,
                     ov_ref, expand_ref, o_ref, *, tq, tk, seq):
    R = NSA_REP
    qi = pl.program_id(2)
    q0 = qi * tq
    n_cmp = kc_ref.shape[0]
    n_slc = seq // NSA_SLC_LEN
    q = q_ref[...]
    qs = jnp.concatenate([q[:, r * HEAD_DIM:(r + 1) * HEAD_DIM] for r in range(R)], axis=0)
    qpos1 = q0 + lax.broadcasted_iota(jnp.int32, (tq, 1), 0)
    qpos = jnp.concatenate([qpos1] * R, axis=0)

    cmp_end = (lax.broadcasted_iota(jnp.int32, (1, n_cmp), 1) * NSA_CMP_STRIDE + (NSA_CMP_LEN - 1))
    s_c = jnp.where(cmp_end <= qpos, _dot_nt(qs, kc_ref[...]), NEG)
    m_c = jnp.max(s_c, axis=1, keepdims=True)
    e_c = jnp.where(cmp_end <= qpos, jnp.exp(s_c - m_c), 0.0)
    p_c = e_c / jnp.maximum(jnp.sum(e_c, axis=1, keepdims=True), 1e-30)
    o_c = _dot(p_c.astype(BF16), vc_ref[...])

    p_sum = p_c[0:tq]
    for r in range(1, R):
        p_sum = p_sum + p_c[r * tq:(r + 1) * tq]
    p_hi = p_sum.astype(BF16)
    p_lo = (p_sum - p_hi.astype(F32)).astype(BF16)
    imp = _dot(p_hi, ov_ref[...]) + _dot(p_lo, ov_ref[...])
    jblk = lax.broadcasted_iota(jnp.int32, (tq, n_slc), 1)
    qblk = qpos1 // NSA_SLC_LEN
    forced = (jblk == 0) | (jblk == qblk) | (jblk == qblk - 1)
    imp = jnp.where(jblk > qblk, -jnp.inf, jnp.where(forced, jnp.inf, imp))
    picked = jnp.zeros((tq, n_slc), jnp.int32)
    for _ in range(min(NSA_SLC_TOPK, n_slc)):
        cand = jnp.where(picked > 0, -jnp.inf, imp)
        best = jnp.max(cand, axis=1, keepdims=True)
        first = jnp.min(jnp.where(cand == best, jblk + picked * n_slc, n_slc), axis=1, keepdims=True)
        picked = jnp.where(jblk == first, 1, picked)
    sel_bias = jnp.where(picked > 0, 0.0, NEG).astype(BF16)

    def slc_tile(j, carry, causal):
        m, l, acc = carry
        k0 = pl.multiple_of(j * tk, tk)
        s = _dot_nt(qs, ks_ref[pl.ds(k0, tk), :])
        bias = _dot(sel_bias, expand_ref[:, pl.ds(k0, tk)])
        if causal:
            kpos = k0 + lax.broadcasted_iota(jnp.int32, (1, tk), 1)
            bias = jnp.where(kpos <= qpos1, bias, NEG)
        s = (s.reshape(R, tq, tk) + bias[None]).reshape(R * tq, tk)
        m_new = jnp.maximum(m, jnp.max(s, axis=1, keepdims=True))
        alpha = jnp.exp(m - m_new)
        p = jnp.exp(s - m_new)
        l = alpha * l + jnp.sum(p, axis=1, keepdims=True)
        acc = alpha * acc + _dot(p.astype(BF16), vs_ref[pl.ds(k0, tk), :])
        return m_new, l, acc

    init = (jnp.full((R * tq, 1), NEG, F32), jnp.zeros((R * tq, 1), F32), jnp.zeros((R * tq, HEAD_DIM), F32))
    last = q0 // tk
    carry = lax.fori_loop(0, last, lambda j, c: slc_tile(j, c, False), init)
    _, l_s, acc_s = slc_tile(last, carry, True)
    o_s = acc_s / jnp.maximum(l_s, 1e-30)

    span = NSA_WINDOW + tq
    w0 = pl.multiple_of(jnp.maximum(q0 - NSA_WINDOW, 0), tq)
    wpos = w0 + lax.broadcasted_iota(jnp.int32, (1, span), 1)
    rel = qpos - wpos
    ok_w = (rel >= 0) & (rel < NSA_WINDOW)
    s_w = jnp.where(ok_w, _dot_nt(qs, kw_ref[pl.ds(w0, span), :]), NEG)
    m_w = jnp.max(s_w, axis=1, keepdims=True)
    e_w = jnp.where(ok_w, jnp.exp(s_w - m_w), 0.0)
    p_w = e_w / jnp.maximum(jnp.sum(e_w, axis=1, keepdims=True), 1e-30)
    o_w = _dot(p_w.astype(BF16), vw_ref[pl.ds(w0, span), :])

    gates = jax.nn.sigmoid(gate_ref[...])
    outs = []
    for r in range(R):
        rows = slice(r * tq, (r + 1) * tq)
        outs.append(gates[:, 3 * r:3 * r + 1] * o_c[rows] + gates[:, 3 * r + 1:3 * r + 2] * o_s[rows]
                    + gates[:, 3 * r + 2:3 * r + 3] * o_w[rows])
    o_ref[...] = jnp.concatenate(outs, axis=1).astype(BF16)


def nsa_attention(zb, kvc, gates, batch, seq, kv_col0):
    tq, tk = NSA_QBLOCK, 512
    G, R = NSA_GROUPS, NSA_REP
    tk = min(tk, seq)
    assert seq % tk == 0 and tk % tq == 0 and seq >= NSA_WINDOW + tq
    nq = seq // tq
    n_cmp = seq // NSA_CMP_STRIDE
    n_slc = seq // NSA_SLC_LEN
    starts = np.arange(n_cmp) * NSA_CMP_STRIDE
    sb = np.arange(n_slc) * NSA_SLC_LEN
    overlap = np.clip(np.minimum(starts[:, None] + NSA_CMP_LEN, sb[None, :] + NSA_SLC_LEN)
                      - np.maximum(starts[:, None], sb[None, :]), 0, None) / NSA_CMP_LEN
    expand = (np.arange(seq)[None, :] // NSA_SLC_LEN == np.arange(n_slc)[:, None]).astype(np.float32)
    kern = functools.partial(_nsa_attn_kernel, tq=tq, tk=tk, seq=seq)
    cb = kv_col0
    kv_spec = lambda off: pl.BlockSpec((seq, LANES), lambda b, g, i: (b, cb + off * G + g))
    cmp_spec = lambda kv: pl.BlockSpec((None, None, None, n_cmp, HEAD_DIM), lambda b, g, i: (b, g, kv, 0, 0))
    return pl.pallas_call(
        kern,
        out_shape=jax.ShapeDtypeStruct((batch * seq, G * R * HEAD_DIM), BF16),
        grid=(batch, G, nq),
        in_specs=[pl.BlockSpec((tq, R * HEAD_DIM), lambda b, g, i: (b * nq + i, g)),
                  cmp_spec(0), cmp_spec(1), kv_spec(2), kv_spec(3), kv_spec(4), kv_spec(5),
                  pl.BlockSpec((None, tq, 3 * R), lambda b, g, i: (g, b * nq + i, 0)),
                  pl.BlockSpec((n_cmp, n_slc), lambda b, g, i: (0, 0)),
                  pl.BlockSpec((n_slc, seq), lambda b, g, i: (0, 0))],
        out_specs=pl.BlockSpec((tq, R * HEAD_DIM), lambda b, g, i: (b * nq + i, g)),
        compiler_params=_cparams(("arbitrary", "arbitrary", "arbitrary")),
        name="nsa_attention",
    )(zb, kvc, kvc, zb, zb, zb, zb, gates, jnp.asarray(overlap, BF16), jnp.asarray(expand, BF16))


def _flash_update(carry, s, v):
    m, l, acc = carry
    m_new = jnp.maximum(m, jnp.max(s, axis=1, keepdims=True))
    alpha = jnp.exp(m - m_new)
    p = jnp.exp(s - m_new)
    l = alpha * l + jnp.sum(p, axis=1, keepdims=True)
    acc = alpha * acc + _dot(p.astype(BF16), v)
    return m_new, l, acc


def _flash_init(tq, width):
    return (jnp.full((tq, 1), NEG, F32), jnp.zeros((tq, 1), F32), jnp.zeros((tq, width), F32))


def _causal_mask(tq):
    return lax.broadcasted_iota(jnp.int32, (tq, tq), 1) <= lax.broadcasted_iota(jnp.int32, (tq, tq), 0)


def _diff_attn_kernel(q_ref, k_ref, v_ref, lam_ref, subg_ref, o_ref, *, tq, lam_init):
    qi = pl.program_id(2)
    q = q_ref[...]
    q1, q2 = q[:, :HEAD_DIM], q[:, HEAD_DIM:]

    def tile(j, carry, diag):
        c1, c2 = carry
        k0 = pl.multiple_of(j * tq, tq)
        k = k_ref[pl.ds(k0, tq), :]
        v = v_ref[pl.ds(k0, tq), :]
        s1 = _dot_nt(q1, k[:, :HEAD_DIM])
        s2 = _dot_nt(q2, k[:, HEAD_DIM:])
        if diag:
            keep = _causal_mask(tq)
            s1 = jnp.where(keep, s1, NEG)
            s2 = jnp.where(keep, s2, NEG)
        return _flash_update(c1, s1, v), _flash_update(c2, s2, v)

    width = v_ref.shape[1]
    init = (_flash_init(tq, width), _flash_init(tq, width))
    carry = lax.fori_loop(0, qi, lambda j, c: tile(j, c, False), init)
    (_, l1, a1), (_, l2, a2) = tile(qi, carry, True)
    lp = lam_ref[...]
    lam = (jnp.exp(jnp.sum(lp[0:1] * lp[1:2], axis=1, keepdims=True))
           - jnp.exp(jnp.sum(lp[2:3] * lp[3:4], axis=1, keepdims=True)) + lam_init)
    o = a1 / jnp.maximum(l1, 1e-30) - lam * (a2 / jnp.maximum(l2, 1e-30))
    o_ref[...] = (_rms(o, subg_ref[...]) * (1.0 - lam_init)).astype(BF16)


def diff_attention(zb, batch, seq, q_col, k_col, v_col, diff_lam, sub_g, layer, *, tq=256):
    H = DIFF_HEADS
    W2 = 2 * HEAD_DIM
    tq = min(tq, seq)
    nq = seq // tq
    lam_init = 0.8 - 0.6 * math.exp(-0.3 * layer)
    kern = functools.partial(_diff_attn_kernel, tq=tq, lam_init=lam_init)
    return pl.pallas_call(
        kern,
        out_shape=jax.ShapeDtypeStruct((batch * seq, H * W2), BF16),
        grid=(batch, H, nq),
        in_specs=[pl.BlockSpec((tq, W2), lambda b, h, i: (b * nq + i, q_col + h)),
                  pl.BlockSpec((seq, W2), lambda b, h, i: (b, k_col + h)),
                  pl.BlockSpec((seq, W2), lambda b, h, i: (b, v_col + h)),
                  pl.BlockSpec((4, HEAD_DIM), lambda b, h, i: (0, 0)),
                  pl.BlockSpec((1, W2), lambda b, h, i: (0, 0))],
        out_specs=pl.BlockSpec((tq, W2), lambda b, h, i: (b * nq + i, h)),
        compiler_params=_cparams(("arbitrary", "arbitrary", "arbitrary")),
        name="diff_attention",
    )(zb, zb, zb, diff_lam, sub_g.reshape(1, W2))


def _fox_attn_kernel(q_ref, k_ref, v_ref, ccol_ref, crow_ref, o_ref, *, tq):
    h = pl.program_id(1)
    qi = pl.program_id(2)
    q = q_ref[...]
    lane = lax.broadcasted_iota(jnp.int32, ccol_ref.shape, 1)
    cq = jnp.sum(jnp.where(lane == h, ccol_ref[...], 0.0), axis=1, keepdims=True)

    def tile(j, carry, diag):
        k0 = pl.multiple_of(j * tq, tq)
        s = _dot_nt(q, k_ref[pl.ds(k0, tq), :]) + cq - crow_ref[:, pl.ds(k0, tq)]
        if diag:
            s = jnp.where(_causal_mask(tq), s, NEG)
        return _flash_update(carry, s, v_ref[pl.ds(k0, tq), :])

    carry = lax.fori_loop(0, qi, lambda j, c: tile(j, c, False), _flash_init(tq, HEAD_DIM))
    _, l, acc = tile(qi, carry, True)
    o_ref[...] = (acc / jnp.maximum(l, 1e-30)).astype(BF16)


def fox_attention(zb, c_tok, c_row, batch, seq, *, tq=512):
    H = FOX_HEADS
    tq = min(tq, seq)
    nq = seq // tq
    kern = functools.partial(_fox_attn_kernel, tq=tq)
    return pl.pallas_call(
        kern,
        out_shape=jax.ShapeDtypeStruct((batch * seq, H * HEAD_DIM), BF16),
        grid=(batch, H, nq),
        in_specs=[pl.BlockSpec((tq, LANES), lambda b, h, i: (b * nq + i, h)),
                  pl.BlockSpec((seq, LANES), lambda b, h, i: (b, H + h)),
                  pl.BlockSpec((seq, LANES), lambda b, h, i: (b, 2 * H + h)),
                  pl.BlockSpec((tq, H), lambda b, h, i: (b * nq + i, 0)),
                  pl.BlockSpec((None, 1, seq), lambda b, h, i: (b * H + h, 0, 0))],
        out_specs=pl.BlockSpec((tq, LANES), lambda b, h, i: (b * nq + i, h)),
        compiler_params=_cparams(("arbitrary", "arbitrary", "arbitrary")),
        name="fox_attention",
    )(zb, zb, zb, c_tok, c_row)


def _fox_gate_kernel(z_ref, b_ref, o_ref, carry_ref, *, tr):
    @pl.when(pl.program_id(1) == 0)
    def _():
        carry_ref[...] = jnp.zeros_like(carry_ref)

    logf = jax.nn.log_sigmoid(z_ref[...] + b_ref[...])
    tri = (lax.broadcasted_iota(jnp.int32, (tr, tr), 1) <= lax.broadcasted_iota(jnp.int32, (tr, tr), 0)).astype(BF16)
    p1 = logf.astype(BF16)
    r1 = logf - p1.astype(F32)
    p2 = r1.astype(BF16)
    p3 = (r1 - p2.astype(F32)).astype(BF16)
    c = _dot(tri, p1) + _dot(tri, p2) + _dot(tri, p3) + carry_ref[...]
    o_ref[...] = c
    carry_ref[...] = c[tr - 1:tr, :]


def fox_gate_cumsum(z, col_block, bias_row, batch, seq, *, tr=256):
    tr = min(tr, seq)
    nt = seq // tr
    kern = functools.partial(_fox_gate_kernel, tr=tr)
    return pl.pallas_call(
        kern,
        out_shape=jax.ShapeDtypeStruct((batch * seq, LANES), F32),
        grid=(batch, nt),
        in_specs=[pl.BlockSpec((tr, LANES), lambda b, i: (b * nt + i, col_block)),
                  pl.BlockSpec((1, LANES), lambda b, i: (0, 0))],
        out_specs=pl.BlockSpec((tr, LANES), lambda b, i: (b * nt + i, 0)),
        scratch_shapes=[pltpu.VMEM((1, LANES), F32)],
        compiler_params=_cparams(("arbitrary", "arbitrary")),
        name="fox_gate_cumsum",
    )(z, bias_row)


def _cross_attn_kernel(x_ref, g_ref, wq_ref, kv_ref, qg_ref, kg_ref, wo_ref, o_ref):
    x = x_ref[...]
    q = _dot(_rms(x, g_ref[...]).astype(BF16), wq_ref[...])
    kv = kv_ref[...]
    width = MEM_HEADS * HEAD_DIM
    outs = []
    for h in range(MEM_HEADS):
        cols = slice(h * HEAD_DIM, (h + 1) * HEAD_DIM)
        qh = _rms(q[:, cols], qg_ref[...]).astype(BF16)
        kh = _rms(kv[:, cols], kg_ref[...]).astype(BF16)
        vh = kv[:, width + h * HEAD_DIM: width + (h + 1) * HEAD_DIM].astype(BF16)
        p = _softmax_rows(_dot_nt(qh, kh))
        outs.append(_dot(p.astype(BF16), vh))
    o = jnp.concatenate(outs, axis=1).astype(BF16)
    o_ref[...] = x + _dot(o, wo_ref[...])


def cross_attention(x, g, wq, kv, q_gain_scaled, k_gain, wo, seq, mem_len, *, tm=512):
    M, D = x.shape
    tm = min(tm, seq)
    nt = seq // tm
    width = MEM_HEADS * HEAD_DIM
    return pl.pallas_call(
        _cross_attn_kernel,
        out_shape=jax.ShapeDtypeStruct((M, D), F32),
        grid=(M // tm,),
        in_specs=[pl.BlockSpec((tm, D), lambda i: (i, 0)),
                  pl.BlockSpec((1, D), lambda i: (0, 0)),
                  pl.BlockSpec((D, width), lambda i: (0, 0)),
                  pl.BlockSpec((mem_len, 2 * width), lambda i: (i // nt, 0)),
                  pl.BlockSpec((1, HEAD_DIM), lambda i: (0, 0)),
                  pl.BlockSpec((1, HEAD_DIM), lambda i: (0, 0)),
                  pl.BlockSpec((width, D), lambda i: (0, 0))],
        out_specs=pl.BlockSpec((tm, D), lambda i: (i, 0)),
        compiler_params=_cparams(("arbitrary",)),
        name="cross_attention",
    )(x, g.reshape(1, D), wq, kv, q_gain_scaled.reshape(1, HEAD_DIM), k_gain.reshape(1, HEAD_DIM), wo)


def _moe_router_kernel(x_ref, g_ref, wr_ref, br_ref, hn_ref, route_ref, onehot_ref):
    hn = _rms(x_ref[...], g_ref[...])
    hn_ref[...] = hn
    w = wr_ref[...]
    w_hi = w.astype(BF16)
    w_lo = (w - w_hi.astype(F32)).astype(BF16)
    h_hi = hn.astype(BF16)
    h_lo = (hn - h_hi.astype(F32)).astype(BF16)
    logits = _dot(h_hi, w_hi) + _dot(h_lo, w_hi) + _dot(h_hi, w_lo) + br_ref[...]
    tm = logits.shape[0]
    lane = lax.broadcasted_iota(jnp.int32, (tm, LANES), 1)
    is_grp = lane < N_GROUPS
    lg = jnp.where(is_grp, logits, -jnp.inf)
    eg = jnp.exp(lg - jnp.max(lg, axis=1, keepdims=True))
    p_grp = eg / jnp.sum(eg, axis=1, keepdims=True)
    p_top = jnp.max(p_grp, axis=1, keepdims=True)
    grp = jnp.min(jnp.where(is_grp & (p_grp == p_top), lane, LANES), axis=1, keepdims=True)
    lo = N_GROUPS + grp * EXPERTS_PER_GROUP
    in_grp = (lane >= lo) & (lane < lo + EXPERTS_PER_GROUP)
    le = jnp.where(in_grp, logits, -jnp.inf)
    ee = jnp.exp(le - jnp.max(le, axis=1, keepdims=True))
    p_in = ee / jnp.sum(ee, axis=1, keepdims=True)
    m1 = jnp.max(p_in, axis=1, keepdims=True)
    i1 = jnp.min(jnp.where(in_grp & (p_in == m1), lane, LANES), axis=1, keepdims=True)
    rest = jnp.where(in_grp & (lane != i1), p_in, -jnp.inf)
    m2 = jnp.max(rest, axis=1, keepdims=True)
    i2 = jnp.min(jnp.where(rest == m2, lane, LANES), axis=1, keepdims=True)
    denom = m1 + m2
    w1 = p_top * m1 / denom
    w2 = p_top * m2 / denom
    e1 = (i1 - N_GROUPS).astype(F32)
    e2 = (i2 - N_GROUPS).astype(F32)
    route_ref[...] = jnp.where(lane == 0, e1, jnp.where(lane == 1, e2, jnp.where(lane == 2, w1, jnp.where(lane == 3, w2, 0.0))))
    onehot_ref[...] = ((lane == i1 - N_GROUPS) | (lane == i2 - N_GROUPS)).astype(BF16)


def moe_router(x, g, w_router, b_router, *, tm=512):
    M, D = x.shape
    tm = min(tm, M)
    return pl.pallas_call(
        _moe_router_kernel,
        out_shape=(jax.ShapeDtypeStruct((M, D), F32),
                   jax.ShapeDtypeStruct((M, LANES), F32),
                   jax.ShapeDtypeStruct((M, LANES), BF16)),
        grid=(M // tm,),
        in_specs=[pl.BlockSpec((tm, D), lambda i: (i, 0)),
                  pl.BlockSpec((1, D), lambda i: (0, 0)),
                  pl.BlockSpec((D, LANES), lambda i: (0, 0)),
                  pl.BlockSpec((1, LANES), lambda i: (0, 0))],
        out_specs=(pl.BlockSpec((tm, D), lambda i: (i, 0)),
                   pl.BlockSpec((tm, LANES), lambda i: (i, 0)),
                   pl.BlockSpec((tm, LANES), lambda i: (i, 0))),
        compiler_params=_cparams(("arbitrary",)),
        name="moe_router",
    )(x, g.reshape(1, D), w_router, b_router)


def _moe_rank_kernel(onehot_ref, rank_ref, count_ref, carry_ref, *, tr):
    i = pl.program_id(0)

    @pl.when(i == 0)
    def _():
        carry_ref[...] = jnp.zeros_like(carry_ref)

    oh = onehot_ref[...]
    tri = (lax.broadcasted_iota(jnp.int32, (tr, tr), 1) < lax.broadcasted_iota(jnp.int32, (tr, tr), 0)).astype(BF16)
    rank_ref[...] = _dot(tri, oh) + carry_ref[...]
    carry_ref[...] = carry_ref[...] + jnp.sum(oh.astype(F32), axis=0, keepdims=True)
    count_ref[...] = carry_ref[...]


def moe_rank(onehot, *, tr=512):
    M = onehot.shape[0]
    tr = min(tr, M)
    kern = functools.partial(_moe_rank_kernel, tr=tr)
    return pl.pallas_call(
        kern,
        out_shape=(jax.ShapeDtypeStruct((M, LANES), F32), jax.ShapeDtypeStruct((1, LANES), F32)),
        grid=(M // tr,),
        in_specs=[pl.BlockSpec((tr, LANES), lambda i: (i, 0))],
        out_specs=(pl.BlockSpec((tr, LANES), lambda i: (i, 0)), pl.BlockSpec((1, LANES), lambda i: (0, 0))),
        scratch_shapes=[pltpu.VMEM((1, LANES), F32)],
        compiler_params=_cparams(("arbitrary",)),
        name="moe_rank",
    )(onehot)


def _moe_slot_kernel(route_ref, rank_ref, pstart_ref, o_ref):
    route = route_ref[...]
    slot = rank_ref[...] + pstart_ref[...]
    lane = lax.broadcasted_iota(jnp.int32, route.shape, 1)
    e1 = route[:, 0:1].astype(jnp.int32)
    e2 = route[:, 1:2].astype(jnp.int32)
    d1 = jnp.sum(jnp.where(lane == e1, slot, 0.0), axis=1, keepdims=True)
    d2 = jnp.sum(jnp.where(lane == e2, slot, 0.0), axis=1, keepdims=True)
    o_ref[...] = jnp.where(lane == 0, d1, jnp.where(lane == 1, d2, 0.0)).astype(jnp.int32)


def moe_slots(route, rank, pstart_row, *, tm=1024):
    M = route.shape[0]
    tm = min(tm, M)
    return pl.pallas_call(
        _moe_slot_kernel,
        out_shape=jax.ShapeDtypeStruct((M, LANES), jnp.int32),
        grid=(M // tm,),
        in_specs=[pl.BlockSpec((tm, LANES), lambda i: (i, 0)),
                  pl.BlockSpec((tm, LANES), lambda i: (i, 0)),
                  pl.BlockSpec((1, LANES), lambda i: (0, 0))],
        out_specs=pl.BlockSpec((tm, LANES), lambda i: (i, 0)),
        compiler_params=_cparams(("arbitrary",)),
        name="moe_slots",
    )(route, rank, pstart_row)


def _moe_dispatch_kernel(lastblk_ref, d1_hbm, d2_hbm, hn_hbm, xb_hbm, zero_ref, s1_ref, s2_ref,
                         idx_sem, row_sem, zero_sem, *, tm, n_blocks):
    i = pl.program_id(0)

    def zero_copy(blk):
        return pltpu.make_async_copy(zero_ref, xb_hbm.at[pl.ds(blk * MOE_ROWS, MOE_ROWS)], zero_sem)

    @pl.when(i == 0)
    def _():
        zero_ref[...] = jnp.zeros_like(zero_ref)
        n_used = lastblk_ref[N_EXPERTS]
        for e in range(N_EXPERTS):
            @pl.when(lastblk_ref[e] >= 0)
            def _():
                zero_copy(lastblk_ref[e]).start()

        def start_tail(blk, _):
            zero_copy(blk).start()
            return 0

        def wait_tail(blk, _):
            zero_copy(blk).wait()
            return 0

        lax.fori_loop(n_used, n_blocks, start_tail, 0)
        for e in range(N_EXPERTS):
            @pl.when(lastblk_ref[e] >= 0)
            def _():
                zero_copy(lastblk_ref[e]).wait()
        lax.fori_loop(n_used, n_blocks, wait_tail, 0)

    c1 = pltpu.make_async_copy(d1_hbm.at[i], s1_ref, idx_sem.at[0])
    c2 = pltpu.make_async_copy(d2_hbm.at[i], s2_ref, idx_sem.at[1])
    c1.start()
    c2.start()
    c1.wait()
    c2.wait()

    def row_copy(r, slot):
        return pltpu.make_async_copy(hn_hbm.at[pl.ds(i * tm + r, 1)], xb_hbm.at[pl.ds(slot, 1)], row_sem)

    def issue(r, _):
        row_copy(r, s1_ref[r]).start()
        row_copy(r, s2_ref[r]).start()
        return 0

    lax.fori_loop(0, tm, issue, 0, unroll=8)

    def drain(r, _):
        row_copy(r, 0).wait()
        row_copy(r, 0).wait()
        return 0

    lax.fori_loop(0, tm, drain, 0, unroll=8)


def moe_dispatch(hn, d1, d2, lastblk, n_slots, *, tm=512):
    M, D = hn.shape
    tm = min(tm, M)
    assert n_slots % MOE_ROWS == 0
    kern = functools.partial(_moe_dispatch_kernel, tm=tm, n_blocks=n_slots // MOE_ROWS)
    grid_spec = pltpu.PrefetchScalarGridSpec(
        num_scalar_prefetch=1,
        grid=(M // tm,),
        in_specs=[pl.BlockSpec(memory_space=pl.ANY)] * 3,
        out_specs=pl.BlockSpec(memory_space=pl.ANY),
        scratch_shapes=[pltpu.VMEM((MOE_ROWS, D), F32),
                        pltpu.SMEM((tm,), jnp.int32), pltpu.SMEM((tm,), jnp.int32),
                        pltpu.SemaphoreType.DMA((2,)), pltpu.SemaphoreType.DMA, pltpu.SemaphoreType.DMA],
    )
    return pl.pallas_call(
        kern,
        out_shape=jax.ShapeDtypeStruct((n_slots, D), F32),
        grid_spec=grid_spec,
        compiler_params=_cparams(("arbitrary",)),
        name="moe_dispatch",
    )(lastblk, d1.reshape(M // tm, tm), d2.reshape(M // tm, tm), hn)


def _moe_expert_kernel(blk_e_ref, n_used_ref, x_ref, w1_ref, w3_ref, w2_ref, o_ref, w1b, w3b, w2b):
    i = pl.program_id(0)

    @pl.when(i < n_used_ref[0])
    def _():
        prev = blk_e_ref[jnp.maximum(i - 1, 0)]

        @pl.when((i == 0) | (blk_e_ref[i] != prev))
        def _():
            w1b[...] = w1_ref[...].astype(BF16)
            w3b[...] = w3_ref[...].astype(BF16)
            w2b[...] = w2_ref[...].astype(BF16)

        x = x_ref[...].astype(BF16)
        h = (jax.nn.silu(_dot(x, w1b[...])) * _dot(x, w3b[...])).astype(BF16)
        o_ref[...] = _dot(h, w2b[...])

    @pl.when(i >= n_used_ref[0])
    def _():
        o_ref[...] = jnp.zeros_like(o_ref)


def moe_experts(xb, blk_e, n_used, w1, w3, w2):
    P, D = xb.shape
    FF = w1.shape[-1]
    n_blk = P // MOE_ROWS

    def row_map(i, blk_e_ref, n_used_ref):
        return (jnp.minimum(i, n_used_ref[0] - 1), 0)

    def out_map(i, blk_e_ref, n_used_ref):
        return (i, 0)

    def w_map(i, blk_e_ref, n_used_ref):
        return (blk_e_ref[i], 0, 0)

    grid_spec = pltpu.PrefetchScalarGridSpec(
        num_scalar_prefetch=2,
        grid=(n_blk,),
        in_specs=[pl.BlockSpec((MOE_ROWS, D), row_map),
                  pl.BlockSpec((None, D, FF), w_map),
                  pl.BlockSpec((None, D, FF), w_map),
                  pl.BlockSpec((None, FF, D), w_map)],
        out_specs=pl.BlockSpec((MOE_ROWS, D), out_map),
        scratch_shapes=[pltpu.VMEM((D, FF), BF16), pltpu.VMEM((D, FF), BF16), pltpu.VMEM((FF, D), BF16)],
    )
    return pl.pallas_call(
        _moe_expert_kernel,
        out_shape=jax.ShapeDtypeStruct((P, D), F32),
        grid_spec=grid_spec,
        compiler_params=_cparams(("arbitrary",)),
        name="moe_experts",
    )(blk_e, n_used, xb, w1, w3, w2)


def _moe_combine_kernel(x_ref, route_ref, d1_hbm, d2_hbm, yb_hbm, o_ref, y1_ref, y2_ref, s1_ref, s2_ref,
                        idx_sem, row_sem, *, tm):
    i = pl.program_id(0)
    c1 = pltpu.make_async_copy(d1_hbm.at[i], s1_ref, idx_sem.at[0])
    c2 = pltpu.make_async_copy(d2_hbm.at[i], s2_ref, idx_sem.at[1])
    c1.start()
    c2.start()
    c1.wait()
    c2.wait()

    def row_copy(slot, dst, r):
        return pltpu.make_async_copy(yb_hbm.at[pl.ds(slot, 1)], dst.at[pl.ds(r, 1)], row_sem)

    def issue(r, _):
        row_copy(s1_ref[r], y1_ref, r).start()
        row_copy(s2_ref[r], y2_ref, r).start()
        return 0

    lax.fori_loop(0, tm, issue, 0, unroll=8)

    def drain(r, _):
        row_copy(0, y1_ref, r).wait()
        row_copy(0, y2_ref, r).wait()
        return 0

    lax.fori_loop(0, tm, drain, 0, unroll=8)
    route = route_ref[...]
    o_ref[...] = x_ref[...] + (y1_ref[...] * route[:, 2:3] + y2_ref[...] * route[:, 3:4])


def moe_combine(x, route, d1, d2, yb, *, tm=256):
    M, D = x.shape
    tm = min(tm, M)
    kern = functools.partial(_moe_combine_kernel, tm=tm)
    return pl.pallas_call(
        kern,
        out_shape=jax.ShapeDtypeStruct((M, D), F32),
        grid=(M // tm,),
        in_specs=[pl.BlockSpec((tm, D), lambda i: (i, 0)),
                  pl.BlockSpec((tm, LANES), lambda i: (i, 0)),
                  pl.BlockSpec(memory_space=pl.ANY),
                  pl.BlockSpec(memory_space=pl.ANY),
                  pl.BlockSpec(memory_space=pl.ANY)],
        out_specs=pl.BlockSpec((tm, D), lambda i: (i, 0)),
        scratch_shapes=[pltpu.VMEM((tm, D), F32), pltpu.VMEM((tm, D), F32),
                        pltpu.SMEM((tm,), jnp.int32), pltpu.SMEM((tm,), jnp.int32),
                        pltpu.SemaphoreType.DMA((2,)), pltpu.SemaphoreType.DMA],
        compiler_params=_cparams(("arbitrary",)),
        name="moe_combine",
    )(x, route, d1.reshape(M // tm, tm), d2.reshape(M // tm, tm), yb)


def hier_moe(x, g, wg, bg, we, be, w1, w3, w2):
    M, D = x.shape
    n_route = N_GROUPS + N_EXPERTS
    w_router = jnp.pad(jnp.concatenate([wg, we], axis=1), ((0, 0), (0, LANES - n_route)))
    b_router = jnp.pad(jnp.concatenate([bg, be]), (0, LANES - n_route)).reshape(1, LANES)
    hn, route, onehot = moe_router(x, g, w_router, b_router)
    rank, counts = moe_rank(onehot)
    cnt = counts[0, :N_EXPERTS].astype(jnp.int32)
    nblk = (cnt + MOE_ROWS - 1) // MOE_ROWS
    bend = jnp.cumsum(nblk)
    bstart = bend - nblk
    n_blk_max = (2 * M) // MOE_ROWS + N_EXPERTS
    pstart_row = jnp.pad((bstart * MOE_ROWS).astype(F32), (0, LANES - N_EXPERTS)).reshape(1, LANES)
    slots = moe_slots(route, rank, pstart_row)
    d1, d2 = slots[:, 0], slots[:, 1]
    n_used = bend[-1:].astype(jnp.int32)
    lastblk = jnp.concatenate([jnp.where(nblk > 0, bend - 1, -1).astype(jnp.int32), n_used])
    blk_e = jnp.minimum(jnp.searchsorted(bend, jnp.arange(n_blk_max, dtype=jnp.int32), side="right"),
                        N_EXPERTS - 1).astype(jnp.int32)
    xb = moe_dispatch(hn, d1, d2, lastblk, n_blk_max * MOE_ROWS)
    yb = moe_experts(xb, blk_e, n_used, w1, w3, w2)
    return moe_combine(x, route, d1, d2, yb)


def _even_mixer(x, g, w_in, w_out, nsa_qk_g, cmp_pe, cmp_w1, cmp_w2, diff_qk_g, diff_lam, diff_sub_g,
                layer, batch, seq, cos, sin):
    M, D = x.shape
    scale = HEAD_DIM ** -0.5
    nq_w = NSA_HEADS * HEAD_DIM
    kv_w = 6 * NSA_GROUPS * HEAD_DIM
    gate_w = 3 * NSA_HEADS
    dq_w = 2 * DIFF_HEADS * HEAD_DIM
    dv_w = DIFF_HEADS * 2 * HEAD_DIM
    c_gate = nq_w + kv_w
    c_diff = c_gate + gate_w
    main_w = nq_w + kv_w + 2 * dq_w + dv_w
    w_cat = jnp.concatenate([w_in[:, :c_gate], w_in[:, c_diff:], w_in[:, c_gate:c_diff],
                             jnp.zeros((D, LANES - gate_w), w_in.dtype)], axis=1).astype(BF16)
    z = rms_matmul(x, g, w_cat, tn=(main_w + LANES) // 9, name="even_in_proj")
    n_blocks = main_w // LANES
    b_q, b_kv, b_dq, b_dk, b_dv = 0, nq_w // LANES, (nq_w + kv_w) // LANES, (nq_w + kv_w + dq_w) // LANES, \
        (nq_w + kv_w + 2 * dq_w) // LANES
    G = NSA_GROUPS
    gidx = np.zeros(n_blocks, np.int32)
    norm = np.zeros(n_blocks, np.int32)
    rope = np.zeros(n_blocks, np.int32)
    gidx[b_q:b_kv], norm[b_q:b_kv], rope[b_q:b_kv] = 0, 1, 1
    gidx[b_kv + 2 * G:b_kv + 3 * G], norm[b_kv + 2 * G:b_kv + 3 * G], rope[b_kv + 2 * G:b_kv + 3 * G] = 1, 1, 1
    gidx[b_kv + 4 * G:b_kv + 5 * G], norm[b_kv + 4 * G:b_kv + 5 * G], rope[b_kv + 4 * G:b_kv + 5 * G] = 2, 1, 1
    gidx[b_dq:b_dk], norm[b_dq:b_dk], rope[b_dq:b_dk] = 3, 1, 1
    gidx[b_dk:b_dv], norm[b_dk:b_dv], rope[b_dk:b_dv] = 4, 1, 1
    gains = jnp.stack([nsa_qk_g[0] * scale, nsa_qk_g[2], nsa_qk_g[3], diff_qk_g[0] * scale, diff_qk_g[1],
                       jnp.ones((HEAD_DIM,), F32), jnp.ones((HEAD_DIM,), F32), jnp.ones((HEAD_DIM,), F32)])
    zb = head_prep(z, n_blocks, gidx, norm, rope, gains, cos, sin, seq, name="even_head_prep")
    cmp_end = np.arange(seq // NSA_CMP_STRIDE) * NSA_CMP_STRIDE + NSA_CMP_LEN - 1
    cos_c, sin_c = rope_tables(cmp_end)
    kvc = nsa_compress(z, b_kv, batch, seq, cmp_pe, cmp_w1.astype(BF16), cmp_w2.astype(BF16), nsa_qk_g[1],
                       cos_c, sin_c)
    gates = z[:, main_w:main_w + gate_w].reshape(M, G, 3 * NSA_REP).transpose(1, 0, 2)
    o_nsa = nsa_attention(zb, kvc, gates, batch, seq, b_kv)
    o_diff = diff_attention(zb, batch, seq, b_dq // 2, b_dk // 2, b_dv // 2, diff_lam, diff_sub_g, layer)
    return matmul_residual([o_nsa, o_diff], w_out.astype(BF16), x, name="even_out_proj")


def _odd_mixer(x, g, w_in, w_out, f_b, qk_g, batch, seq, cos, sin):
    M, D = x.shape
    H = FOX_HEADS
    scale = HEAD_DIM ** -0.5
    width = H * HEAD_DIM
    w_cat = jnp.pad(w_in, ((0, 0), (0, LANES - H))).astype(BF16)
    z = rms_matmul(x, g, w_cat, tn=(3 * width + LANES) // 7, name="odd_in_proj")
    n_blocks = 3 * H
    gidx = np.zeros(n_blocks, np.int32)
    norm = np.zeros(n_blocks, np.int32)
    rope = np.zeros(n_blocks, np.int32)
    gidx[H:2 * H] = 1
    norm[:2 * H] = 1
    gains = jnp.concatenate([jnp.stack([qk_g[0] * scale, qk_g[1]]), jnp.ones((6, HEAD_DIM), F32)])
    zb = head_prep(z, n_blocks, gidx, norm, rope, gains, cos, sin, seq, name="odd_head_prep")
    bias_row = jnp.pad(f_b, (0, LANES - H)).reshape(1, LANES)
    c = fox_gate_cumsum(z, n_blocks, bias_row, batch, seq)
    c_tok = c[:, :H]
    c_row = c_tok.reshape(batch, seq, H).transpose(0, 2, 1).reshape(batch * H, 1, seq)
    o = fox_attention(zb, c_tok, c_row, batch, seq)
    return matmul_residual([o], w_out.astype(BF16), x, name="odd_out_proj")


def kernel(x, mem, norm_g, ev_w_in, ev_w_out, nsa_qk_g, nsa_cmp_pe, nsa_cmp_w1, nsa_cmp_w2, diff_qk_g, diff_lam, diff_sub_g, od_w_in, od_w_out, fox_f_b, fox_qk_g, ca_wq, ca_wkv, ca_qk_g, ca_wo, moe_wg, moe_bg, moe_we, moe_be, moe_w1, moe_w3, moe_w2):
    B, T, D = x.shape
    mem_len = mem.shape[1]
    depth = norm_g.shape[0]
    scale = HEAD_DIM ** -0.5
    cos, sin = rope_tables(np.arange(T))
    xt = x.reshape(B * T, D)
    mt = mem.reshape(B * mem_len, D)
    for layer in range(depth):
        i = layer // 2
        if layer % 2 == 0:
            xt = _even_mixer(xt, norm_g[layer, 0], ev_w_in[i], ev_w_out[i], nsa_qk_g[i], nsa_cmp_pe[i],
                             nsa_cmp_w1[i], nsa_cmp_w2[i], diff_qk_g[i], diff_lam[i], diff_sub_g[i], layer,
                             B, T, cos, sin)
        else:
            xt = _odd_mixer(xt, norm_g[layer, 0], od_w_in[i], od_w_out[i], fox_f_b[i], fox_qk_g[i], B, T, cos, sin)
        kv = rms_matmul(mt, norm_g[layer, 2], ca_wkv[layer].astype(BF16), name="mem_kv_proj")
        xt = cross_attention(xt, norm_g[layer, 1], ca_wq[layer].astype(BF16), kv, ca_qk_g[layer, 0] * scale,
                             ca_qk_g[layer, 1], ca_wo[layer].astype(BF16), T, mem_len)
        xt = hier_moe(xt, norm_g[layer, 3], moe_wg[layer], moe_bg[layer], moe_we[layer], moe_be[layer],
                      moe_w1[layer], moe_w3[layer], moe_w2[layer])
    return xt.reshape(B, T, D)
```

```python
import functools
import math

import numpy as np
import jax
import jax.numpy as jnp
from jax import lax
from jax.experimental import pallas as pl
from jax.experimental.pallas import tpu as pltpu

F32 = jnp.float32
BF16 = jnp.bfloat16

HEAD_DIM = 128
ROPE_THETA = 10000.0
EPS = 1e-6
NEG = -1e30

NSA_HEADS = 8
NSA_GROUPS = 2
NSA_REP = NSA_HEADS // NSA_GROUPS
NSA_CMP_LEN = 32
NSA_CMP_STRIDE = 16
NSA_CMP_HIDDEN = 256
NSA_SLC_LEN = 64
NSA_SLC_TOPK = 8
NSA_WINDOW = 512
NSA_QBLOCK = 128
DIFF_HEADS = 4
FOX_HEADS = 16
MEM_HEADS = 4
N_GROUPS = 4
EXPERTS_PER_GROUP = 8
N_EXPERTS = N_GROUPS * EXPERTS_PER_GROUP
MOE_ROWS = 256
LANES = 128
VMEM_LIMIT = 56 * 1024 * 1024


def _cparams(sem):
    return pltpu.CompilerParams(dimension_semantics=sem, vmem_limit_bytes=VMEM_LIMIT)


def _dot(a, b):
    return jnp.dot(a, b, preferred_element_type=F32)


def _dot_nt(a, b):
    return lax.dot_general(a, b, (((1,), (1,)), ((), ())), preferred_element_type=F32)


def _rms(x, g):
    ms = jnp.mean(x * x, axis=-1, keepdims=True)
    return x * lax.rsqrt(ms + EPS) * g


def _rope(y, cos, sin_signed):
    return y * cos + pltpu.roll(y, HEAD_DIM // 2, 1) * sin_signed


def _rms_matmul_kernel(x_ref, g_ref, w_ref, o_ref, xn_ref):
    @pl.when(pl.program_id(1) == 0)
    def _():
        xn_ref[...] = _rms(x_ref[...], g_ref[...]).astype(BF16)

    o_ref[...] = _dot(xn_ref[...], w_ref[...])


def rms_matmul(x, g, w, *, tm=1024, tn=512, name="rms_matmul"):
    M, K = x.shape
    N = w.shape[1]
    tm, tn = min(tm, M), min(tn, N)
    assert M % tm == 0 and N % tn == 0
    return pl.pallas_call(
        _rms_matmul_kernel,
        out_shape=jax.ShapeDtypeStruct((M, N), F32),
        grid=(M // tm, N // tn),
        in_specs=[pl.BlockSpec((tm, K), lambda i, j: (i, 0)),
                  pl.BlockSpec((1, K), lambda i, j: (0, 0)),
                  pl.BlockSpec((K, tn), lambda i, j: (0, j))],
        out_specs=pl.BlockSpec((tm, tn), lambda i, j: (i, j)),
        scratch_shapes=[pltpu.VMEM((tm, K), BF16)],
        compiler_params=_cparams(("arbitrary", "arbitrary")),
        name=name,
    )(x, g.reshape(1, K), w)


def _matmul_res_kernel(*refs):
    *x_refs, w_ref, r_ref, o_ref = refs
    acc = r_ref[...]
    k0 = 0
    for x_ref in x_refs:
        k = x_ref.shape[1]
        acc = acc + _dot(x_ref[...], w_ref[k0:k0 + k, :])
        k0 += k
    o_ref[...] = acc


def matmul_residual(xs, w, res, *, tm=1024, tn=512, name="matmul_residual"):
    M = xs[0].shape[0]
    N = w.shape[1]
    tm, tn = min(tm, M), min(tn, N)
    assert M % tm == 0 and N % tn == 0 and sum(x.shape[1] for x in xs) == w.shape[0]
    return pl.pallas_call(
        _matmul_res_kernel,
        out_shape=jax.ShapeDtypeStruct((M, N), F32),
        grid=(M // tm, N // tn),
        in_specs=[pl.BlockSpec((tm, x.shape[1]), lambda i, j: (i, 0)) for x in xs]
        + [pl.BlockSpec((w.shape[0], tn), lambda i, j: (0, j)),
           pl.BlockSpec((tm, tn), lambda i, j: (i, j))],
        out_specs=pl.BlockSpec((tm, tn), lambda i, j: (i, j)),
        compiler_params=_cparams(("arbitrary", "arbitrary")),
        name=name,
    )(*xs, w, res)


def _head_prep_kernel(z_ref, gains_ref, cos_ref, sin_ref, o_ref, *, modes):
    for j, (gain_row, rotary) in enumerate(modes):
        cols = slice(j * LANES, (j + 1) * LANES)
        y = z_ref[:, cols]
        if gain_row is not None:
            y = _rms(y, gains_ref[gain_row:gain_row + 1, :])
        if rotary:
            y = _rope(y, cos_ref[...], sin_ref[...])
        o_ref[:, cols] = y.astype(BF16)


def head_prep(z, modes, gains, cos, sin, seq, *, tm=256, name="head_prep"):
    M, W = z.shape
    tm = min(tm, seq)
    assert seq % tm == 0 and M % tm == 0
    nt = seq // tm
    n_out = len(modes) * LANES
    kern = functools.partial(_head_prep_kernel, modes=tuple(modes))
    return pl.pallas_call(
        kern,
        out_shape=jax.ShapeDtypeStruct((M, n_out), BF16),
        grid=(M // tm,),
        in_specs=[pl.BlockSpec((tm, W), lambda i: (i, 0)),
                  pl.BlockSpec(gains.shape, lambda i: (0, 0)),
                  pl.BlockSpec((tm, LANES), lambda i: (i % nt, 0)),
                  pl.BlockSpec((tm, LANES), lambda i: (i % nt, 0))],
        out_specs=pl.BlockSpec((tm, n_out), lambda i: (i, 0)),
        compiler_params=_cparams(("arbitrary",)),
        name=name,
    )(z, gains, cos, sin)


def rope_tables(pos):
    half = HEAD_DIM // 2
    inv = ROPE_THETA ** (-jnp.arange(half, dtype=F32) / half)
    ang = jnp.asarray(pos).astype(F32)[:, None] * inv[None, :]
    cos, sin = jnp.cos(ang), jnp.sin(ang)
    return jnp.concatenate([cos, cos], axis=-1), jnp.concatenate([-sin, sin], axis=-1)


def _nsa_compress_kernel(t_ref, pe_ref, w1_ref, w2_ref, g_ref, cos_ref, sin_ref, o_ref, *, n_rows):
    kv = pl.program_id(2)
    half = NSA_CMP_LEN // 2
    hid = w1_ref.shape[-1]
    lo = jnp.zeros((n_rows, hid), F32)
    hi = jnp.zeros((n_rows, hid), F32)
    for l in range(NSA_CMP_LEN):
        rows = t_ref[pl.ds(l % half, n_rows, stride=NSA_CMP_STRIDE), :]
        a = (rows + pe_ref[l:l + 1, :]).astype(BF16)
        part = _dot(a, w1_ref[l * HEAD_DIM:(l + 1) * HEAD_DIM, :])
        if l < half:
            lo = lo + part
        else:
            hi = hi + part
    pre = lo + pltpu.roll(hi, n_rows - 1, 0)
    out = _dot(jax.nn.gelu(pre).astype(BF16), w2_ref[...])
    keyed = _rope(_rms(out, g_ref[...]), cos_ref[...], sin_ref[...])
    o_ref[...] = jnp.where(kv == 0, keyed, out).astype(BF16)


def nsa_compress(z, col0, batch, seq, pe, w1, w2, g_kc, cos_c, sin_c):
    n_rows = seq // NSA_CMP_STRIDE
    G = NSA_GROUPS
    kern = functools.partial(_nsa_compress_kernel, n_rows=n_rows)
    return pl.pallas_call(
        kern,
        out_shape=jax.ShapeDtypeStruct((batch, G, 2, n_rows, HEAD_DIM), BF16),
        grid=(batch, G, 2),
        in_specs=[pl.BlockSpec((seq, LANES), lambda b, g, kv: (b, col0 + kv * G + g)),
                  pl.BlockSpec((None, NSA_CMP_LEN, HEAD_DIM), lambda b, g, kv: (kv, 0, 0)),
                  pl.BlockSpec((None,) + w1.shape[1:], lambda b, g, kv: (kv, 0, 0)),
                  pl.BlockSpec((None,) + w2.shape[1:], lambda b, g, kv: (kv, 0, 0)),
                  pl.BlockSpec((1, HEAD_DIM), lambda b, g, kv: (0, 0)),
                  pl.BlockSpec((n_rows, HEAD_DIM), lambda b, g, kv: (0, 0)),
                  pl.BlockSpec((n_rows, HEAD_DIM), lambda b, g, kv: (0, 0))],
        out_specs=pl.BlockSpec((None, None, None, n_rows, HEAD_DIM), lambda b, g, kv: (b, g, kv, 0, 0)),
        compiler_params=_cparams(("arbitrary", "arbitrary", "arbitrary")),
        name="nsa_compress",
    )(z, pe, w1, w2, g_kc.reshape(1, HEAD_DIM), cos_c, sin_c)


def _softmax_rows(s):
    m = jnp.max(s, axis=1, keepdims=True)
    e = jnp.exp(s - m)
    return e / jnp.maximum(jnp.sum(e, axis=1, keepdims=True), 1e-30)


def _nsa_attn_kernel(q_ref, kc_ref, vc_ref, ks_ref, vs_ref, kw_ref, vw_ref, gate_ref,
                     ov_ref, expand_ref, o_ref, *, tq, tk, seq):
    R = NSA_REP
    qi = pl.program_id(2)
    q0 = qi * tq
    n_cmp = kc_ref.shape[0]
    n_slc = seq // NSA_SLC_LEN
    q = q_ref[...]
    qs = jnp.concatenate([q[:, r * HEAD_DIM:(r + 1) * HEAD_DIM] for r in range(R)], axis=0)
    qpos1 = q0 + lax.broadcasted_iota(jnp.int32, (tq, 1), 0)
    qpos = jnp.concatenate([qpos1] * R, axis=0)

    cmp_end = (lax.broadcasted_iota(jnp.int32, (1, n_cmp), 1) * NSA_CMP_STRIDE + (NSA_CMP_LEN - 1))
    s_c = jnp.where(cmp_end <= qpos, _dot_nt(qs, kc_ref[...]), NEG)
    m_c = jnp.max(s_c, axis=1, keepdims=True)
    e_c = jnp.where(cmp_end <= qpos, jnp.exp(s_c - m_c), 0.0)
    p_c = e_c / jnp.maximum(jnp.sum(e_c, axis=1, keepdims=True), 1e-30)
    o_c = _dot(p_c.astype(BF16), vc_ref[...])

    p_sum = p_c[0:tq]
    for r in range(1, R):
        p_sum = p_sum + p_c[r * tq:(r + 1) * tq]
    p_hi = p_sum.astype(BF16)
    p_lo = (p_sum - p_hi.astype(F32)).astype(BF16)
    imp = _dot(p_hi, ov_ref[...]) + _dot(p_lo, ov_ref[...])
    jblk = lax.broadcasted_iota(jnp.int32, (tq, n_slc), 1)
    qblk = qpos1 // NSA_SLC_LEN
    forced = (jblk == 0) | (jblk == qblk) | (jblk == qblk - 1)
    imp = jnp.where(jblk > qblk, -jnp.inf, jnp.where(forced, jnp.inf, imp))
    picked = jnp.zeros((tq, n_slc), jnp.int32)
    for _ in range(min(NSA_SLC_TOPK, n_slc)):
        cand = jnp.where(picked > 0, -jnp.inf, imp)
        best = jnp.max(cand, axis=1, keepdims=True)
        first = jnp.min(jnp.where(cand == best, jblk + picked * n_slc, n_slc), axis=1, keepdims=True)
        picked = jnp.where(jblk == first, 1, picked)
    sel_bias = jnp.where(picked > 0, 0.0, NEG).astype(BF16)

    def slc_tile(j, carry, causal):
        m, l, acc = carry
        k0 = pl.multiple_of(j * tk, tk)
        s = _dot_nt(qs, ks_ref[pl.ds(k0, tk), :])
        bias = _dot(sel_bias, expand_ref[:, pl.ds(k0, tk)])
        if causal:
            kpos = k0 + lax.broadcasted_iota(jnp.int32, (1, tk), 1)
            bias = jnp.where(kpos <= qpos1, bias, NEG)
        s = (s.reshape(R, tq, tk) + bias[None]).reshape(R * tq, tk)
        m_new = jnp.maximum(m, jnp.max(s, axis=1, keepdims=True))
        alpha = jnp.exp(m - m_new)
        p = jnp.exp(s - m_new)
        l = alpha * l + jnp.sum(p, axis=1, keepdims=True)
        acc = alpha * acc + _dot(p.astype(BF16), vs_ref[pl.ds(k0, tk), :])
        return m_new, l, acc

    init = (jnp.full((R * tq, 1), NEG, F32), jnp.zeros((R * tq, 1), F32), jnp.zeros((R * tq, HEAD_DIM), F32))
    last = q0 // tk
    carry = lax.fori_loop(0, last, lambda j, c: slc_tile(j, c, False), init)
    _, l_s, acc_s = slc_tile(last, carry, True)
    o_s = acc_s / jnp.maximum(l_s, 1e-30)

    span = NSA_WINDOW + tq
    w0 = pl.multiple_of(jnp.maximum(q0 - NSA_WINDOW, 0), tq)
    wpos = w0 + lax.broadcasted_iota(jnp.int32, (1, span), 1)
    rel = qpos - wpos
    ok_w = (rel >= 0) & (rel < NSA_WINDOW)
    s_w = jnp.where(ok_w, _dot_nt(qs, kw_ref[pl.ds(w0, span), :]), NEG)
    m_w = jnp.max(s_w, axis=1, keepdims=True)
    e_w = jnp.where(ok_w, jnp.exp(s_w - m_w), 0.0)
    p_w = e_w / jnp.maximum(jnp.sum(e_w, axis=1, keepdims=True), 1e-30)
    o_w = _dot(p_w.astype(BF16), vw_ref[pl.ds(w0, span), :])

    gates = jax.nn.sigmoid(gate_ref[...])
    outs = []
    for r in range(R):
        rows = slice(r * tq, (r + 1) * tq)
        outs.append(gates[:, 3 * r:3 * r + 1] * o_c[rows] + gates[:, 3 * r + 1:3 * r + 2] * o_s[rows]
                    + gates[:, 3 * r + 2:3 * r + 3] * o_w[rows])
    o_ref[...] = jnp.concatenate(outs, axis=1).astype(BF16)


def nsa_attention(zb, kvc, gates, batch, seq, kv_col0):
    tq, tk = NSA_QBLOCK, 512
    G, R = NSA_GROUPS, NSA_REP
    tk = min(tk, seq)
    assert seq % tk == 0 and tk % tq == 0 and seq >= NSA_WINDOW + tq
    nq = seq // tq
    n_cmp = seq // NSA_CMP_STRIDE
    n_slc = seq // NSA_SLC_LEN
    starts = np.arange(n_cmp) * NSA_CMP_STRIDE
    sb = np.arange(n_slc) * NSA_SLC_LEN
    overlap = np.clip(np.minimum(starts[:, None] + NSA_CMP_LEN, sb[None, :] + NSA_SLC_LEN)
                      - np.maximum(starts[:, None], sb[None, :]), 0, None) / NSA_CMP_LEN
    expand = (np.arange(seq)[None, :] // NSA_SLC_LEN == np.arange(n_slc)[:, None]).astype(np.float32)
    kern = functools.partial(_nsa_attn_kernel, tq=tq, tk=tk, seq=seq)
    cb = kv_col0
    kv_spec = lambda off: pl.BlockSpec((seq, LANES), lambda b, g, i: (b, cb + off * G + g))
    cmp_spec = lambda kv: pl.BlockSpec((None, None, None, n_cmp, HEAD_DIM), lambda b, g, i: (b, g, kv, 0, 0))
    return pl.pallas_call(
        kern,
        out_shape=jax.ShapeDtypeStruct((batch * seq, G * R * HEAD_DIM), BF16),
        grid=(batch, G, nq),
        in_specs=[pl.BlockSpec((tq, R * HEAD_DIM), lambda b, g, i: (b * nq + i, g)),
                  cmp_spec(0), cmp_spec(1), kv_spec(2), kv_spec(3), kv_spec(4), kv_spec(5),
                  pl.BlockSpec((None, tq, 3 * R), lambda b, g, i: (g, b * nq + i, 0)),
                  pl.BlockSpec((n_cmp, n_slc), lambda b, g, i: (0, 0)),
                  pl.BlockSpec((n_slc, seq), lambda b, g, i: (0, 0))],
        out_specs=pl.BlockSpec((tq, R * HEAD_DIM), lambda b, g, i: (b * nq + i, g)),
        compiler_params=_cparams(("arbitrary", "arbitrary", "arbitrary")),
        name="nsa_attention",
    )(zb, kvc, kvc, zb, zb, zb, zb, gates, jnp.asarray(overlap, BF16), jnp.asarray(expand, BF16))


def _flash_update(carry, s, v):
    m, l, acc = carry
    m_new = jnp.maximum(m, jnp.max(s, axis=1, keepdims=True))
    alpha = jnp.exp(m - m_new)
    p = jnp.exp(s - m_new)
    l = alpha * l + jnp.sum(p, axis=1, keepdims=True)
    acc = alpha * acc + _dot(p.astype(BF16), v)
    return m_new, l, acc


def _flash_init(tq, width):
    return (jnp.full((tq, 1), NEG, F32), jnp.zeros((tq, 1), F32), jnp.zeros((tq, width), F32))


def _causal_mask(tq):
    return lax.broadcasted_iota(jnp.int32, (tq, tq), 1) <= lax.broadcasted_iota(jnp.int32, (tq, tq), 0)


def _diff_attn_kernel(q_ref, k_ref, v_ref, lam_ref, subg_ref, o_ref, *, tq, lam_init):
    qi = pl.program_id(2)
    q = q_ref[...]
    q1, q2 = q[:, :HEAD_DIM], q[:, HEAD_DIM:]

    def tile(j, carry, diag):
        c1, c2 = carry
        k0 = pl.multiple_of(j * tq, tq)
        k = k_ref[pl.ds(k0, tq), :]
        v = v_ref[pl.ds(k0, tq), :]
        s1 = _dot_nt(q1, k[:, :HEAD_DIM])
        s2 = _dot_nt(q2, k[:, HEAD_DIM:])
        if diag:
            keep = _causal_mask(tq)
            s1 = jnp.where(keep, s1, NEG)
            s2 = jnp.where(keep, s2, NEG)
        return _flash_update(c1, s1, v), _flash_update(c2, s2, v)

    width = v_ref.shape[1]
    init = (_flash_init(tq, width), _flash_init(tq, width))
    carry = lax.fori_loop(0, qi, lambda j, c: tile(j, c, False), init)
    (_, l1, a1), (_, l2, a2) = tile(qi, carry, True)
    lp = lam_ref[...]
    lam = (jnp.exp(jnp.sum(lp[0:1] * lp[1:2], axis=1, keepdims=True))
           - jnp.exp(jnp.sum(lp[2:3] * lp[3:4], axis=1, keepdims=True)) + lam_init)
    o = a1 / jnp.maximum(l1, 1e-30) - lam * (a2 / jnp.maximum(l2, 1e-30))
    o_ref[...] = (_rms(o, subg_ref[...]) * (1.0 - lam_init)).astype(BF16)


def diff_attention(zb, batch, seq, q_col, k_col, v_col, diff_lam, sub_g, layer, *, tq=256):
    H = DIFF_HEADS
    W2 = 2 * HEAD_DIM
    tq = min(tq, seq)
    nq = seq // tq
    lam_init = 0.8 - 0.6 * math.exp(-0.3 * layer)
    kern = functools.partial(_diff_attn_kernel, tq=tq, lam_init=lam_init)
    return pl.pallas_call(
        kern,
        out_shape=jax.ShapeDtypeStruct((batch * seq, H * W2), BF16),
        grid=(batch, H, nq),
        in_specs=[pl.BlockSpec((tq, W2), lambda b, h, i: (b * nq + i, q_col + h)),
                  pl.BlockSpec((seq, W2), lambda b, h, i: (b, k_col + h)),
                  pl.BlockSpec((seq, W2), lambda b, h, i: (b, v_col + h)),
                  pl.BlockSpec((4, HEAD_DIM), lambda b, h, i: (0, 0)),
                  pl.BlockSpec((1, W2), lambda b, h, i: (0, 0))],
        out_specs=pl.BlockSpec((tq, W2), lambda b, h, i: (b * nq + i, h)),
        compiler_params=_cparams(("arbitrary", "arbitrary", "arbitrary")),
        name="diff_attention",
    )(zb, zb, zb, diff_lam, sub_g.reshape(1, W2))


def _fox_attn_kernel(q_ref, k_ref, v_ref, ccol_ref, crow_ref, o_ref, *, tq):
    h = pl.program_id(1)
    qi = pl.program_id(2)
    q = q_ref[...]
    lane = lax.broadcasted_iota(jnp.int32, ccol_ref.shape, 1)
    cq = jnp.sum(jnp.where(lane == h, ccol_ref[...], 0.0), axis=1, keepdims=True)

    def tile(j, carry, diag):
        k0 = pl.multiple_of(j * tq, tq)
        s = _dot_nt(q, k_ref[pl.ds(k0, tq), :]) + cq - crow_ref[:, pl.ds(k0, tq)]
        if diag:
            s = jnp.where(_causal_mask(tq), s, NEG)
        return _flash_update(carry, s, v_ref[pl.ds(k0, tq), :])

    carry = lax.fori_loop(0, qi, lambda j, c: tile(j, c, False), _flash_init(tq, HEAD_DIM))
    _, l, acc = tile(qi, carry, True)
    o_ref[...] = (acc / jnp.maximum(l, 1e-30)).astype(BF16)


def fox_attention(zb, c_tok, c_row, batch, seq, *, tq=512):
    H = FOX_HEADS
    tq = min(tq, seq)
    nq = seq // tq
    kern = functools.partial(_fox_attn_kernel, tq=tq)
    return pl.pallas_call(
        kern,
        out_shape=jax.ShapeDtypeStruct((batch * seq, H * HEAD_DIM), BF16),
        grid=(batch, H, nq),
        in_specs=[pl.BlockSpec((tq, LANES), lambda b, h, i: (b * nq + i, h)),
                  pl.BlockSpec((seq, LANES), lambda b, h, i: (b, H + h)),
                  pl.BlockSpec((seq, LANES), lambda b, h, i: (b, 2 * H + h)),
                  pl.BlockSpec((tq, H), lambda b, h, i: (b * nq + i, 0)),
                  pl.BlockSpec((None, 1, seq), lambda b, h, i: (b * H + h, 0, 0))],
        out_specs=pl.BlockSpec((tq, LANES), lambda b, h, i: (b * nq + i, h)),
        compiler_params=_cparams(("arbitrary", "arbitrary", "arbitrary")),
        name="fox_attention",
    )(zb, zb, zb, c_tok, c_row)


def _fox_gate_kernel(z_ref, b_ref, o_ref, carry_ref, *, tr):
    @pl.when(pl.program_id(1) == 0)
    def _():
        carry_ref[...] = jnp.zeros_like(carry_ref)

    logf = jax.nn.log_sigmoid(z_ref[...] + b_ref[...])
    tri = (lax.broadcasted_iota(jnp.int32, (tr, tr), 1) <= lax.broadcasted_iota(jnp.int32, (tr, tr), 0)).astype(BF16)
    p1 = logf.astype(BF16)
    r1 = logf - p1.astype(F32)
    p2 = r1.astype(BF16)
    p3 = (r1 - p2.astype(F32)).astype(BF16)
    c = _dot(tri, p1) + _dot(tri, p2) + _dot(tri, p3) + carry_ref[...]
    o_ref[...] = c
    carry_ref[...] = c[tr - 1:tr, :]


def fox_gate_cumsum(z, col_block, bias_row, batch, seq, *, tr=256):
    tr = min(tr, seq)
    nt = seq // tr
    kern = functools.partial(_fox_gate_kernel, tr=tr)
    return pl.pallas_call(
        kern,
        out_shape=jax.ShapeDtypeStruct((batch * seq, LANES), F32),
        grid=(batch, nt),
        in_specs=[pl.BlockSpec((tr, LANES), lambda b, i: (b * nt + i, col_block)),
                  pl.BlockSpec((1, LANES), lambda b, i: (0, 0))],
        out_specs=pl.BlockSpec((tr, LANES), lambda b, i: (b * nt + i, 0)),
        scratch_shapes=[pltpu.VMEM((1, LANES), F32)],
        compiler_params=_cparams(("arbitrary", "arbitrary")),
        name="fox_gate_cumsum",
    )(z, bias_row)


def _cross_attn_kernel(x_ref, g_ref, wq_ref, kv_ref, qg_ref, kg_ref, wo_ref, o_ref):
    x = x_ref[...]
    q = _dot(_rms(x, g_ref[...]).astype(BF16), wq_ref[...])
    kv = kv_ref[...]
    width = MEM_HEADS * HEAD_DIM
    outs = []
    for h in range(MEM_HEADS):
        cols = slice(h * HEAD_DIM, (h + 1) * HEAD_DIM)
        qh = _rms(q[:, cols], qg_ref[...]).astype(BF16)
        kh = _rms(kv[:, cols], kg_ref[...]).astype(BF16)
        vh = kv[:, width + h * HEAD_DIM: width + (h + 1) * HEAD_DIM].astype(BF16)
        p = _softmax_rows(_dot_nt(qh, kh))
        outs.append(_dot(p.astype(BF16), vh))
    o = jnp.concatenate(outs, axis=1).astype(BF16)
    o_ref[...] = x + _dot(o, wo_ref[...])


def cross_attention(x, g, wq, kv, q_gain_scaled, k_gain, wo, seq, mem_len, *, tm=512):
    M, D = x.shape
    tm = min(tm, seq)
    nt = seq // tm
    width = MEM_HEADS * HEAD_DIM
    return pl.pallas_call(
        _cross_attn_kernel,
        out_shape=jax.ShapeDtypeStruct((M, D), F32),
        grid=(M // tm,),
        in_specs=[pl.BlockSpec((tm, D), lambda i: (i, 0)),
                  pl.BlockSpec((1, D), lambda i: (0, 0)),
                  pl.BlockSpec((D, width), lambda i: (0, 0)),
                  pl.BlockSpec((mem_len, 2 * width), lambda i: (i // nt, 0)),
                  pl.BlockSpec((1, HEAD_DIM), lambda i: (0, 0)),
                  pl.BlockSpec((1, HEAD_DIM), lambda i: (0, 0)),
                  pl.BlockSpec((width, D), lambda i: (0, 0))],
        out_specs=pl.BlockSpec((tm, D), lambda i: (i, 0)),
        compiler_params=_cparams(("arbitrary",)),
        name="cross_attention",
    )(x, g.reshape(1, D), wq, kv, q_gain_scaled.reshape(1, HEAD_DIM), k_gain.reshape(1, HEAD_DIM), wo)


def _moe_router_kernel(x_ref, g_ref, wr_ref, br_ref, hn_ref, route_ref, onehot_ref):
    hn = _rms(x_ref[...], g_ref[...])
    hn_ref[...] = hn
    w = wr_ref[...]
    w_hi = w.astype(BF16)
    w_lo = (w - w_hi.astype(F32)).astype(BF16)
    h_hi = hn.astype(BF16)
    h_lo = (hn - h_hi.astype(F32)).astype(BF16)
    logits = _dot(h_hi, w_hi) + _dot(h_lo, w_hi) + _dot(h_hi, w_lo) + br_ref[...]
    tm = logits.shape[0]
    lane = lax.broadcasted_iota(jnp.int32, (tm, LANES), 1)
    is_grp = lane < N_GROUPS
    lg = jnp.where(is_grp, logits, -jnp.inf)
    eg = jnp.exp(lg - jnp.max(lg, axis=1, keepdims=True))
    p_grp = eg / jnp.sum(eg, axis=1, keepdims=True)
    p_top = jnp.max(p_grp, axis=1, keepdims=True)
    grp = jnp.min(jnp.where(is_grp & (p_grp == p_top), lane, LANES), axis=1, keepdims=True)
    lo = N_GROUPS + grp * EXPERTS_PER_GROUP
    in_grp = (lane >= lo) & (lane < lo + EXPERTS_PER_GROUP)
    le = jnp.where(in_grp, logits, -jnp.inf)
    ee = jnp.exp(le - jnp.max(le, axis=1, keepdims=True))
    p_in = ee / jnp.sum(ee, axis=1, keepdims=True)
    m1 = jnp.max(p_in, axis=1, keepdims=True)
    i1 = jnp.min(jnp.where(in_grp & (p_in == m1), lane, LANES), axis=1, keepdims=True)
    rest = jnp.where(in_grp & (lane != i1), p_in, -jnp.inf)
    m2 = jnp.max(rest, axis=1, keepdims=True)
    i2 = jnp.min(jnp.where(rest == m2, lane, LANES), axis=1, keepdims=True)
    denom = m1 + m2
    w1 = p_top * m1 / denom
    w2 = p_top * m2 / denom
    e1 = (i1 - N_GROUPS).astype(F32)
    e2 = (i2 - N_GROUPS).astype(F32)
    route_ref[...] = jnp.where(lane == 0, e1, jnp.where(lane == 1, e2, jnp.where(lane == 2, w1, jnp.where(lane == 3, w2, 0.0))))
    onehot_ref[...] = ((lane == i1 - N_GROUPS) | (lane == i2 - N_GROUPS)).astype(BF16)


def moe_router(x, g, w_router, b_router, *, tm=512):
    M, D = x.shape
    tm = min(tm, M)
    return pl.pallas_call(
        _moe_router_kernel,
        out_shape=(jax.ShapeDtypeStruct((M, D), F32),
                   jax.ShapeDtypeStruct((M, LANES), F32),
                   jax.ShapeDtypeStruct((M, LANES), BF16)),
        grid=(M // tm,),
        in_specs=[pl.BlockSpec((tm, D), lambda i: (i, 0)),
                  pl.BlockSpec((1, D), lambda i: (0, 0)),
                  pl.BlockSpec((D, LANES), lambda i: (0, 0)),
                  pl.BlockSpec((1, LANES), lambda i: (0, 0))],
        out_specs=(pl.BlockSpec((tm, D), lambda i: (i, 0)),
                   pl.BlockSpec((tm, LANES), lambda i: (i, 0)),
                   pl.BlockSpec((tm, LANES), lambda i: (i, 0))),
        compiler_params=_cparams(("arbitrary",)),
        name="moe_router",
    )(x, g.reshape(1, D), w_router, b_router)


def _moe_rank_kernel(onehot_ref, rank_ref, count_ref, carry_ref, *, tr):
    i = pl.program_id(0)

    @pl.when(i == 0)
    def _():
        carry_ref[...] = jnp.zeros_like(carry_ref)

    oh = onehot_ref[...]
    tri = (lax.broadcasted_iota(jnp.int32, (tr, tr), 1) < lax.broadcasted_iota(jnp.int32, (tr, tr), 0)).astype(BF16)
    rank_ref[...] = _dot(tri, oh) + carry_ref[...]
    carry_ref[...] = carry_ref[...] + jnp.sum(oh.astype(F32), axis=0, keepdims=True)
    count_ref[...] = carry_ref[...]


def moe_rank(onehot, *, tr=512):
    M = onehot.shape[0]
    tr = min(tr, M)
    kern = functools.partial(_moe_rank_kernel, tr=tr)
    return pl.pallas_call(
        kern,
        out_shape=(jax.ShapeDtypeStruct((M, LANES), F32), jax.ShapeDtypeStruct((1, LANES), F32)),
        grid=(M // tr,),
        in_specs=[pl.BlockSpec((tr, LANES), lambda i: (i, 0))],
        out_specs=(pl.BlockSpec((tr, LANES), lambda i: (i, 0)), pl.BlockSpec((1, LANES), lambda i: (0, 0))),
        scratch_shapes=[pltpu.VMEM((1, LANES), F32)],
        compiler_params=_cparams(("arbitrary",)),
        name="moe_rank",
    )(onehot)


def _moe_slot_kernel(route_ref, rank_ref, pstart_ref, o_ref):
    route = route_ref[...]
    slot = rank_ref[...] + pstart_ref[...]
    lane = lax.broadcasted_iota(jnp.int32, route.shape, 1)
    e1 = route[:, 0:1].astype(jnp.int32)
    e2 = route[:, 1:2].astype(jnp.int32)
    d1 = jnp.sum(jnp.where(lane == e1, slot, 0.0), axis=1, keepdims=True)
    d2 = jnp.sum(jnp.where(lane == e2, slot, 0.0), axis=1, keepdims=True)
    o_ref[...] = jnp.where(lane == 0, d1, jnp.where(lane == 1, d2, 0.0)).astype(jnp.int32)


def moe_slots(route, rank, pstart_row, *, tm=1024):
    M = route.shape[0]
    tm = min(tm, M)
    return pl.pallas_call(
        _moe_slot_kernel,
        out_shape=jax.ShapeDtypeStruct((M, LANES), jnp.int32),
        grid=(M // tm,),
        in_specs=[pl.BlockSpec((tm, LANES), lambda i: (i, 0)),
                  pl.BlockSpec((tm, LANES), lambda i: (i, 0)),
                  pl.BlockSpec((1, LANES), lambda i: (0, 0))],
        out_specs=pl.BlockSpec((tm, LANES), lambda i: (i, 0)),
        compiler_params=_cparams(("arbitrary",)),
        name="moe_slots",
    )(route, rank, pstart_row)


def _moe_dispatch_kernel(lastblk_ref, d1_hbm, d2_hbm, hn_ref, xb_hbm, zero_ref, s1_ref, s2_ref,
                         idx_sem, row_sem, zero_sem, *, tm, n_blocks):
    i = pl.program_id(0)

    def zero_copy(blk):
        return pltpu.make_async_copy(zero_ref, xb_hbm.at[pl.ds(blk * MOE_ROWS, MOE_ROWS)], zero_sem)

    @pl.when(i == 0)
    def _():
        zero_ref[...] = jnp.zeros_like(zero_ref)
        n_used = lastblk_ref[N_EXPERTS]
        for e in range(N_EXPERTS):
            @pl.when(lastblk_ref[e] >= 0)
            def _():
                zero_copy(lastblk_ref[e]).start()

        def start_tail(blk, _):
            zero_copy(blk).start()
            return 0

        def wait_tail(blk, _):
            zero_copy(blk).wait()
            return 0

        lax.fori_loop(n_used, n_blocks, start_tail, 0)
        for e in range(N_EXPERTS):
            @pl.when(lastblk_ref[e] >= 0)
            def _():
                zero_copy(lastblk_ref[e]).wait()
        lax.fori_loop(n_used, n_blocks, wait_tail, 0)

    c1 = pltpu.make_async_copy(d1_hbm.at[i], s1_ref, idx_sem.at[0])
    c2 = pltpu.make_async_copy(d2_hbm.at[i], s2_ref, idx_sem.at[1])
    c1.start()
    c2.start()
    c1.wait()
    c2.wait()

    def row_copy(r, slot):
        return pltpu.make_async_copy(hn_ref.at[pl.ds(r, 1)], xb_hbm.at[pl.ds(slot, 1)], row_sem)

    def issue(r, _):
        row_copy(r, s1_ref[r]).start()
        row_copy(r, s2_ref[r]).start()
        return 0

    lax.fori_loop(0, tm, issue, 0, unroll=8)

    def drain(r, _):
        row_copy(r, 0).wait()
        row_copy(r, 0).wait()
        return 0

    lax.fori_loop(0, tm, drain, 0, unroll=8)


def moe_dispatch(hn, d1, d2, lastblk, n_slots, *, tm=512):
    M, D = hn.shape
    tm = min(tm, M)
    assert n_slots % MOE_ROWS == 0
    kern = functools.partial(_moe_dispatch_kernel, tm=tm, n_blocks=n_slots // MOE_ROWS)
    grid_spec = pltpu.PrefetchScalarGridSpec(
        num_scalar_prefetch=1,
        grid=(M // tm,),
        in_specs=[pl.BlockSpec(memory_space=pl.ANY), pl.BlockSpec(memory_space=pl.ANY),
                  pl.BlockSpec((tm, D), lambda i, *_: (i, 0))],
        out_specs=pl.BlockSpec(memory_space=pl.ANY),
        scratch_shapes=[pltpu.VMEM((MOE_ROWS, D), F32),
                        pltpu.SMEM((tm,), jnp.int32), pltpu.SMEM((tm,), jnp.int32),
                        pltpu.SemaphoreType.DMA((2,)), pltpu.SemaphoreType.DMA, pltpu.SemaphoreType.DMA],
    )
    return pl.pallas_call(
        kern,
        out_shape=jax.ShapeDtypeStruct((n_slots, D), F32),
        grid_spec=grid_spec,
        compiler_params=_cparams(("arbitrary",)),
        name="moe_dispatch",
    )(lastblk, d1.reshape(M // tm, tm), d2.reshape(M // tm, tm), hn)


def _moe_expert_kernel(blk_e_ref, n_used_ref, x_ref, w1_ref, w3_ref, w2_ref, o_ref, w1b, w3b, w2b):
    i = pl.program_id(0)

    @pl.when(i < n_used_ref[0])
    def _():
        prev = blk_e_ref[jnp.maximum(i - 1, 0)]

        @pl.when((i == 0) | (blk_e_ref[i] != prev))
        def _():
            w1b[...] = w1_ref[...].astype(BF16)
            w3b[...] = w3_ref[...].astype(BF16)
            w2b[...] = w2_ref[...].astype(BF16)

        x = x_ref[...].astype(BF16)
        h = (jax.nn.silu(_dot(x, w1b[...])) * _dot(x, w3b[...])).astype(BF16)
        o_ref[...] = _dot(h, w2b[...])

    @pl.when(i >= n_used_ref[0])
    def _():
        o_ref[...] = jnp.zeros_like(o_ref)


def moe_experts(xb, blk_e, n_used, w1, w3, w2):
    P, D = xb.shape
    FF = w1.shape[-1]
    n_blk = P // MOE_ROWS

    def row_map(i, blk_e_ref, n_used_ref):
        return (jnp.minimum(i, n_used_ref[0] - 1), 0)

    def out_map(i, blk_e_ref, n_used_ref):
        return (i, 0)

    def w_map(i, blk_e_ref, n_used_ref):
        return (blk_e_ref[i], 0, 0)

    grid_spec = pltpu.PrefetchScalarGridSpec(
        num_scalar_prefetch=2,
        grid=(n_blk,),
        in_specs=[pl.BlockSpec((MOE_ROWS, D), row_map),
                  pl.BlockSpec((None, D, FF), w_map),
                  pl.BlockSpec((None, D, FF), w_map),
                  pl.BlockSpec((None, FF, D), w_map)],
        out_specs=pl.BlockSpec((MOE_ROWS, D), out_map),
        scratch_shapes=[pltpu.VMEM((D, FF), BF16), pltpu.VMEM((D, FF), BF16), pltpu.VMEM((FF, D), BF16)],
    )
    return pl.pallas_call(
        _moe_expert_kernel,
        out_shape=jax.ShapeDtypeStruct((P, D), F32),
        grid_spec=grid_spec,
        compiler_params=_cparams(("arbitrary",)),
        name="moe_experts",
    )(blk_e, n_used, xb, w1, w3, w2)


def _moe_combine_kernel(x_ref, route_ref, d1_hbm, d2_hbm, yb_hbm, o_ref, y1_ref, y2_ref, s1_ref, s2_ref,
                        idx_sem, row_sem, *, tm):
    i = pl.program_id(0)
    c1 = pltpu.make_async_copy(d1_hbm.at[i], s1_ref, idx_sem.at[0])
    c2 = pltpu.make_async_copy(d2_hbm.at[i], s2_ref, idx_sem.at[1])
    c1.start()
    c2.start()
    c1.wait()
    c2.wait()

    def row_copy(slot, dst, r):
        return pltpu.make_async_copy(yb_hbm.at[pl.ds(slot, 1)], dst.at[pl.ds(r, 1)], row_sem)

    def issue(r, _):
        row_copy(s1_ref[r], y1_ref, r).start()
        row_copy(s2_ref[r], y2_ref, r).start()
        return 0

    lax.fori_loop(0, tm, issue, 0, unroll=8)

    def drain(r, _):
        row_copy(0, y1_ref, r).wait()
        row_copy(0, y2_ref, r).wait()
        return 0

    lax.fori_loop(0, tm, drain, 0, unroll=8)
    route = route_ref[...]
    o_ref[...] = x_ref[...] + (y1_ref[...] * route[:, 2:3] + y2_ref[...] * route[:, 3:4])


def moe_combine(x, route, d1, d2, yb, *, tm=256):
    M, D = x.shape
    tm = min(tm, M)
    kern = functools.partial(_moe_combine_kernel, tm=tm)
    return pl.pallas_call(
        kern,
        out_shape=jax.ShapeDtypeStruct((M, D), F32),
        grid=(M // tm,),
        in_specs=[pl.BlockSpec((tm, D), lambda i: (i, 0)),
                  pl.BlockSpec((tm, LANES), lambda i: (i, 0)),
                  pl.BlockSpec(memory_space=pl.ANY),
                  pl.BlockSpec(memory_space=pl.ANY),
                  pl.BlockSpec(memory_space=pl.ANY)],
        out_specs=pl.BlockSpec((tm, D), lambda i: (i, 0)),
        scratch_shapes=[pltpu.VMEM((tm, D), F32), pltpu.VMEM((tm, D), F32),
                        pltpu.SMEM((tm,), jnp.int32), pltpu.SMEM((tm,), jnp.int32),
                        pltpu.SemaphoreType.DMA((2,)), pltpu.SemaphoreType.DMA],
        compiler_params=_cparams(("arbitrary",)),
        name="moe_combine",
    )(x, route, d1.reshape(M // tm, tm), d2.reshape(M // tm, tm), yb)


def hier_moe(x, g, wg, bg, we, be, w1, w3, w2):
    M, D = x.shape
    n_route = N_GROUPS + N_EXPERTS
    w_router = jnp.pad(jnp.concatenate([wg, we], axis=1), ((0, 0), (0, LANES - n_route)))
    b_router = jnp.pad(jnp.concatenate([bg, be]), (0, LANES - n_route)).reshape(1, LANES)
    hn, route, onehot = moe_router(x, g, w_router, b_router)
    rank, counts = moe_rank(onehot)
    cnt = counts[0, :N_EXPERTS].astype(jnp.int32)
    nblk = (cnt + MOE_ROWS - 1) // MOE_ROWS
    bend = jnp.cumsum(nblk)
    bstart = bend - nblk
    n_blk_max = (2 * M) // MOE_ROWS + N_EXPERTS
    pstart_row = jnp.pad((bstart * MOE_ROWS).astype(F32), (0, LANES - N_EXPERTS)).reshape(1, LANES)
    slots = moe_slots(route, rank, pstart_row)
    d1, d2 = slots[:, 0], slots[:, 1]
    n_used = bend[-1:].astype(jnp.int32)
    lastblk = jnp.concatenate([jnp.where(nblk > 0, bend - 1, -1).astype(jnp.int32), n_used])
    blk_ids = jnp.arange(n_blk_max, dtype=jnp.int32)
    blk_e = jnp.minimum(jnp.sum(bend[None, :] <= blk_ids[:, None], axis=1), N_EXPERTS - 1).astype(jnp.int32)
    xb = moe_dispatch(hn, d1, d2, lastblk, n_blk_max * MOE_ROWS)
    yb = moe_experts(xb, blk_e, n_used, w1, w3, w2)
    return moe_combine(x, route, d1, d2, yb)


def _even_mixer(x, g, w_in, w_out, nsa_qk_g, cmp_pe, cmp_w1, cmp_w2, diff_qk_g, diff_lam, diff_sub_g,
                layer, batch, seq, cos, sin):
    M, D = x.shape
    scale = HEAD_DIM ** -0.5
    nq_w = NSA_HEADS * HEAD_DIM
    kv_w = 6 * NSA_GROUPS * HEAD_DIM
    gate_w = 3 * NSA_HEADS
    dq_w = 2 * DIFF_HEADS * HEAD_DIM
    dv_w = DIFF_HEADS * 2 * HEAD_DIM
    c_gate = nq_w + kv_w
    c_diff = c_gate + gate_w
    main_w = nq_w + kv_w + 2 * dq_w + dv_w
    w_cat = jnp.concatenate([w_in[:, :c_gate], w_in[:, c_diff:], w_in[:, c_gate:c_diff],
                             jnp.zeros((D, LANES - gate_w), w_in.dtype)], axis=1).astype(BF16)
    z = rms_matmul(x, g, w_cat, tn=(main_w + LANES) // 9, name="even_in_proj")
    n_blocks = main_w // LANES
    b_q, b_kv, b_dq, b_dk, b_dv = 0, nq_w // LANES, (nq_w + kv_w) // LANES, (nq_w + kv_w + dq_w) // LANES, \
        (nq_w + kv_w + 2 * dq_w) // LANES
    G = NSA_GROUPS
    modes = [(None, False)] * n_blocks
    for lo, hi, row in ((b_q, b_kv, 0), (b_kv + 2 * G, b_kv + 3 * G, 1), (b_kv + 4 * G, b_kv + 5 * G, 2),
                        (b_dq, b_dk, 3), (b_dk, b_dv, 4)):
        modes[lo:hi] = [(row, True)] * (hi - lo)
    gains = jnp.stack([nsa_qk_g[0] * scale, nsa_qk_g[2], nsa_qk_g[3], diff_qk_g[0] * scale, diff_qk_g[1]])
    zb = head_prep(z, modes, gains, cos, sin, seq, name="even_head_prep")
    cmp_end = np.arange(seq // NSA_CMP_STRIDE) * NSA_CMP_STRIDE + NSA_CMP_LEN - 1
    cos_c, sin_c = rope_tables(cmp_end)
    kvc = nsa_compress(z, b_kv, batch, seq, cmp_pe, cmp_w1.astype(BF16), cmp_w2.astype(BF16), nsa_qk_g[1],
                       cos_c, sin_c)
    gates = z[:, main_w:main_w + gate_w].reshape(M, G, 3 * NSA_REP).transpose(1, 0, 2)
    o_nsa = nsa_attention(zb, kvc, gates, batch, seq, b_kv)
    o_diff = diff_attention(zb, batch, seq, b_dq // 2, b_dk // 2, b_dv // 2, diff_lam, diff_sub_g, layer)
    return matmul_residual([o_nsa, o_diff], w_out.astype(BF16), x, name="even_out_proj")


def _odd_mixer(x, g, w_in, w_out, f_b, qk_g, batch, seq, cos, sin):
    M, D = x.shape
    H = FOX_HEADS
    scale = HEAD_DIM ** -0.5
    width = H * HEAD_DIM
    w_cat = jnp.pad(w_in, ((0, 0), (0, LANES - H))).astype(BF16)
    z = rms_matmul(x, g, w_cat, tn=(3 * width + LANES) // 7, name="odd_in_proj")
    n_blocks = 3 * H
    modes = [(0, False)] * H + [(1, False)] * H + [(None, False)] * H
    gains = jnp.stack([qk_g[0] * scale, qk_g[1]])
    zb = head_prep(z, modes, gains, cos, sin, seq, name="odd_head_prep")
    bias_row = jnp.pad(f_b, (0, LANES - H)).reshape(1, LANES)
    c = fox_gate_cumsum(z, n_blocks, bias_row, batch, seq)
    c_tok = c[:, :H]
    c_row = c_tok.reshape(batch, seq, H).transpose(0, 2, 1).reshape(batch * H, 1, seq)
    o = fox_attention(zb, c_tok, c_row, batch, seq)
    return matmul_residual([o], w_out.astype(BF16), x, name="odd_out_proj")


def kernel(x, mem, norm_g, ev_w_in, ev_w_out, nsa_qk_g, nsa_cmp_pe, nsa_cmp_w1, nsa_cmp_w2, diff_qk_g, diff_lam, diff_sub_g, od_w_in, od_w_out, fox_f_b, fox_qk_g, ca_wq, ca_wkv, ca_qk_g, ca_wo, moe_wg, moe_bg, moe_we, moe_be, moe_w1, moe_w3, moe_w2):
    B, T, D = x.shape
    mem_len = mem.shape[1]
    depth = norm_g.shape[0]
    scale = HEAD_DIM ** -0.5
    cos, sin = rope_tables(np.arange(T))
    xt = x.reshape(B * T, D)
    mt = mem.reshape(B * mem_len, D)
    for layer in range(depth):
        i = layer // 2
        if layer % 2 == 0:
            xt = _even_mixer(xt, norm_g[layer, 0], ev_w_in[i], ev_w_out[i], nsa_qk_g[i], nsa_cmp_pe[i],
                             nsa_cmp_w1[i], nsa_cmp_w2[i], diff_qk_g[i], diff_lam[i], diff_sub_g[i], layer,
                             B, T, cos, sin)
        else:
            xt = _odd_mixer(xt, norm_g[layer, 0], od_w_in[i], od_w_out[i], fox_f_b[i], fox_qk_g[i], B, T, cos, sin)
        kv = rms_matmul(mt, norm_g[layer, 2], ca_wkv[layer].astype(BF16), name="mem_kv_proj")
        xt = cross_attention(xt, norm_g[layer, 1], ca_wq[layer].astype(BF16), kv, ca_qk_g[layer, 0] * scale,
                             ca_qk_g[layer, 1], ca_wo[layer].astype(BF16), T, mem_len)
        xt = hier_moe(xt, norm_g[layer, 3], moe_wg[layer], moe_bg[layer], moe_we[layer], moe_be[layer],
                      moe_w1[layer], moe_w3[layer], moe_w2[layer])
    return xt.reshape(B, T, D)
```

```python
import functools
import math

import numpy as np
import jax
import jax.numpy as jnp
from jax import lax
from jax.experimental import pallas as pl
from jax.experimental.pallas import tpu as pltpu

F32 = jnp.float32
BF16 = jnp.bfloat16

HEAD_DIM = 128
ROPE_THETA = 10000.0
EPS = 1e-6
NEG = -1e30
LOG2E = math.log2(math.e)

NSA_HEADS = 8
NSA_GROUPS = 2
NSA_REP = NSA_HEADS // NSA_GROUPS
NSA_CMP_LEN = 32
NSA_CMP_STRIDE = 16
NSA_CMP_HIDDEN = 256
NSA_SLC_LEN = 64
NSA_SLC_TOPK = 8
NSA_WINDOW = 512
NSA_QBLOCK = 128
DIFF_HEADS = 4
FOX_HEADS = 16
MEM_HEADS = 4
N_GROUPS = 4
EXPERTS_PER_GROUP = 8
N_EXPERTS = N_GROUPS * EXPERTS_PER_GROUP
MOE_ROWS = 256
ROW_DMA_UNROLL = 8
LANES = 128
VMEM_LIMIT = 56 * 1024 * 1024


def _cparams(sem):
    return pltpu.CompilerParams(dimension_semantics=sem, vmem_limit_bytes=VMEM_LIMIT)


def _dot(a, b):
    return jnp.dot(a, b, preferred_element_type=F32)


def _dot_nt(a, b):
    return lax.dot_general(a, b, (((1,), (1,)), ((), ())), preferred_element_type=F32)


def _rms(x, g):
    ms = jnp.mean(x * x, axis=-1, keepdims=True)
    return x * lax.rsqrt(ms + EPS) * g


def _rope(y, cos, sin_signed):
    return y * cos + pltpu.roll(y, HEAD_DIM // 2, 1) * sin_signed


def _rms_matmul_kernel(x_ref, g_ref, w_ref, o_ref, xn_ref):
    @pl.when(pl.program_id(1) == 0)
    def _():
        xn_ref[...] = _rms(x_ref[...], g_ref[...]).astype(BF16)

    o_ref[...] = _dot(xn_ref[...], w_ref[...])


def rms_matmul(x, g, w, *, tm=1024, tn=512, name="rms_matmul"):
    M, K = x.shape
    N = w.shape[1]
    tm, tn = min(tm, M), min(tn, N)
    assert M % tm == 0 and N % tn == 0
    return pl.pallas_call(
        _rms_matmul_kernel,
        out_shape=jax.ShapeDtypeStruct((M, N), F32),
        grid=(M // tm, N // tn),
        in_specs=[pl.BlockSpec((tm, K), lambda i, j: (i, 0)),
                  pl.BlockSpec((1, K), lambda i, j: (0, 0)),
                  pl.BlockSpec((K, tn), lambda i, j: (0, j))],
        out_specs=pl.BlockSpec((tm, tn), lambda i, j: (i, j)),
        scratch_shapes=[pltpu.VMEM((tm, K), BF16)],
        compiler_params=_cparams(("arbitrary", "arbitrary")),
        name=name,
    )(x, g.reshape(1, K), w)


def _matmul_res_kernel(*refs, tn):
    *x_refs, w_ref, r_ref, o_ref = refs
    for n0 in range(0, o_ref.shape[1], tn):
        acc = r_ref[:, n0:n0 + tn]
        k0 = 0
        for x_ref in x_refs:
            k = x_ref.shape[1]
            acc = acc + _dot(x_ref[...], w_ref[k0:k0 + k, n0:n0 + tn])
            k0 += k
        o_ref[:, n0:n0 + tn] = acc


def matmul_residual(xs, w, res, *, tm=512, tn=512, name="matmul_residual"):
    M = xs[0].shape[0]
    N = w.shape[1]
    tm, tn = min(tm, M), min(tn, N)
    assert M % tm == 0 and N % tn == 0 and sum(x.shape[1] for x in xs) == w.shape[0]
    return pl.pallas_call(
        functools.partial(_matmul_res_kernel, tn=tn),
        out_shape=jax.ShapeDtypeStruct((M, N), F32),
        grid=(M // tm,),
        in_specs=[pl.BlockSpec((tm, x.shape[1]), lambda i: (i, 0)) for x in xs]
        + [pl.BlockSpec(w.shape, lambda i: (0, 0), pipeline_mode=pl.Buffered(1)),
           pl.BlockSpec((tm, N), lambda i: (i, 0))],
        out_specs=pl.BlockSpec((tm, N), lambda i: (i, 0)),
        compiler_params=_cparams(("arbitrary",)),
        name=name,
    )(*xs, w, res)


HEAD_KINDS = ("row", "t", "t32", "raw")


def _head_proj_kernel(x_ref, g_ref, w_ref, gains_ref, cos_ref, sin_ref, *out_refs, groups, kinds):
    refs = dict(zip(kinds, out_refs))
    xn = _rms(x_ref[...], g_ref[...]).astype(BF16)
    col = 0
    for modes in groups:
        z = _dot(xn, w_ref[:, col:col + len(modes) * LANES])
        col += len(modes) * LANES
        for c, (kind, gain_row, rotary, dst) in enumerate(modes):
            if kind == "skip":
                continue
            y = z[:, c * LANES:(c + 1) * LANES]
            if gain_row is not None:
                y = _rms(y, gains_ref[gain_row:gain_row + 1, :])
            if rotary:
                y = _rope(y, cos_ref[...], sin_ref[...])
            span = slice(dst * LANES, (dst + 1) * LANES)
            if kind == "row":
                refs[kind][:, span] = y.astype(BF16)
            elif kind == "raw":
                refs[kind][:, span] = y
            elif kind == "t":
                refs[kind][span, :] = y.T.astype(BF16)
            else:
                refs[kind][span, :] = y.T


def head_projection(x, g, w, modes, gains, cos, sin, batch, seq, *, group=8, tm=512, name="head_projection"):
    M, K = x.shape
    n_blocks = w.shape[1] // LANES
    assert len(modes) == n_blocks
    tm = min(tm, seq)
    assert seq % tm == 0 and M % tm == 0 and tm % LANES == 0
    nt = seq // tm
    counts = {k: 0 for k in HEAD_KINDS}
    placed = []
    for kind, gain_row, rotary in modes:
        placed.append((kind, gain_row, rotary, counts.get(kind, 0)))
        if kind in counts:
            counts[kind] += 1
    groups = tuple(tuple(placed[i:i + group]) for i in range(0, n_blocks, group))
    kinds = tuple(k for k in HEAD_KINDS if counts[k])
    out_shape, out_specs = [], []
    for kind in kinds:
        width = counts[kind] * LANES
        dt = BF16 if kind in ("row", "t") else F32
        if kind in ("row", "raw"):
            out_shape.append(jax.ShapeDtypeStruct((M, width), dt))
            out_specs.append(pl.BlockSpec((tm, width), lambda i: (i, 0)))
        else:
            out_shape.append(jax.ShapeDtypeStruct((batch, width, seq), dt))
            out_specs.append(pl.BlockSpec((None, width, tm), lambda i: (i // nt, 0, i % nt)))
    kern = functools.partial(_head_proj_kernel, groups=groups, kinds=kinds)
    outs = pl.pallas_call(
        kern,
        out_shape=tuple(out_shape),
        grid=(M // tm,),
        in_specs=[pl.BlockSpec((tm, K), lambda i: (i, 0)),
                  pl.BlockSpec((1, K), lambda i: (0, 0)),
                  pl.BlockSpec(w.shape, lambda i: (0, 0), pipeline_mode=pl.Buffered(1)),
                  pl.BlockSpec(gains.shape, lambda i: (0, 0)),
                  pl.BlockSpec((tm, LANES), lambda i: (i % nt, 0)),
                  pl.BlockSpec((tm, LANES), lambda i: (i % nt, 0))],
        out_specs=tuple(out_specs),
        compiler_params=_cparams(("arbitrary",)),
        name=name,
    )(x, g.reshape(1, K), w, gains, cos, sin)
    return dict(zip(kinds, outs))


def rope_tables(pos):
    half = HEAD_DIM // 2
    inv = ROPE_THETA ** (-jnp.arange(half, dtype=F32) / half)
    ang = jnp.asarray(pos).astype(F32)[:, None] * inv[None, :]
    cos, sin = jnp.cos(ang), jnp.sin(ang)
    return jnp.concatenate([cos, cos], axis=-1), jnp.concatenate([-sin, sin], axis=-1)


def _nsa_compress_kernel(tk_ref, tv_ref, pek_ref, pev_ref, w1k_ref, w1v_ref, w2k_ref, w2v_ref,
                         g_ref, cos_ref, sin_ref, kc_ref, vc_ref, *, n_rows):
    half = NSA_CMP_LEN // 2

    def compress(t_ref, pe_ref, w1_ref, w2_ref):
        hid = w1_ref.shape[-1]
        lo = jnp.zeros((n_rows, hid), F32)
        hi = jnp.zeros((n_rows, hid), F32)
        for l in range(NSA_CMP_LEN):
            rows = t_ref[pl.ds(l % half, n_rows, stride=NSA_CMP_STRIDE), :]
            a = (rows + pe_ref[l:l + 1, :]).astype(BF16)
            part = _dot(a, w1_ref[l * HEAD_DIM:(l + 1) * HEAD_DIM, :])
            if l < half:
                lo = lo + part
            else:
                hi = hi + part
        pre = lo + pltpu.roll(hi, n_rows - 1, 0)
        return _dot(jax.nn.gelu(pre).astype(BF16), w2_ref[...])

    kc = compress(tk_ref, pek_ref, w1k_ref, w2k_ref)
    kc_ref[...] = _rope(_rms(kc, g_ref[...]), cos_ref[...], sin_ref[...]).astype(BF16)
    vc_ref[...] = compress(tv_ref, pev_ref, w1v_ref, w2v_ref).astype(BF16)


def nsa_compress(z, col0, batch, seq, pe, w1, w2, g_kc, cos_c, sin_c):
    n_rows = seq // NSA_CMP_STRIDE
    G = NSA_GROUPS
    kern = functools.partial(_nsa_compress_kernel, n_rows=n_rows)
    full = lambda a: pl.BlockSpec(a.shape, lambda b, g: (0,) * a.ndim)
    part = lambda a, kv: pl.BlockSpec((None,) + a.shape[1:], lambda b, g: (kv,) + (0,) * (a.ndim - 1))
    g_kc = g_kc.reshape(1, HEAD_DIM)
    return pl.pallas_call(
        kern,
        out_shape=(jax.ShapeDtypeStruct((batch, G, n_rows, HEAD_DIM), BF16),
                   jax.ShapeDtypeStruct((batch, G, n_rows, HEAD_DIM), BF16)),
        grid=(batch, G),
        in_specs=[pl.BlockSpec((seq, LANES), lambda b, g: (b, col0 + g)),
                  pl.BlockSpec((seq, LANES), lambda b, g: (b, col0 + G + g)),
                  part(pe, 0), part(pe, 1), part(w1, 0), part(w1, 1), part(w2, 0), part(w2, 1),
                  full(g_kc), full(cos_c), full(sin_c)],
        out_specs=(pl.BlockSpec((None, None, n_rows, HEAD_DIM), lambda b, g: (b, g, 0, 0)),
                   pl.BlockSpec((None, None, n_rows, HEAD_DIM), lambda b, g: (b, g, 0, 0))),
        compiler_params=_cparams(("arbitrary", "arbitrary")),
        name="nsa_compress",
    )(z, z, pe, pe, w1, w1, w2, w2, g_kc, cos_c, sin_c)


def _softmax_rows(s):
    m = jnp.max(s, axis=1, keepdims=True)
    e = jnp.exp(s - m)
    return e / jnp.maximum(jnp.sum(e, axis=1, keepdims=True), 1e-30)


def _flash_update_t(carry, s, vt):
    m, l, acc = carry
    m_new = jnp.maximum(m, jnp.max(s, axis=0, keepdims=True))
    alpha = jnp.exp2(m - m_new)
    p = jnp.exp2(s - m_new)
    l = alpha * l + jnp.sum(p, axis=0, keepdims=True)
    acc = alpha * acc + _dot(vt, p.astype(BF16))
    return m_new, l, acc


def _flash_init_t(tq, width):
    return (jnp.full((1, tq), NEG, F32), jnp.zeros((1, tq), F32), jnp.zeros((width, tq), F32))


def _causal_keep_t(tk, tq):
    return lax.broadcasted_iota(jnp.int32, (tk, tq), 0) <= lax.broadcasted_iota(jnp.int32, (tk, tq), 1)


Q_CHUNK = 512


def _flash_streams(carries_list, k_tiles, chunk_lists, vt, masks):
    scores = []
    for k_tile, q_chunks in zip(k_tiles, chunk_lists):
        row = []
        for qc, mk in zip(q_chunks, masks):
            if mk is False:
                row.append(None)
                continue
            s = _dot_nt(k_tile, qc)
            row.append(s if mk is None else jnp.where(mk, s, NEG))
        scores.append(row)
    return [tuple(c if s is None else _flash_update_t(c, s, vt) for c, s in zip(carries, row))
            for carries, row in zip(carries_list, scores)]


def _causal_tail(carries_list, q0, n_ch, load_kv, chunk_lists):
    keep = _causal_keep_t(Q_CHUNK, Q_CHUNK)
    for kb in range(n_ch):
        k0 = q0 + kb * Q_CHUNK
        k_tiles, vt = load_kv(k0 if isinstance(k0, int) else pl.multiple_of(k0, Q_CHUNK), Q_CHUNK)
        masks = [False if kb > c else (keep if kb == c else None) for c in range(n_ch)]
        carries_list = _flash_streams(carries_list, k_tiles, chunk_lists, vt, masks)
    return carries_list


def _nsa_attn_kernel(q_ref, kc_ref, vc_ref, ks_ref, kw_ref, vst_ref, vwt_ref, gt_ref,
                     ovt_ref, expt_ref, o_ref, m_ref, l_ref, acc_ref, *, tq, tk, seq):
    R = NSA_REP
    qi = pl.program_id(2)
    q0 = qi * tq
    n_cmp = kc_ref.shape[0]
    n_slc = seq // NSA_SLC_LEN
    q = q_ref[...]
    qs = jnp.concatenate([q[:, r * HEAD_DIM:(r + 1) * HEAD_DIM] for r in range(R)], axis=0)
    qpos1 = q0 + lax.broadcasted_iota(jnp.int32, (1, tq), 1)
    hp = Q_CHUNK // tq
    n_ch = R // hp
    q_chunks = [qs[c * Q_CHUNK:(c + 1) * Q_CHUNK] for c in range(n_ch)]
    qpos = jnp.concatenate([qpos1] * hp, axis=1)

    cmp_end = lax.broadcasted_iota(jnp.int32, (n_cmp, 1), 0) * NSA_CMP_STRIDE + (NSA_CMP_LEN - 1)
    vis = cmp_end <= qpos
    kc = kc_ref[...]
    vct = vc_ref[...].astype(F32).T.astype(BF16)
    p_cs, o_cs = [], []
    for qc in q_chunks:
        s_c = jnp.where(vis, _dot_nt(kc, qc), NEG)
        e_c = jnp.where(vis, jnp.exp2(s_c - jnp.max(s_c, axis=0, keepdims=True)), 0.0)
        p_c = e_c / jnp.maximum(jnp.sum(e_c, axis=0, keepdims=True), 1e-30)
        p_cs.append(p_c)
        o_cs.append(_dot(vct, p_c.astype(BF16)))

    p_sum = None
    for r in range(R):
        part = p_cs[r // hp][:, (r % hp) * tq:(r % hp + 1) * tq]
        p_sum = part if p_sum is None else p_sum + part
    p_hi = p_sum.astype(BF16)
    p_lo = (p_sum - p_hi.astype(F32)).astype(BF16)
    imp = _dot(ovt_ref[...], p_hi) + _dot(ovt_ref[...], p_lo)
    jblk = lax.broadcasted_iota(jnp.int32, (n_slc, tq), 0)
    qblk = qpos1 // NSA_SLC_LEN
    forced = (jblk == 0) | (jblk == qblk) | (jblk == qblk - 1)
    imp = jnp.where(jblk > qblk, -jnp.inf, jnp.where(forced, jnp.inf, imp))
    picked = jnp.zeros((n_slc, tq), jnp.int32)
    for _ in range(min(NSA_SLC_TOPK, n_slc)):
        cand = jnp.where(picked > 0, -jnp.inf, imp)
        best = jnp.max(cand, axis=0, keepdims=True)
        first = jnp.min(jnp.where(cand == best, jblk + picked * n_slc, n_slc), axis=0, keepdims=True)
        picked = jnp.where(jblk == first, 1, picked)
    sel_bias = jnp.where(picked > 0, 0.0, NEG).astype(BF16)

    def slc_tile(j, carries, causal):
        k0 = j * tk if isinstance(j, int) else pl.multiple_of(j * tk, tk)
        ks = ks_ref[pl.ds(k0, tk), :]
        vst = vst_ref[:, pl.ds(k0, tk)]
        bias = _dot(expt_ref[pl.ds(k0, tk), :], sel_bias)
        if causal:
            kpos = k0 + lax.broadcasted_iota(jnp.int32, (tk, 1), 0)
            bias = jnp.where(kpos <= qpos1, bias, NEG)
        bias = jnp.concatenate([bias] * hp, axis=1)
        scores = [_dot_nt(ks, qc) + bias for qc in q_chunks]
        return tuple(_flash_update_t(carry, s, vst) for carry, s in zip(carries, scores))

    last = q0 // tk
    n_tiles = seq // tk
    st_refs = (m_ref, l_ref, acc_ref)
    for c in range(n_ch):
        for ref, val in zip(st_refs, _flash_init_t(Q_CHUNK, HEAD_DIM)):
            ref[c] = val
    for count in range(1, n_tiles):
        @pl.when(last == count)
        def _():
            carries = tuple(_flash_init_t(Q_CHUNK, HEAD_DIM) for _ in range(n_ch))
            for j in range(count):
                carries = slc_tile(j, carries, False)
            for c in range(n_ch):
                for ref, val in zip(st_refs, carries[c]):
                    ref[c] = val
    carries = tuple(tuple(ref[c] for ref in st_refs) for c in range(n_ch))
    o_ss = [acc / jnp.maximum(l, 1e-30) for _, l, acc in slc_tile(last, carries, True)]

    span = NSA_WINDOW + tq
    w0 = pl.multiple_of(jnp.maximum(q0 - NSA_WINDOW, 0), tq)
    rel = qpos - (w0 + lax.broadcasted_iota(jnp.int32, (span, 1), 0))
    kw = kw_ref[pl.ds(w0, span), :]
    vwt = vwt_ref[:, pl.ds(w0, span)]
    o_ws = []
    for qc in q_chunks:
        s_w = jnp.where(rel >= 0, jnp.where(rel < NSA_WINDOW, _dot_nt(kw, qc), NEG), NEG)
        e_w = jnp.exp2(s_w - jnp.max(s_w, axis=0, keepdims=True))
        p_w = e_w / jnp.maximum(jnp.sum(e_w, axis=0, keepdims=True), 1e-30)
        o_ws.append(_dot(vwt, p_w.astype(BF16)))

    gates = jax.nn.sigmoid(gt_ref[...])
    outs = []
    for r in range(R):
        c, cols = r // hp, slice((r % hp) * tq, (r % hp + 1) * tq)
        o_r = (gates[3 * r:3 * r + 1] * o_cs[c][:, cols] + gates[3 * r + 1:3 * r + 2] * o_ss[c][:, cols]
               + gates[3 * r + 2:3 * r + 3] * o_ws[c][:, cols])
        outs.append(o_r.T)
    o_ref[...] = jnp.concatenate(outs, axis=1).astype(BF16)


NSA_GATE_ROWS = 16


def nsa_attention(zb, zt, gt, kc, vc, batch, seq, ks_col, kw_col, vs_row, vw_row):
    tq, tk = 2 * NSA_QBLOCK, 512
    G, R = NSA_GROUPS, NSA_REP
    tk = min(tk, seq)
    assert seq % tk == 0 and tk % tq == 0 and seq >= NSA_WINDOW + tq and 3 * R <= NSA_GATE_ROWS
    assert Q_CHUNK % tq == 0 and R % (Q_CHUNK // tq) == 0
    n_chains = R * tq // Q_CHUNK
    nq = seq // tq
    n_cmp = seq // NSA_CMP_STRIDE
    n_slc = seq // NSA_SLC_LEN
    starts = np.arange(n_cmp) * NSA_CMP_STRIDE
    sb = np.arange(n_slc) * NSA_SLC_LEN
    overlap = np.clip(np.minimum(starts[:, None] + NSA_CMP_LEN, sb[None, :] + NSA_SLC_LEN)
                      - np.maximum(starts[:, None], sb[None, :]), 0, None) / NSA_CMP_LEN
    expand_t = (np.arange(seq)[:, None] // NSA_SLC_LEN == np.arange(n_slc)[None, :]).astype(np.float32)
    kern = functools.partial(_nsa_attn_kernel, tq=tq, tk=tk, seq=seq)
    return pl.pallas_call(
        kern,
        out_shape=jax.ShapeDtypeStruct((batch * seq, G * R * HEAD_DIM), BF16),
        grid=(batch, G, nq),
        in_specs=[pl.BlockSpec((tq, R * HEAD_DIM), lambda b, g, i: (b * nq + i, g)),
                  pl.BlockSpec((None, None, n_cmp, HEAD_DIM), lambda b, g, i: (b, g, 0, 0)),
                  pl.BlockSpec((None, None, n_cmp, HEAD_DIM), lambda b, g, i: (b, g, 0, 0)),
                  pl.BlockSpec((seq, LANES), lambda b, g, i: (b, ks_col + g)),
                  pl.BlockSpec((seq, LANES), lambda b, g, i: (b, kw_col + g)),
                  pl.BlockSpec((None, HEAD_DIM, seq), lambda b, g, i: (b, vs_row + g, 0)),
                  pl.BlockSpec((None, HEAD_DIM, seq), lambda b, g, i: (b, vw_row + g, 0)),
                  pl.BlockSpec((None, NSA_GATE_ROWS, tq), lambda b, g, i: (b, g, i)),
                  pl.BlockSpec((n_slc, n_cmp), lambda b, g, i: (0, 0)),
                  pl.BlockSpec((seq, n_slc), lambda b, g, i: (0, 0))],
        out_specs=pl.BlockSpec((tq, R * HEAD_DIM), lambda b, g, i: (b * nq + i, g)),
        scratch_shapes=[pltpu.VMEM((n_chains, 1, Q_CHUNK), F32), pltpu.VMEM((n_chains, 1, Q_CHUNK), F32),
                        pltpu.VMEM((n_chains, HEAD_DIM, Q_CHUNK), F32)],
        compiler_params=_cparams(("arbitrary", "arbitrary", "arbitrary")),
        name="nsa_attention",
    )(zb, kc, vc, zb, zb, zt, zt, gt, jnp.asarray(overlap.T, BF16), jnp.asarray(expand_t, BF16))


def _diff_attn_kernel(q_ref, k_ref, vt_ref, lam_ref, subg_ref, o_ref, *, tq, tk, nq, lam_init):
    qi = pl.program_id(2)
    n_ch = tq // Q_CHUNK
    q = q_ref[...]
    q1 = [q[c * Q_CHUNK:(c + 1) * Q_CHUNK, :HEAD_DIM] for c in range(n_ch)]
    q2 = [q[c * Q_CHUNK:(c + 1) * Q_CHUNK, HEAD_DIM:] for c in range(n_ch)]

    def load_kv(k0, size):
        k = k_ref[pl.ds(k0, size), :]
        return [k[:, :HEAD_DIM], k[:, HEAD_DIM:]], vt_ref[:, pl.ds(k0, size)]

    width = vt_ref.shape[0]
    lp = lam_ref[...]
    lam = (jnp.exp(jnp.sum(lp[0:1] * lp[1:2], axis=1, keepdims=True))
           - jnp.exp(jnp.sum(lp[2:3] * lp[3:4], axis=1, keepdims=True)) + lam_init)

    for qv in range(nq):
        @pl.when(qi == qv)
        def _():
            carries = [tuple(_flash_init_t(Q_CHUNK, width) for _ in range(n_ch)) for _ in range(2)]
            for j in range(qv * (tq // tk)):
                k_tiles, vt = load_kv(j * tk, tk)
                carries = _flash_streams(carries, k_tiles, [q1, q2], vt, [None] * n_ch)
            c1, c2 = _causal_tail(carries, qv * tq, n_ch, load_kv, [q1, q2])
            o = jnp.concatenate([(a1 / jnp.maximum(l1, 1e-30) - lam * (a2 / jnp.maximum(l2, 1e-30))).T
                                 for (_, l1, a1), (_, l2, a2) in zip(c1, c2)], axis=0)
            o_ref[...] = (_rms(o, subg_ref[...]) * (1.0 - lam_init)).astype(BF16)


def diff_attention(zb, zt, batch, seq, q_col, k_col, vt_row, diff_lam, sub_g, layer, *, tq=1024, tk=512):
    H = DIFF_HEADS
    W2 = 2 * HEAD_DIM
    tq = min(tq, seq)
    tk = min(tk, tq)
    assert seq % tq == 0 and tq % tk == 0 and tq % Q_CHUNK == 0
    nq = seq // tq
    lam_init = 0.8 - 0.6 * math.exp(-0.3 * layer)
    kern = functools.partial(_diff_attn_kernel, tq=tq, tk=tk, nq=nq, lam_init=lam_init)
    return pl.pallas_call(
        kern,
        out_shape=jax.ShapeDtypeStruct((batch * seq, H * W2), BF16),
        grid=(batch, H, nq),
        in_specs=[pl.BlockSpec((tq, W2), lambda b, h, i: (b * nq + i, q_col + h)),
                  pl.BlockSpec((seq, W2), lambda b, h, i: (b, k_col + h)),
                  pl.BlockSpec((None, W2, seq), lambda b, h, i: (b, vt_row + h, 0)),
                  pl.BlockSpec((4, HEAD_DIM), lambda b, h, i: (0, 0)),
                  pl.BlockSpec((1, W2), lambda b, h, i: (0, 0))],
        out_specs=pl.BlockSpec((tq, W2), lambda b, h, i: (b * nq + i, h)),
        compiler_params=_cparams(("arbitrary", "arbitrary", "arbitrary")),
        name="diff_attention",
    )(zb, zb, zt, diff_lam, sub_g.reshape(1, W2))


FOX_AUG = 6


def _fox_attn_kernel(q_ref, qaug_ref, k_ref, kaug_ref, vt_ref, o_ref, *, tq, tk, nq):
    h = pl.program_id(1)
    qi = pl.program_id(2)
    n_ch = tq // Q_CHUNK
    lane = lax.broadcasted_iota(jnp.int32, (tq, LANES), 1)
    mine = (lane >= h * FOX_AUG) & (lane < (h + 1) * FOX_AUG)
    q2 = jnp.concatenate([q_ref[...], jnp.where(mine, qaug_ref[...], jnp.zeros_like(qaug_ref))], axis=1)
    q_chunks = [q2[c * Q_CHUNK:(c + 1) * Q_CHUNK] for c in range(n_ch)]

    def load_kv(k0, size):
        k2 = jnp.concatenate([k_ref[pl.ds(k0, size), :], kaug_ref[pl.ds(k0, size), :]], axis=1)
        return [k2], vt_ref[:, pl.ds(k0, size)]

    for qv in range(nq):
        @pl.when(qi == qv)
        def _():
            carries = tuple(_flash_init_t(Q_CHUNK, HEAD_DIM) for _ in range(n_ch))
            for j in range(qv * (tq // tk)):
                k_tiles, vt = load_kv(j * tk, tk)
                (carries,) = _flash_streams([carries], k_tiles, [q_chunks], vt, [None] * n_ch)
            (carries,) = _causal_tail([carries], qv * tq, n_ch, load_kv, [q_chunks])
            o_ref[...] = jnp.concatenate([(acc / jnp.maximum(l, 1e-30)).T for _, l, acc in carries],
                                         axis=0).astype(BF16)


def fox_attention(zb, zt, qaug, kaug, batch, seq, *, tq=2048, tk=512):
    H = FOX_HEADS
    assert H * FOX_AUG <= LANES
    tq = min(tq, seq)
    tk = min(tk, tq)
    assert seq % tq == 0 and tq % tk == 0 and tq % Q_CHUNK == 0
    nq = seq // tq
    kern = functools.partial(_fox_attn_kernel, tq=tq, tk=tk, nq=nq)
    return pl.pallas_call(
        kern,
        out_shape=jax.ShapeDtypeStruct((batch * seq, H * HEAD_DIM), BF16),
        grid=(batch, H, nq),
        in_specs=[pl.BlockSpec((tq, LANES), lambda b, h, i: (b * nq + i, h)),
                  pl.BlockSpec((tq, LANES), lambda b, h, i: (b * nq + i, 0)),
                  pl.BlockSpec((seq, LANES), lambda b, h, i: (b, H + h)),
                  pl.BlockSpec((seq, LANES), lambda b, h, i: (b, 0)),
                  pl.BlockSpec((None, HEAD_DIM, seq), lambda b, h, i: (b, h, 0))],
        out_specs=pl.BlockSpec((tq, LANES), lambda b, h, i: (b * nq + i, h)),
        compiler_params=_cparams(("arbitrary", "arbitrary", "arbitrary")),
        name="fox_attention",
    )(zb, qaug, zb, kaug, zt)


def _split3(x):
    p1 = x.astype(BF16)
    r1 = x - p1.astype(F32)
    p2 = r1.astype(BF16)
    p3 = (r1 - p2.astype(F32)).astype(BF16)
    return p1, p2, p3


def _fox_gate_kernel(z_ref, b_ref, place_ref, pat_ref, qaug_ref, kaug_ref, carry_ref, *, tr):
    @pl.when(pl.program_id(1) == 0)
    def _():
        carry_ref[...] = jnp.zeros_like(carry_ref)

    logf = jax.nn.log_sigmoid(z_ref[...] + b_ref[...])
    tri = (lax.broadcasted_iota(jnp.int32, (tr, tr), 1) <= lax.broadcasted_iota(jnp.int32, (tr, tr), 0)).astype(BF16)
    p1, p2, p3 = _split3(logf)
    c = _dot(tri, p1) + _dot(tri, p2) + _dot(tri, p3) + carry_ref[...]
    carry_ref[...] = c[tr - 1:tr, :]
    e1, e2, e3 = _split3(c * LOG2E)
    place = place_ref[...]
    cexp = _dot(e1, place) + _dot(e2, place) + _dot(e3, place)
    h1, h2, h3 = _split3(cexp)
    pat = pat_ref[...]
    piece = h1.astype(F32) * pat[0:1] + h2.astype(F32) * pat[1:2] + h3.astype(F32) * pat[2:3]
    kaug_ref[...] = (pat[4:5] - piece * pat[3:4]).astype(BF16)
    qaug_ref[...] = (pat[3:4] + piece * pat[4:5]).astype(BF16)


def fox_gate_bias(z, col_block, bias_row, batch, seq, *, tr=256):
    H = FOX_HEADS
    tr = min(tr, seq)
    nt = seq // tr
    lanes = np.arange(LANES)
    used = lanes < H * FOX_AUG
    place = (lanes[None, :] // FOX_AUG == np.arange(LANES)[:, None]) & used[None, :] & (np.arange(LANES)[:, None] < H)
    j = lanes % FOX_AUG
    pat = np.zeros((8, LANES), np.float32)
    for r in range(3):
        pat[r] = used & (j % 3 == r)
    pat[3] = used & (j < 3)
    pat[4] = used & (j >= 3)
    kern = functools.partial(_fox_gate_kernel, tr=tr)
    out = jax.ShapeDtypeStruct((batch * seq, LANES), BF16)
    return pl.pallas_call(
        kern,
        out_shape=(out, out),
        grid=(batch, nt),
        in_specs=[pl.BlockSpec((tr, LANES), lambda b, i: (b * nt + i, col_block)),
                  pl.BlockSpec((1, LANES), lambda b, i: (0, 0)),
                  pl.BlockSpec((LANES, LANES), lambda b, i: (0, 0)),
                  pl.BlockSpec((8, LANES), lambda b, i: (0, 0))],
        out_specs=(pl.BlockSpec((tr, LANES), lambda b, i: (b * nt + i, 0)),
                   pl.BlockSpec((tr, LANES), lambda b, i: (b * nt + i, 0))),
        scratch_shapes=[pltpu.VMEM((1, LANES), F32)],
        compiler_params=_cparams(("arbitrary", "arbitrary")),
        name="fox_gate_bias",
    )(z, bias_row, jnp.asarray(place, BF16), jnp.asarray(pat))


def _cross_attn_kernel(x_ref, g_ref, wq_ref, kv_ref, qg_ref, kg_ref, wo_ref, o_ref):
    x = x_ref[...]
    q = _dot(_rms(x, g_ref[...]).astype(BF16), wq_ref[...])
    kv = kv_ref[...]
    width = MEM_HEADS * HEAD_DIM
    outs = []
    for h in range(MEM_HEADS):
        cols = slice(h * HEAD_DIM, (h + 1) * HEAD_DIM)
        qh = _rms(q[:, cols], qg_ref[...]).astype(BF16)
        kh = _rms(kv[:, cols], kg_ref[...]).astype(BF16)
        vh = kv[:, width + h * HEAD_DIM: width + (h + 1) * HEAD_DIM].astype(BF16)
        p = _softmax_rows(_dot_nt(qh, kh))
        outs.append(_dot(p.astype(BF16), vh))
    o = jnp.concatenate(outs, axis=1).astype(BF16)
    o_ref[...] = x + _dot(o, wo_ref[...])


def cross_attention(x, g, wq, kv, q_gain_scaled, k_gain, wo, seq, mem_len, *, tm=512):
    M, D = x.shape
    tm = min(tm, seq)
    nt = seq // tm
    width = MEM_HEADS * HEAD_DIM
    return pl.pallas_call(
        _cross_attn_kernel,
        out_shape=jax.ShapeDtypeStruct((M, D), F32),
        grid=(M // tm,),
        in_specs=[pl.BlockSpec((tm, D), lambda i: (i, 0)),
                  pl.BlockSpec((1, D), lambda i: (0, 0)),
                  pl.BlockSpec((D, width), lambda i: (0, 0)),
                  pl.BlockSpec((mem_len, 2 * width), lambda i: (i // nt, 0)),
                  pl.BlockSpec((1, HEAD_DIM), lambda i: (0, 0)),
                  pl.BlockSpec((1, HEAD_DIM), lambda i: (0, 0)),
                  pl.BlockSpec((width, D), lambda i: (0, 0))],
        out_specs=pl.BlockSpec((tm, D), lambda i: (i, 0)),
        compiler_params=_cparams(("arbitrary",)),
        name="cross_attention",
    )(x, g.reshape(1, D), wq, kv, q_gain_scaled.reshape(1, HEAD_DIM), k_gain.reshape(1, HEAD_DIM), wo)


def _moe_router_kernel(x_ref, g_ref, wr_ref, br_ref, hn_ref, route_ref, onehot_ref):
    hn = _rms(x_ref[...], g_ref[...])
    hn_ref[...] = hn
    w = wr_ref[...]
    w_hi = w.astype(BF16)
    w_lo = (w - w_hi.astype(F32)).astype(BF16)
    h_hi = hn.astype(BF16)
    h_lo = (hn - h_hi.astype(F32)).astype(BF16)
    logits = _dot(h_hi, w_hi) + _dot(h_lo, w_hi) + _dot(h_hi, w_lo) + br_ref[...]
    tm = logits.shape[0]
    lane = lax.broadcasted_iota(jnp.int32, (tm, LANES), 1)
    is_grp = lane < N_GROUPS
    lg = jnp.where(is_grp, logits, -jnp.inf)
    eg = jnp.exp(lg - jnp.max(lg, axis=1, keepdims=True))
    p_grp = eg / jnp.sum(eg, axis=1, keepdims=True)
    p_top = jnp.max(p_grp, axis=1, keepdims=True)
    grp = jnp.min(jnp.where(is_grp & (p_grp == p_top), lane, LANES), axis=1, keepdims=True)
    lo = N_GROUPS + grp * EXPERTS_PER_GROUP
    in_grp = (lane >= lo) & (lane < lo + EXPERTS_PER_GROUP)
    le = jnp.where(in_grp, logits, -jnp.inf)
    ee = jnp.exp(le - jnp.max(le, axis=1, keepdims=True))
    p_in = ee / jnp.sum(ee, axis=1, keepdims=True)
    m1 = jnp.max(p_in, axis=1, keepdims=True)
    i1 = jnp.min(jnp.where(in_grp & (p_in == m1), lane, LANES), axis=1, keepdims=True)
    rest = jnp.where(in_grp & (lane != i1), p_in, -jnp.inf)
    m2 = jnp.max(rest, axis=1, keepdims=True)
    i2 = jnp.min(jnp.where(rest == m2, lane, LANES), axis=1, keepdims=True)
    denom = m1 + m2
    w1 = p_top * m1 / denom
    w2 = p_top * m2 / denom
    e1 = (i1 - N_GROUPS).astype(F32)
    e2 = (i2 - N_GROUPS).astype(F32)
    route_ref[...] = jnp.where(lane == 0, e1, jnp.where(lane == 1, e2, jnp.where(lane == 2, w1, jnp.where(lane == 3, w2, 0.0))))
    onehot_ref[...] = ((lane == i1 - N_GROUPS) | (lane == i2 - N_GROUPS)).astype(BF16)


def moe_router(x, g, w_router, b_router, *, tm=512):
    M, D = x.shape
    tm = min(tm, M)
    return pl.pallas_call(
        _moe_router_kernel,
        out_shape=(jax.ShapeDtypeStruct((M, D), F32),
                   jax.ShapeDtypeStruct((M, LANES), F32),
                   jax.ShapeDtypeStruct((M, LANES), BF16)),
        grid=(M // tm,),
        in_specs=[pl.BlockSpec((tm, D), lambda i: (i, 0)),
                  pl.BlockSpec((1, D), lambda i: (0, 0)),
                  pl.BlockSpec((D, LANES), lambda i: (0, 0)),
                  pl.BlockSpec((1, LANES), lambda i: (0, 0))],
        out_specs=(pl.BlockSpec((tm, D), lambda i: (i, 0)),
                   pl.BlockSpec((tm, LANES), lambda i: (i, 0)),
                   pl.BlockSpec((tm, LANES), lambda i: (i, 0))),
        compiler_params=_cparams(("arbitrary",)),
        name="moe_router",
    )(x, g.reshape(1, D), w_router, b_router)


def _moe_rank_kernel(onehot_ref, rank_ref, count_ref, carry_ref, *, tr):
    i = pl.program_id(0)

    @pl.when(i == 0)
    def _():
        carry_ref[...] = jnp.zeros_like(carry_ref)

    oh = onehot_ref[...]
    tri = (lax.broadcasted_iota(jnp.int32, (tr, tr), 1) < lax.broadcasted_iota(jnp.int32, (tr, tr), 0)).astype(BF16)
    rank_ref[...] = _dot(tri, oh) + carry_ref[...]
    carry_ref[...] = carry_ref[...] + jnp.sum(oh.astype(F32), axis=0, keepdims=True)
    count_ref[...] = carry_ref[...]


def moe_rank(onehot, *, tr=512):
    M = onehot.shape[0]
    tr = min(tr, M)
    kern = functools.partial(_moe_rank_kernel, tr=tr)
    return pl.pallas_call(
        kern,
        out_shape=(jax.ShapeDtypeStruct((M, LANES), F32), jax.ShapeDtypeStruct((1, LANES), F32)),
        grid=(M // tr,),
        in_specs=[pl.BlockSpec((tr, LANES), lambda i: (i, 0))],
        out_specs=(pl.BlockSpec((tr, LANES), lambda i: (i, 0)), pl.BlockSpec((1, LANES), lambda i: (0, 0))),
        scratch_shapes=[pltpu.VMEM((1, LANES), F32)],
        compiler_params=_cparams(("arbitrary",)),
        name="moe_rank",
    )(onehot)


def _moe_slot_kernel(route_ref, rank_ref, pstart_ref, o_ref):
    route = route_ref[...]
    slot = rank_ref[...] + pstart_ref[...]
    lane = lax.broadcasted_iota(jnp.int32, route.shape, 1)
    e1 = route[:, 0:1].astype(jnp.int32)
    e2 = route[:, 1:2].astype(jnp.int32)
    d1 = jnp.sum(jnp.where(lane == e1, slot, 0.0), axis=1, keepdims=True)
    d2 = jnp.sum(jnp.where(lane == e2, slot, 0.0), axis=1, keepdims=True)
    o_ref[...] = jnp.where(lane == 0, d1, jnp.where(lane == 1, d2, 0.0)).astype(jnp.int32)


def moe_slots(route, rank, pstart_row, *, tm=1024):
    M = route.shape[0]
    tm = min(tm, M)
    return pl.pallas_call(
        _moe_slot_kernel,
        out_shape=jax.ShapeDtypeStruct((M, LANES), jnp.int32),
        grid=(M // tm,),
        in_specs=[pl.BlockSpec((tm, LANES), lambda i: (i, 0)),
                  pl.BlockSpec((tm, LANES), lambda i: (i, 0)),
                  pl.BlockSpec((1, LANES), lambda i: (0, 0))],
        out_specs=pl.BlockSpec((tm, LANES), lambda i: (i, 0)),
        compiler_params=_cparams(("arbitrary",)),
        name="moe_slots",
    )(route, rank, pstart_row)


def _moe_dispatch_kernel(lastblk_ref, d1_hbm, d2_hbm, hn_ref, xb_hbm, zero_ref, s1_ref, s2_ref,
                         idx_sem, row_sem, zero_sem, *, tm, n_blocks):
    i = pl.program_id(0)

    def zero_copy(blk):
        return pltpu.make_async_copy(zero_ref, xb_hbm.at[pl.ds(blk * MOE_ROWS, MOE_ROWS)], zero_sem)

    @pl.when(i == 0)
    def _():
        zero_ref[...] = jnp.zeros_like(zero_ref)
        n_used = lastblk_ref[N_EXPERTS]
        for e in range(N_EXPERTS):
            @pl.when(lastblk_ref[e] >= 0)
            def _():
                zero_copy(lastblk_ref[e]).start()

        def start_tail(blk, _):
            zero_copy(blk).start()
            return 0

        def wait_tail(blk, _):
            zero_copy(blk).wait()
            return 0

        lax.fori_loop(n_used, n_blocks, start_tail, 0)
        for e in range(N_EXPERTS):
            @pl.when(lastblk_ref[e] >= 0)
            def _():
                zero_copy(lastblk_ref[e]).wait()
        lax.fori_loop(n_used, n_blocks, wait_tail, 0)

    c1 = pltpu.make_async_copy(d1_hbm.at[i], s1_ref, idx_sem.at[0])
    c2 = pltpu.make_async_copy(d2_hbm.at[i], s2_ref, idx_sem.at[1])
    c1.start()
    c2.start()
    c1.wait()
    c2.wait()

    def row_copy(r, slot):
        return pltpu.make_async_copy(hn_ref.at[pl.ds(r, 1)], xb_hbm.at[pl.ds(slot, 1)], row_sem)

    def issue(blk, _):
        for u in range(ROW_DMA_UNROLL):
            r = blk * ROW_DMA_UNROLL + u
            row_copy(r, s1_ref[r]).start(priority=0)
            row_copy(r, s2_ref[r]).start(priority=1)
        return 0

    lax.fori_loop(0, tm // ROW_DMA_UNROLL, issue, 0)

    def drain(r, _):
        row_copy(r, 0).wait()
        row_copy(r, 0).wait()
        return 0

    lax.fori_loop(0, tm, drain, 0, unroll=8)


def moe_dispatch(hn, d1, d2, lastblk, n_slots, *, tm=512):
    M, D = hn.shape
    tm = min(tm, M)
    assert n_slots % MOE_ROWS == 0
    kern = functools.partial(_moe_dispatch_kernel, tm=tm, n_blocks=n_slots // MOE_ROWS)
    grid_spec = pltpu.PrefetchScalarGridSpec(
        num_scalar_prefetch=1,
        grid=(M // tm,),
        in_specs=[pl.BlockSpec(memory_space=pl.ANY), pl.BlockSpec(memory_space=pl.ANY),
                  pl.BlockSpec((tm, D), lambda i, *_: (i, 0))],
        out_specs=pl.BlockSpec(memory_space=pl.ANY),
        scratch_shapes=[pltpu.VMEM((MOE_ROWS, D), F32),
                        pltpu.SMEM((tm,), jnp.int32), pltpu.SMEM((tm,), jnp.int32),
                        pltpu.SemaphoreType.DMA((2,)), pltpu.SemaphoreType.DMA, pltpu.SemaphoreType.DMA],
    )
    return pl.pallas_call(
        kern,
        out_shape=jax.ShapeDtypeStruct((n_slots, D), F32),
        grid_spec=grid_spec,
        compiler_params=_cparams(("arbitrary",)),
        name="moe_dispatch",
    )(lastblk, d1.reshape(M // tm, tm), d2.reshape(M // tm, tm), hn)


def _moe_expert_kernel(blk_e_ref, n_used_ref, x_ref, w1_ref, w3_ref, w2_ref, o_ref, w1b, w3b, w2b):
    i = pl.program_id(0)

    @pl.when(i < n_used_ref[0])
    def _():
        prev = blk_e_ref[jnp.maximum(i - 1, 0)]

        @pl.when((i == 0) | (blk_e_ref[i] != prev))
        def _():
            w1b[...] = w1_ref[...].astype(BF16)
            w3b[...] = w3_ref[...].astype(BF16)
            w2b[...] = w2_ref[...].astype(BF16)

        x = x_ref[...].astype(BF16)
        h = (jax.nn.silu(_dot(x, w1b[...])) * _dot(x, w3b[...])).astype(BF16)
        o_ref[...] = _dot(h, w2b[...])

    @pl.when(i >= n_used_ref[0])
    def _():
        o_ref[...] = jnp.zeros_like(o_ref)


def moe_experts(xb, blk_e, n_used, w1, w3, w2, layer):
    P, D = xb.shape
    FF = w1.shape[-1]
    n_blk = P // MOE_ROWS

    def row_map(i, blk_e_ref, n_used_ref):
        return (jnp.minimum(i, n_used_ref[0] - 1), 0)

    def out_map(i, blk_e_ref, n_used_ref):
        return (i, 0)

    def w_map(i, blk_e_ref, n_used_ref):
        return (layer, blk_e_ref[i], 0, 0)

    grid_spec = pltpu.PrefetchScalarGridSpec(
        num_scalar_prefetch=2,
        grid=(n_blk,),
        in_specs=[pl.BlockSpec((MOE_ROWS, D), row_map),
                  pl.BlockSpec((None, None, D, FF), w_map),
                  pl.BlockSpec((None, None, D, FF), w_map),
                  pl.BlockSpec((None, None, FF, D), w_map)],
        out_specs=pl.BlockSpec((MOE_ROWS, D), out_map),
        scratch_shapes=[pltpu.VMEM((D, FF), BF16), pltpu.VMEM((D, FF), BF16), pltpu.VMEM((FF, D), BF16)],
    )
    return pl.pallas_call(
        _moe_expert_kernel,
        out_shape=jax.ShapeDtypeStruct((P, D), F32),
        grid_spec=grid_spec,
        compiler_params=_cparams(("arbitrary",)),
        name="moe_experts",
    )(blk_e, n_used, xb, w1, w3, w2)


def _moe_combine_kernel(x_ref, route_ref, d1_hbm, d2_hbm, yb_hbm, o_ref, y1_ref, y2_ref, s1_ref, s2_ref,
                        idx_sem, row_sem, *, tm):
    i = pl.program_id(0)
    c1 = pltpu.make_async_copy(d1_hbm.at[i], s1_ref, idx_sem.at[0])
    c2 = pltpu.make_async_copy(d2_hbm.at[i], s2_ref, idx_sem.at[1])
    c1.start()
    c2.start()
    c1.wait()
    c2.wait()

    def row_copy(slot, dst, r):
        return pltpu.make_async_copy(yb_hbm.at[pl.ds(slot, 1)], dst.at[pl.ds(r, 1)], row_sem)

    def issue(blk, _):
        for u in range(ROW_DMA_UNROLL):
            r = blk * ROW_DMA_UNROLL + u
            row_copy(s1_ref[r], y1_ref, r).start(priority=0)
            row_copy(s2_ref[r], y2_ref, r).start(priority=1)
        return 0

    lax.fori_loop(0, tm // ROW_DMA_UNROLL, issue, 0)

    def drain(r, _):
        row_copy(0, y1_ref, r).wait()
        row_copy(0, y2_ref, r).wait()
        return 0

    lax.fori_loop(0, tm, drain, 0, unroll=8)
    route = route_ref[...]
    o_ref[...] = x_ref[...] + (y1_ref[...] * route[:, 2:3] + y2_ref[...] * route[:, 3:4])


def moe_combine(x, route, d1, d2, yb, *, tm=256):
    M, D = x.shape
    tm = min(tm, M)
    kern = functools.partial(_moe_combine_kernel, tm=tm)
    return pl.pallas_call(
        kern,
        out_shape=jax.ShapeDtypeStruct((M, D), F32),
        grid=(M // tm,),
        in_specs=[pl.BlockSpec((tm, D), lambda i: (i, 0)),
                  pl.BlockSpec((tm, LANES), lambda i: (i, 0)),
                  pl.BlockSpec(memory_space=pl.ANY),
                  pl.BlockSpec(memory_space=pl.ANY),
                  pl.BlockSpec(memory_space=pl.ANY)],
        out_specs=pl.BlockSpec((tm, D), lambda i: (i, 0)),
        scratch_shapes=[pltpu.VMEM((tm, D), F32), pltpu.VMEM((tm, D), F32),
                        pltpu.SMEM((tm,), jnp.int32), pltpu.SMEM((tm,), jnp.int32),
                        pltpu.SemaphoreType.DMA((2,)), pltpu.SemaphoreType.DMA],
        compiler_params=_cparams(("arbitrary",)),
        name="moe_combine",
    )(x, route, d1.reshape(M // tm, tm), d2.reshape(M // tm, tm), yb)


def hier_moe(x, g, wg, bg, we, be, w1, w3, w2, layer):
    M, D = x.shape
    n_route = N_GROUPS + N_EXPERTS
    w_router = jnp.pad(jnp.concatenate([wg, we], axis=1), ((0, 0), (0, LANES - n_route)))
    b_router = jnp.pad(jnp.concatenate([bg, be]), (0, LANES - n_route)).reshape(1, LANES)
    hn, route, onehot = moe_router(x, g, w_router, b_router)
    rank, counts = moe_rank(onehot)
    cnt = counts[0, :N_EXPERTS].astype(jnp.int32)
    nblk = (cnt + MOE_ROWS - 1) // MOE_ROWS
    bend = jnp.cumsum(nblk)
    bstart = bend - nblk
    n_blk_max = (2 * M) // MOE_ROWS + N_EXPERTS
    pstart_row = jnp.pad((bstart * MOE_ROWS).astype(F32), (0, LANES - N_EXPERTS)).reshape(1, LANES)
    slots = moe_slots(route, rank, pstart_row)
    d1, d2 = slots[:, 0], slots[:, 1]
    n_used = bend[-1:].astype(jnp.int32)
    lastblk = jnp.concatenate([jnp.where(nblk > 0, bend - 1, -1).astype(jnp.int32), n_used])
    blk_ids = jnp.arange(n_blk_max, dtype=jnp.int32)
    blk_e = jnp.minimum(jnp.sum(bend[None, :] <= blk_ids[:, None], axis=1), N_EXPERTS - 1).astype(jnp.int32)
    xb = moe_dispatch(hn, d1, d2, lastblk, n_blk_max * MOE_ROWS)
    yb = moe_experts(xb, blk_e, n_used, w1, w3, w2, layer)
    return moe_combine(x, route, d1, d2, yb)


def _even_mixer(x, g, w_in, w_out, nsa_qk_g, cmp_pe, cmp_w1, cmp_w2, diff_qk_g, diff_lam, diff_sub_g,
                layer, batch, seq, cos, sin):
    M, D = x.shape
    scale = HEAD_DIM ** -0.5
    nq_w = NSA_HEADS * HEAD_DIM
    kv_w = 6 * NSA_GROUPS * HEAD_DIM
    gate_w = 3 * NSA_HEADS
    dq_w = 2 * DIFF_HEADS * HEAD_DIM
    dv_w = DIFF_HEADS * 2 * HEAD_DIM
    c_gate = nq_w + kv_w
    c_diff = c_gate + gate_w
    main_w = nq_w + kv_w + 2 * dq_w + dv_w
    G, R3 = NSA_GROUPS, 3 * NSA_REP
    gate_cols = []
    for grp in range(G):
        gate_cols += [w_in[:, c_gate + grp * R3:c_gate + (grp + 1) * R3], jnp.zeros((D, NSA_GATE_ROWS - R3), w_in.dtype)]
    gate_cols.append(jnp.zeros((D, LANES - G * NSA_GATE_ROWS), w_in.dtype))
    w_cat = jnp.concatenate([w_in[:, :c_gate], w_in[:, c_diff:]] + gate_cols, axis=1).astype(BF16)
    b_kv, b_dq, b_dk, b_dv, b_gate = nq_w // LANES, (nq_w + kv_w) // LANES, (nq_w + kv_w + dq_w) // LANES, \
        (nq_w + kv_w + 2 * dq_w) // LANES, main_w // LANES
    qs = scale * LOG2E
    gains = jnp.stack([nsa_qk_g[0] * qs, nsa_qk_g[2], nsa_qk_g[3], diff_qk_g[0] * qs, diff_qk_g[1]])
    modes = ([("row", 0, True)] * (b_kv - 0)
             + [("raw", None, False)] * (2 * G)
             + [("row", 1, True)] * G + [("t", None, False)] * G
             + [("row", 2, True)] * G + [("t", None, False)] * G
             + [("row", 3, True)] * (b_dk - b_dq) + [("row", 4, True)] * (b_dv - b_dk)
             + [("t", None, False)] * (b_gate - b_dv)
             + [("t32", None, False)])
    proj = head_projection(x, g, w_cat, modes, gains, cos, sin, batch, seq, name="even_in_proj")
    zb, zt, gt, zc = proj["row"], proj["t"], proj["t32"], proj["raw"]
    r_ks = b_kv
    r_kw, r_dq = r_ks + G, r_ks + 2 * G
    r_dk = r_dq + (b_dk - b_dq)
    cmp_end = np.arange(seq // NSA_CMP_STRIDE) * NSA_CMP_STRIDE + NSA_CMP_LEN - 1
    cos_c, sin_c = rope_tables(cmp_end)
    kc, vc = nsa_compress(zc, 0, batch, seq, cmp_pe, cmp_w1.astype(BF16), cmp_w2.astype(BF16), nsa_qk_g[1],
                          cos_c, sin_c)
    o_nsa = nsa_attention(zb, zt, gt, kc, vc, batch, seq, r_ks, r_kw, 0, G)
    o_diff = diff_attention(zb, zt, batch, seq, r_dq // 2, r_dk // 2, (2 * G) // 2, diff_lam, diff_sub_g, layer)
    return matmul_residual([o_nsa, o_diff], w_out.astype(BF16), x, name="even_out_proj")


def _odd_mixer(x, g, w_in, w_out, f_b, qk_g, batch, seq, cos, sin):
    M, D = x.shape
    H = FOX_HEADS
    scale = HEAD_DIM ** -0.5
    width = H * HEAD_DIM
    w_cat = jnp.pad(w_in, ((0, 0), (0, LANES - H))).astype(BF16)
    modes = [("row", 0, False)] * H + [("row", 1, False)] * H + [("t", None, False)] * H + [("raw", None, False)]
    gains = jnp.stack([qk_g[0] * (scale * LOG2E), qk_g[1]])
    proj = head_projection(x, g, w_cat, modes, gains, cos, sin, batch, seq, name="odd_in_proj")
    bias_row = jnp.pad(f_b, (0, LANES - H)).reshape(1, LANES)
    qaug, kaug = fox_gate_bias(proj["raw"], 0, bias_row, batch, seq)
    o = fox_attention(proj["row"], proj["t"], qaug, kaug, batch, seq)
    return matmul_residual([o], w_out.astype(BF16), x, name="odd_out_proj")


def kernel(x, mem, norm_g, ev_w_in, ev_w_out, nsa_qk_g, nsa_cmp_pe, nsa_cmp_w1, nsa_cmp_w2, diff_qk_g, diff_lam, diff_sub_g, od_w_in, od_w_out, fox_f_b, fox_qk_g, ca_wq, ca_wkv, ca_qk_g, ca_wo, moe_wg, moe_bg, moe_we, moe_be, moe_w1, moe_w3, moe_w2):
    B, T, D = x.shape
    mem_len = mem.shape[1]
    depth = norm_g.shape[0]
    scale = HEAD_DIM ** -0.5
    cos, sin = rope_tables(np.arange(T))
    xt = x.reshape(B * T, D)
    mt = mem.reshape(B * mem_len, D)
    for layer in range(depth):
        i = layer // 2
        if layer % 2 == 0:
            xt = _even_mixer(xt, norm_g[layer, 0], ev_w_in[i], ev_w_out[i], nsa_qk_g[i], nsa_cmp_pe[i],
                             nsa_cmp_w1[i], nsa_cmp_w2[i], diff_qk_g[i], diff_lam[i], diff_sub_g[i], layer,
                             B, T, cos, sin)
        else:
            xt = _odd_mixer(xt, norm_g[layer, 0], od_w_in[i], od_w_out[i], fox_f_b[i], fox_qk_g[i], B, T, cos, sin)
        kv = rms_matmul(mt, norm_g[layer, 2], ca_wkv[layer].astype(BF16), name="mem_kv_proj")
        xt = cross_attention(xt, norm_g[layer, 1], ca_wq[layer].astype(BF16), kv, ca_qk_g[layer, 0] * scale,
                             ca_qk_g[layer, 1], ca_wo[layer].astype(BF16), T, mem_len)
        xt = hier_moe(xt, norm_g[layer, 3], moe_wg[layer], moe_bg[layer], moe_we[layer], moe_be[layer],
                      moe_w1, moe_w3, moe_w2, layer)
    return xt.reshape(B, T, D)
```

```python
import functools
import math

import numpy as np
import jax
import jax.numpy as jnp
from jax import lax
from jax.experimental import pallas as pl
from jax.experimental.pallas import tpu as pltpu

F32 = jnp.float32
BF16 = jnp.bfloat16

HEAD_DIM = 128
ROPE_THETA = 10000.0
EPS = 1e-6
NEG = -1e30
LOG2E = math.log2(math.e)

NSA_HEADS = 8
NSA_GROUPS = 2
NSA_REP = NSA_HEADS // NSA_GROUPS
NSA_CMP_LEN = 32
NSA_CMP_STRIDE = 16
NSA_CMP_HIDDEN = 256
NSA_SLC_LEN = 64
NSA_SLC_TOPK = 8
NSA_WINDOW = 512
NSA_QBLOCK = 128
DIFF_HEADS = 4
FOX_HEADS = 16
MEM_HEADS = 4
N_GROUPS = 4
EXPERTS_PER_GROUP = 8
N_EXPERTS = N_GROUPS * EXPERTS_PER_GROUP
MOE_ROWS = 256
ROW_DMA_UNROLL = 8
LANES = 128
VMEM_LIMIT = 56 * 1024 * 1024


def _cparams(sem):
    return pltpu.CompilerParams(dimension_semantics=sem, vmem_limit_bytes=VMEM_LIMIT)


def _dot(a, b):
    return jnp.dot(a, b, preferred_element_type=F32)


def _dot_nt(a, b):
    return lax.dot_general(a, b, (((1,), (1,)), ((), ())), preferred_element_type=F32)


def _rms(x, g):
    ms = jnp.mean(x * x, axis=-1, keepdims=True)
    return x * lax.rsqrt(ms + EPS) * g


def _rope(y, cos, sin_signed):
    return y * cos + pltpu.roll(y, HEAD_DIM // 2, 1) * sin_signed


def _rms_matmul_kernel(x_ref, g_ref, w_ref, o_ref, xn_ref):
    @pl.when(pl.program_id(1) == 0)
    def _():
        xn_ref[...] = _rms(x_ref[...], g_ref[...]).astype(BF16)

    o_ref[...] = _dot(xn_ref[...], w_ref[...])


def rms_matmul(x, g, w, *, tm=1024, tn=512, name="rms_matmul"):
    M, K = x.shape
    N = w.shape[1]
    tm, tn = min(tm, M), min(tn, N)
    assert M % tm == 0 and N % tn == 0
    return pl.pallas_call(
        _rms_matmul_kernel,
        out_shape=jax.ShapeDtypeStruct((M, N), F32),
        grid=(M // tm, N // tn),
        in_specs=[pl.BlockSpec((tm, K), lambda i, j: (i, 0)),
                  pl.BlockSpec((1, K), lambda i, j: (0, 0)),
                  pl.BlockSpec((K, tn), lambda i, j: (0, j))],
        out_specs=pl.BlockSpec((tm, tn), lambda i, j: (i, j)),
        scratch_shapes=[pltpu.VMEM((tm, K), BF16)],
        compiler_params=_cparams(("arbitrary", "arbitrary")),
        name=name,
    )(x, g.reshape(1, K), w)


def _matmul_res_kernel(*refs, tn):
    *x_refs, w_ref, r_ref, o_ref = refs
    for n0 in range(0, o_ref.shape[1], tn):
        acc = r_ref[:, n0:n0 + tn]
        k0 = 0
        for x_ref in x_refs:
            k = x_ref.shape[1]
            acc = acc + _dot(x_ref[...], w_ref[k0:k0 + k, n0:n0 + tn])
            k0 += k
        o_ref[:, n0:n0 + tn] = acc


def matmul_residual(xs, w, res, *, tm=512, tn=512, name="matmul_residual"):
    M = xs[0].shape[0]
    N = w.shape[1]
    tm, tn = min(tm, M), min(tn, N)
    assert M % tm == 0 and N % tn == 0 and sum(x.shape[1] for x in xs) == w.shape[0]
    return pl.pallas_call(
        functools.partial(_matmul_res_kernel, tn=tn),
        out_shape=jax.ShapeDtypeStruct((M, N), F32),
        grid=(M // tm,),
        in_specs=[pl.BlockSpec((tm, x.shape[1]), lambda i: (i, 0)) for x in xs]
        + [pl.BlockSpec(w.shape, lambda i: (0, 0), pipeline_mode=pl.Buffered(1)),
           pl.BlockSpec((tm, N), lambda i: (i, 0))],
        out_specs=pl.BlockSpec((tm, N), lambda i: (i, 0)),
        compiler_params=_cparams(("arbitrary",)),
        name=name,
    )(*xs, w, res)


HEAD_KINDS = ("row", "t", "t32", "raw")


def _head_proj_kernel(x_ref, g_ref, w_ref, gains_ref, cos_ref, sin_ref, *out_refs, groups, kinds):
    refs = dict(zip(kinds, out_refs))
    xn = _rms(x_ref[...], g_ref[...]).astype(BF16)
    col = 0
    for modes in groups:
        z = _dot(xn, w_ref[:, col:col + len(modes) * LANES])
        col += len(modes) * LANES
        for c, (kind, gain_row, rotary, dst) in enumerate(modes):
            if kind == "skip":
                continue
            y = z[:, c * LANES:(c + 1) * LANES]
            if gain_row is not None:
                y = _rms(y, gains_ref[gain_row:gain_row + 1, :])
            if rotary:
                y = _rope(y, cos_ref[...], sin_ref[...])
            span = slice(dst * LANES, (dst + 1) * LANES)
            if kind == "row":
                refs[kind][:, span] = y.astype(BF16)
            elif kind == "raw":
                refs[kind][:, span] = y
            elif kind == "t":
                refs[kind][span, :] = y.T.astype(BF16)
            else:
                refs[kind][span, :] = y.T


def head_projection(x, g, w, modes, gains, cos, sin, batch, seq, *, group=8, tm=512, name="head_projection"):
    M, K = x.shape
    n_blocks = w.shape[1] // LANES
    assert len(modes) == n_blocks
    tm = min(tm, seq)
    assert seq % tm == 0 and M % tm == 0 and tm % LANES == 0
    nt = seq // tm
    counts = {k: 0 for k in HEAD_KINDS}
    placed = []
    for kind, gain_row, rotary in modes:
        placed.append((kind, gain_row, rotary, counts.get(kind, 0)))
        if kind in counts:
            counts[kind] += 1
    groups = tuple(tuple(placed[i:i + group]) for i in range(0, n_blocks, group))
    kinds = tuple(k for k in HEAD_KINDS if counts[k])
    out_shape, out_specs = [], []
    for kind in kinds:
        width = counts[kind] * LANES
        dt = BF16 if kind in ("row", "t") else F32
        if kind in ("row", "raw"):
            out_shape.append(jax.ShapeDtypeStruct((M, width), dt))
            out_specs.append(pl.BlockSpec((tm, width), lambda i: (i, 0)))
        else:
            out_shape.append(jax.ShapeDtypeStruct((batch, width, seq), dt))
            out_specs.append(pl.BlockSpec((None, width, tm), lambda i: (i // nt, 0, i % nt)))
    kern = functools.partial(_head_proj_kernel, groups=groups, kinds=kinds)
    outs = pl.pallas_call(
        kern,
        out_shape=tuple(out_shape),
        grid=(M // tm,),
        in_specs=[pl.BlockSpec((tm, K), lambda i: (i, 0)),
                  pl.BlockSpec((1, K), lambda i: (0, 0)),
                  pl.BlockSpec(w.shape, lambda i: (0, 0), pipeline_mode=pl.Buffered(1)),
                  pl.BlockSpec(gains.shape, lambda i: (0, 0)),
                  pl.BlockSpec((tm, LANES), lambda i: (i % nt, 0)),
                  pl.BlockSpec((tm, LANES), lambda i: (i % nt, 0))],
        out_specs=tuple(out_specs),
        compiler_params=_cparams(("arbitrary",)),
        name=name,
    )(x, g.reshape(1, K), w, gains, cos, sin)
    return dict(zip(kinds, outs))


def rope_tables(pos):
    half = HEAD_DIM // 2
    inv = ROPE_THETA ** (-jnp.arange(half, dtype=F32) / half)
    ang = jnp.asarray(pos).astype(F32)[:, None] * inv[None, :]
    cos, sin = jnp.cos(ang), jnp.sin(ang)
    return jnp.concatenate([cos, cos], axis=-1), jnp.concatenate([-sin, sin], axis=-1)


def _nsa_compress_kernel(tk_ref, tv_ref, pek_ref, pev_ref, w1k_ref, w1v_ref, w2k_ref, w2v_ref,
                         g_ref, cos_ref, sin_ref, kc_ref, vc_ref, *, n_rows):
    half = NSA_CMP_LEN // 2

    def compress(t_ref, pe_ref, w1_ref, w2_ref):
        hid = w1_ref.shape[-1]
        lo = jnp.zeros((n_rows, hid), F32)
        hi = jnp.zeros((n_rows, hid), F32)
        for l in range(NSA_CMP_LEN):
            rows = t_ref[pl.ds(l % half, n_rows, stride=NSA_CMP_STRIDE), :]
            a = (rows + pe_ref[l:l + 1, :]).astype(BF16)
            part = _dot(a, w1_ref[l * HEAD_DIM:(l + 1) * HEAD_DIM, :])
            if l < half:
                lo = lo + part
            else:
                hi = hi + part
        pre = lo + pltpu.roll(hi, n_rows - 1, 0)
        return _dot(jax.nn.gelu(pre).astype(BF16), w2_ref[...])

    kc = compress(tk_ref, pek_ref, w1k_ref, w2k_ref)
    kc_ref[...] = _rope(_rms(kc, g_ref[...]), cos_ref[...], sin_ref[...]).astype(BF16)
    vc_ref[...] = compress(tv_ref, pev_ref, w1v_ref, w2v_ref).astype(BF16)


def nsa_compress(z, col0, batch, seq, pe, w1, w2, g_kc, cos_c, sin_c):
    n_rows = seq // NSA_CMP_STRIDE
    G = NSA_GROUPS
    kern = functools.partial(_nsa_compress_kernel, n_rows=n_rows)
    full = lambda a: pl.BlockSpec(a.shape, lambda b, g: (0,) * a.ndim)
    part = lambda a, kv: pl.BlockSpec((None,) + a.shape[1:], lambda b, g: (kv,) + (0,) * (a.ndim - 1))
    g_kc = g_kc.reshape(1, HEAD_DIM)
    return pl.pallas_call(
        kern,
        out_shape=(jax.ShapeDtypeStruct((batch, G, n_rows, HEAD_DIM), BF16),
                   jax.ShapeDtypeStruct((batch, G, n_rows, HEAD_DIM), BF16)),
        grid=(batch, G),
        in_specs=[pl.BlockSpec((seq, LANES), lambda b, g: (b, col0 + g)),
                  pl.BlockSpec((seq, LANES), lambda b, g: (b, col0 + G + g)),
                  part(pe, 0), part(pe, 1), part(w1, 0), part(w1, 1), part(w2, 0), part(w2, 1),
                  full(g_kc), full(cos_c), full(sin_c)],
        out_specs=(pl.BlockSpec((None, None, n_rows, HEAD_DIM), lambda b, g: (b, g, 0, 0)),
                   pl.BlockSpec((None, None, n_rows, HEAD_DIM), lambda b, g: (b, g, 0, 0))),
        compiler_params=_cparams(("arbitrary", "arbitrary")),
        name="nsa_compress",
    )(z, z, pe, pe, w1, w1, w2, w2, g_kc, cos_c, sin_c)


def _softmax_rows(s):
    m = jnp.max(s, axis=1, keepdims=True)
    e = jnp.exp(s - m)
    return e / jnp.maximum(jnp.sum(e, axis=1, keepdims=True), 1e-30)


def _flash_update_t(carry, s, vt):
    m, l, acc = carry
    m_new = jnp.maximum(m, jnp.max(s, axis=0, keepdims=True))
    alpha = jnp.exp2(m - m_new)
    p = jnp.exp2(s - m_new)
    l = alpha * l + jnp.sum(p, axis=0, keepdims=True)
    acc = alpha * acc + _dot(vt, p.astype(BF16))
    return m_new, l, acc


def _flash_init_t(tq, width):
    return (jnp.full((1, tq), NEG, F32), jnp.zeros((1, tq), F32), jnp.zeros((width, tq), F32))


def _causal_keep_t(tk, tq):
    return lax.broadcasted_iota(jnp.int32, (tk, tq), 0) <= lax.broadcasted_iota(jnp.int32, (tk, tq), 1)


Q_CHUNK = 512


def _flash_streams(carries_list, k_tiles, chunk_lists, vt, masks):
    scores = []
    for k_tile, q_chunks in zip(k_tiles, chunk_lists):
        row = []
        for qc, mk in zip(q_chunks, masks):
            if mk is False:
                row.append(None)
                continue
            s = _dot_nt(k_tile, qc)
            row.append(s if mk is None else jnp.where(mk, s, NEG))
        scores.append(row)
    return [tuple(c if s is None else _flash_update_t(c, s, vt) for c, s in zip(carries, row))
            for carries, row in zip(carries_list, scores)]


def _causal_tail(carries_list, q0, n_ch, load_kv, chunk_lists):
    keep = _causal_keep_t(Q_CHUNK, Q_CHUNK)
    for kb in range(n_ch):
        k0 = q0 + kb * Q_CHUNK
        k_tiles, vt = load_kv(k0 if isinstance(k0, int) else pl.multiple_of(k0, Q_CHUNK), Q_CHUNK)
        masks = [False if kb > c else (keep if kb == c else None) for c in range(n_ch)]
        carries_list = _flash_streams(carries_list, k_tiles, chunk_lists, vt, masks)
    return carries_list


def _nsa_attn_kernel(q_ref, kc_ref, vc_ref, ks_ref, kw_ref, vst_ref, vwt_ref, gt_ref,
                     ovt_ref, expt_ref, o_ref, m_ref, l_ref, acc_ref, *, tq, tk, seq):
    R = NSA_REP
    qi = pl.program_id(2)
    q0 = qi * tq
    n_cmp = kc_ref.shape[0]
    n_slc = seq // NSA_SLC_LEN
    q = q_ref[...]
    qs = jnp.concatenate([q[:, r * HEAD_DIM:(r + 1) * HEAD_DIM] for r in range(R)], axis=0)
    qpos1 = q0 + lax.broadcasted_iota(jnp.int32, (1, tq), 1)
    hp = Q_CHUNK // tq
    n_ch = R // hp
    q_chunks = [qs[c * Q_CHUNK:(c + 1) * Q_CHUNK] for c in range(n_ch)]
    qpos = jnp.concatenate([qpos1] * hp, axis=1)

    cmp_end = lax.broadcasted_iota(jnp.int32, (n_cmp, 1), 0) * NSA_CMP_STRIDE + (NSA_CMP_LEN - 1)
    vis = cmp_end <= qpos
    kc = kc_ref[...]
    vct = vc_ref[...].astype(F32).T.astype(BF16)
    p_cs, o_cs = [], []
    for qc in q_chunks:
        s_c = jnp.where(vis, _dot_nt(kc, qc), NEG)
        e_c = jnp.where(vis, jnp.exp2(s_c - jnp.max(s_c, axis=0, keepdims=True)), 0.0)
        p_c = e_c / jnp.maximum(jnp.sum(e_c, axis=0, keepdims=True), 1e-30)
        p_cs.append(p_c)
        o_cs.append(_dot(vct, p_c.astype(BF16)))

    p_sum = None
    for r in range(R):
        part = p_cs[r // hp][:, (r % hp) * tq:(r % hp + 1) * tq]
        p_sum = part if p_sum is None else p_sum + part
    p_hi = p_sum.astype(BF16)
    p_lo = (p_sum - p_hi.astype(F32)).astype(BF16)
    imp = _dot(ovt_ref[...], p_hi) + _dot(ovt_ref[...], p_lo)
    jblk = lax.broadcasted_iota(jnp.int32, (n_slc, tq), 0)
    qblk = qpos1 // NSA_SLC_LEN
    forced = (jblk == 0) | (jblk == qblk) | (jblk == qblk - 1)
    imp = jnp.where(jblk > qblk, -jnp.inf, jnp.where(forced, jnp.inf, imp))
    picked = jnp.zeros((n_slc, tq), jnp.int32)
    for _ in range(min(NSA_SLC_TOPK, n_slc)):
        cand = jnp.where(picked > 0, -jnp.inf, imp)
        best = jnp.max(cand, axis=0, keepdims=True)
        first = jnp.min(jnp.where(cand == best, jblk + picked * n_slc, n_slc), axis=0, keepdims=True)
        picked = jnp.where(jblk == first, 1, picked)
    sel_bias = jnp.where(picked > 0, 0.0, NEG).astype(BF16)

    def slc_tile(j, carries, causal):
        k0 = j * tk if isinstance(j, int) else pl.multiple_of(j * tk, tk)
        ks = ks_ref[pl.ds(k0, tk), :]
        vst = vst_ref[:, pl.ds(k0, tk)]
        bias = _dot(expt_ref[pl.ds(k0, tk), :], sel_bias)
        if causal:
            kpos = k0 + lax.broadcasted_iota(jnp.int32, (tk, 1), 0)
            bias = jnp.where(kpos <= qpos1, bias, NEG)
        bias = jnp.concatenate([bias] * hp, axis=1)
        scores = [_dot_nt(ks, qc) + bias for qc in q_chunks]
        return tuple(_flash_update_t(carry, s, vst) for carry, s in zip(carries, scores))

    last = q0 // tk
    n_tiles = seq // tk
    st_refs = (m_ref, l_ref, acc_ref)
    for c in range(n_ch):
        for ref, val in zip(st_refs, _flash_init_t(Q_CHUNK, HEAD_DIM)):
            ref[c] = val
    for count in range(1, n_tiles):
        @pl.when(last == count)
        def _():
            carries = tuple(_flash_init_t(Q_CHUNK, HEAD_DIM) for _ in range(n_ch))
            for j in range(count):
                carries = slc_tile(j, carries, False)
            for c in range(n_ch):
                for ref, val in zip(st_refs, carries[c]):
                    ref[c] = val
    carries = tuple(tuple(ref[c] for ref in st_refs) for c in range(n_ch))
    o_ss = [acc / jnp.maximum(l, 1e-30) for _, l, acc in slc_tile(last, carries, True)]

    span = NSA_WINDOW + tq
    w0 = pl.multiple_of(jnp.maximum(q0 - NSA_WINDOW, 0), tq)
    rel = qpos - (w0 + lax.broadcasted_iota(jnp.int32, (span, 1), 0))
    kw = kw_ref[pl.ds(w0, span), :]
    vwt = vwt_ref[:, pl.ds(w0, span)]
    o_ws = []
    for qc in q_chunks:
        s_w = jnp.where(rel >= 0, jnp.where(rel < NSA_WINDOW, _dot_nt(kw, qc), NEG), NEG)
        e_w = jnp.exp2(s_w - jnp.max(s_w, axis=0, keepdims=True))
        p_w = e_w / jnp.maximum(jnp.sum(e_w, axis=0, keepdims=True), 1e-30)
        o_ws.append(_dot(vwt, p_w.astype(BF16)))

    gates = jax.nn.sigmoid(gt_ref[...])
    outs = []
    for r in range(R):
        c, cols = r // hp, slice((r % hp) * tq, (r % hp + 1) * tq)
        o_r = (gates[3 * r:3 * r + 1] * o_cs[c][:, cols] + gates[3 * r + 1:3 * r + 2] * o_ss[c][:, cols]
               + gates[3 * r + 2:3 * r + 3] * o_ws[c][:, cols])
        outs.append(o_r.T)
    o_ref[...] = jnp.concatenate(outs, axis=1).astype(BF16)


NSA_GATE_ROWS = 16


def nsa_attention(zb, zt, gt, kc, vc, batch, seq, ks_col, kw_col, vs_row, vw_row):
    tq, tk = 2 * NSA_QBLOCK, 512
    G, R = NSA_GROUPS, NSA_REP
    tk = min(tk, seq)
    assert seq % tk == 0 and tk % tq == 0 and seq >= NSA_WINDOW + tq and 3 * R <= NSA_GATE_ROWS
    assert Q_CHUNK % tq == 0 and R % (Q_CHUNK // tq) == 0
    n_chains = R * tq // Q_CHUNK
    nq = seq // tq
    n_cmp = seq // NSA_CMP_STRIDE
    n_slc = seq // NSA_SLC_LEN
    starts = np.arange(n_cmp) * NSA_CMP_STRIDE
    sb = np.arange(n_slc) * NSA_SLC_LEN
    overlap = np.clip(np.minimum(starts[:, None] + NSA_CMP_LEN, sb[None, :] + NSA_SLC_LEN)
                      - np.maximum(starts[:, None], sb[None, :]), 0, None) / NSA_CMP_LEN
    expand_t = (np.arange(seq)[:, None] // NSA_SLC_LEN == np.arange(n_slc)[None, :]).astype(np.float32)
    kern = functools.partial(_nsa_attn_kernel, tq=tq, tk=tk, seq=seq)
    return pl.pallas_call(
        kern,
        out_shape=jax.ShapeDtypeStruct((batch * seq, G * R * HEAD_DIM), BF16),
        grid=(batch, G, nq),
        in_specs=[pl.BlockSpec((tq, R * HEAD_DIM), lambda b, g, i: (b * nq + i, g)),
                  pl.BlockSpec((None, None, n_cmp, HEAD_DIM), lambda b, g, i: (b, g, 0, 0)),
                  pl.BlockSpec((None, None, n_cmp, HEAD_DIM), lambda b, g, i: (b, g, 0, 0)),
                  pl.BlockSpec((seq, LANES), lambda b, g, i: (b, ks_col + g)),
                  pl.BlockSpec((seq, LANES), lambda b, g, i: (b, kw_col + g)),
                  pl.BlockSpec((None, HEAD_DIM, seq), lambda b, g, i: (b, vs_row + g, 0)),
                  pl.BlockSpec((None, HEAD_DIM, seq), lambda b, g, i: (b, vw_row + g, 0)),
                  pl.BlockSpec((None, NSA_GATE_ROWS, tq), lambda b, g, i: (b, g, i)),
                  pl.BlockSpec((n_slc, n_cmp), lambda b, g, i: (0, 0)),
                  pl.BlockSpec((seq, n_slc), lambda b, g, i: (0, 0))],
        out_specs=pl.BlockSpec((tq, R * HEAD_DIM), lambda b, g, i: (b * nq + i, g)),
        scratch_shapes=[pltpu.VMEM((n_chains, 1, Q_CHUNK), F32), pltpu.VMEM((n_chains, 1, Q_CHUNK), F32),
                        pltpu.VMEM((n_chains, HEAD_DIM, Q_CHUNK), F32)],
        compiler_params=_cparams(("arbitrary", "arbitrary", "arbitrary")),
        name="nsa_attention",
    )(zb, kc, vc, zb, zb, zt, zt, gt, jnp.asarray(overlap.T, BF16), jnp.asarray(expand_t, BF16))


def _diff_attn_kernel(q_ref, k_ref, vt_ref, lam_ref, subg_ref, o_ref, *, tq, tk, nq, lam_init):
    qi = pl.program_id(2)
    n_ch = tq // Q_CHUNK
    q = q_ref[...]
    q1 = [q[c * Q_CHUNK:(c + 1) * Q_CHUNK, :HEAD_DIM] for c in range(n_ch)]
    q2 = [q[c * Q_CHUNK:(c + 1) * Q_CHUNK, HEAD_DIM:] for c in range(n_ch)]

    def load_kv(k0, size):
        k = k_ref[pl.ds(k0, size), :]
        return [k[:, :HEAD_DIM], k[:, HEAD_DIM:]], vt_ref[:, pl.ds(k0, size)]

    width = vt_ref.shape[0]
    lp = lam_ref[...]
    lam = (jnp.exp(jnp.sum(lp[0:1] * lp[1:2], axis=1, keepdims=True))
           - jnp.exp(jnp.sum(lp[2:3] * lp[3:4], axis=1, keepdims=True)) + lam_init)

    for qv in range(nq):
        @pl.when(qi == qv)
        def _():
            carries = [tuple(_flash_init_t(Q_CHUNK, width) for _ in range(n_ch)) for _ in range(2)]
            for j in range(qv * (tq // tk)):
                k_tiles, vt = load_kv(j * tk, tk)
                carries = _flash_streams(carries, k_tiles, [q1, q2], vt, [None] * n_ch)
            c1, c2 = _causal_tail(carries, qv * tq, n_ch, load_kv, [q1, q2])
            o = jnp.concatenate([(a1 / jnp.maximum(l1, 1e-30) - lam * (a2 / jnp.maximum(l2, 1e-30))).T
                                 for (_, l1, a1), (_, l2, a2) in zip(c1, c2)], axis=0)
            o_ref[...] = (_rms(o, subg_ref[...]) * (1.0 - lam_init)).astype(BF16)


def diff_attention(zb, zt, batch, seq, q_col, k_col, vt_row, diff_lam, sub_g, layer, *, tq=1024, tk=512):
    H = DIFF_HEADS
    W2 = 2 * HEAD_DIM
    tq = min(tq, seq)
    tk = min(tk, tq)
    assert seq % tq == 0 and tq % tk == 0 and tq % Q_CHUNK == 0
    nq = seq // tq
    lam_init = 0.8 - 0.6 * math.exp(-0.3 * layer)
    kern = functools.partial(_diff_attn_kernel, tq=tq, tk=tk, nq=nq, lam_init=lam_init)
    return pl.pallas_call(
        kern,
        out_shape=jax.ShapeDtypeStruct((batch * seq, H * W2), BF16),
        grid=(batch, H, nq),
        in_specs=[pl.BlockSpec((tq, W2), lambda b, h, i: (b * nq + i, q_col + h)),
                  pl.BlockSpec((seq, W2), lambda b, h, i: (b, k_col + h)),
                  pl.BlockSpec((None, W2, seq), lambda b, h, i: (b, vt_row + h, 0)),
                  pl.BlockSpec((4, HEAD_DIM), lambda b, h, i: (0, 0)),
                  pl.BlockSpec((1, W2), lambda b, h, i: (0, 0))],
        out_specs=pl.BlockSpec((tq, W2), lambda b, h, i: (b * nq + i, h)),
        compiler_params=_cparams(("arbitrary", "arbitrary", "arbitrary")),
        name="diff_attention",
    )(zb, zb, zt, diff_lam, sub_g.reshape(1, W2))


FOX_AUG = 6


def _fox_attn_kernel(q_ref, qaug_ref, k_ref, kaug_ref, vt_ref, o_ref, *, tq, tk, nq):
    h = pl.program_id(1)
    qi = pl.program_id(2)
    n_ch = tq // Q_CHUNK
    lane = lax.broadcasted_iota(jnp.int32, (tq, LANES), 1)
    mine = (lane >= h * FOX_AUG) & (lane < (h + 1) * FOX_AUG)
    q2 = jnp.concatenate([q_ref[...], jnp.where(mine, qaug_ref[...], jnp.zeros_like(qaug_ref))], axis=1)
    q_chunks = [q2[c * Q_CHUNK:(c + 1) * Q_CHUNK] for c in range(n_ch)]

    def load_kv(k0, size):
        k2 = jnp.concatenate([k_ref[pl.ds(k0, size), :], kaug_ref[pl.ds(k0, size), :]], axis=1)
        return [k2], vt_ref[:, pl.ds(k0, size)]

    for qv in range(nq):
        @pl.when(qi == qv)
        def _():
            carries = tuple(_flash_init_t(Q_CHUNK, HEAD_DIM) for _ in range(n_ch))
            for j in range(qv * (tq // tk)):
                k_tiles, vt = load_kv(j * tk, tk)
                (carries,) = _flash_streams([carries], k_tiles, [q_chunks], vt, [None] * n_ch)
            (carries,) = _causal_tail([carries], qv * tq, n_ch, load_kv, [q_chunks])
            o_ref[...] = jnp.concatenate([(acc / jnp.maximum(l, 1e-30)).T for _, l, acc in carries],
                                         axis=0).astype(BF16)


def fox_attention(zb, zt, qaug, kaug, batch, seq, *, tq=2048, tk=512):
    H = FOX_HEADS
    assert H * FOX_AUG <= LANES
    tq = min(tq, seq)
    tk = min(tk, tq)
    assert seq % tq == 0 and tq % tk == 0 and tq % Q_CHUNK == 0
    nq = seq // tq
    kern = functools.partial(_fox_attn_kernel, tq=tq, tk=tk, nq=nq)
    return pl.pallas_call(
        kern,
        out_shape=jax.ShapeDtypeStruct((batch * seq, H * HEAD_DIM), BF16),
        grid=(batch, H, nq),
        in_specs=[pl.BlockSpec((tq, LANES), lambda b, h, i: (b * nq + i, h)),
                  pl.BlockSpec((tq, LANES), lambda b, h, i: (b * nq + i, 0)),
                  pl.BlockSpec((seq, LANES), lambda b, h, i: (b, H + h)),
                  pl.BlockSpec((seq, LANES), lambda b, h, i: (b, 0)),
                  pl.BlockSpec((None, HEAD_DIM, seq), lambda b, h, i: (b, h, 0))],
        out_specs=pl.BlockSpec((tq, LANES), lambda b, h, i: (b * nq + i, h)),
        compiler_params=_cparams(("arbitrary", "arbitrary", "arbitrary")),
        name="fox_attention",
    )(zb, qaug, zb, kaug, zt)


def _split3(x):
    p1 = x.astype(BF16)
    r1 = x - p1.astype(F32)
    p2 = r1.astype(BF16)
    p3 = (r1 - p2.astype(F32)).astype(BF16)
    return p1, p2, p3


def _fox_gate_kernel(z_ref, b_ref, place_ref, pat_ref, qaug_ref, kaug_ref, carry_ref, *, tr):
    @pl.when(pl.program_id(1) == 0)
    def _():
        carry_ref[...] = jnp.zeros_like(carry_ref)

    logf = jax.nn.log_sigmoid(z_ref[...] + b_ref[...])
    tri = (lax.broadcasted_iota(jnp.int32, (tr, tr), 1) <= lax.broadcasted_iota(jnp.int32, (tr, tr), 0)).astype(BF16)
    p1, p2, p3 = _split3(logf)
    c = _dot(tri, p1) + _dot(tri, p2) + _dot(tri, p3) + carry_ref[...]
    carry_ref[...] = c[tr - 1:tr, :]
    e1, e2, e3 = _split3(c * LOG2E)
    place = place_ref[...]
    cexp = _dot(e1, place) + _dot(e2, place) + _dot(e3, place)
    h1, h2, h3 = _split3(cexp)
    pat = pat_ref[...]
    piece = h1.astype(F32) * pat[0:1] + h2.astype(F32) * pat[1:2] + h3.astype(F32) * pat[2:3]
    kaug_ref[...] = (pat[4:5] - piece * pat[3:4]).astype(BF16)
    qaug_ref[...] = (pat[3:4] + piece * pat[4:5]).astype(BF16)


def fox_gate_bias(z, col_block, bias_row, batch, seq, *, tr=256):
    H = FOX_HEADS
    tr = min(tr, seq)
    nt = seq // tr
    lanes = np.arange(LANES)
    used = lanes < H * FOX_AUG
    place = (lanes[None, :] // FOX_AUG == np.arange(LANES)[:, None]) & used[None, :] & (np.arange(LANES)[:, None] < H)
    j = lanes % FOX_AUG
    pat = np.zeros((8, LANES), np.float32)
    for r in range(3):
        pat[r] = used & (j % 3 == r)
    pat[3] = used & (j < 3)
    pat[4] = used & (j >= 3)
    kern = functools.partial(_fox_gate_kernel, tr=tr)
    out = jax.ShapeDtypeStruct((batch * seq, LANES), BF16)
    return pl.pallas_call(
        kern,
        out_shape=(out, out),
        grid=(batch, nt),
        in_specs=[pl.BlockSpec((tr, LANES), lambda b, i: (b * nt + i, col_block)),
                  pl.BlockSpec((1, LANES), lambda b, i: (0, 0)),
                  pl.BlockSpec((LANES, LANES), lambda b, i: (0, 0)),
                  pl.BlockSpec((8, LANES), lambda b, i: (0, 0))],
        out_specs=(pl.BlockSpec((tr, LANES), lambda b, i: (b * nt + i, 0)),
                   pl.BlockSpec((tr, LANES), lambda b, i: (b * nt + i, 0))),
        scratch_shapes=[pltpu.VMEM((1, LANES), F32)],
        compiler_params=_cparams(("arbitrary", "arbitrary")),
        name="fox_gate_bias",
    )(z, bias_row, jnp.asarray(place, BF16), jnp.asarray(pat))


def _cross_attn_kernel(x_ref, g_ref, wq_ref, kv_ref, qg_ref, kg_ref, wo_ref, o_ref):
    x = x_ref[...]
    q = _dot(_rms(x, g_ref[...]).astype(BF16), wq_ref[...])
    kv = kv_ref[...]
    width = MEM_HEADS * HEAD_DIM
    outs = []
    for h in range(MEM_HEADS):
        cols = slice(h * HEAD_DIM, (h + 1) * HEAD_DIM)
        qh = _rms(q[:, cols], qg_ref[...]).astype(BF16)
        kh = _rms(kv[:, cols], kg_ref[...]).astype(BF16)
        vh = kv[:, width + h * HEAD_DIM: width + (h + 1) * HEAD_DIM].astype(BF16)
        p = _softmax_rows(_dot_nt(qh, kh))
        outs.append(_dot(p.astype(BF16), vh))
    o = jnp.concatenate(outs, axis=1).astype(BF16)
    o_ref[...] = x + _dot(o, wo_ref[...])


def cross_attention(x, g, wq, kv, q_gain_scaled, k_gain, wo, seq, mem_len, *, tm=512):
    M, D = x.shape
    tm = min(tm, seq)
    nt = seq // tm
    width = MEM_HEADS * HEAD_DIM
    return pl.pallas_call(
        _cross_attn_kernel,
        out_shape=jax.ShapeDtypeStruct((M, D), F32),
        grid=(M // tm,),
        in_specs=[pl.BlockSpec((tm, D), lambda i: (i, 0)),
                  pl.BlockSpec((1, D), lambda i: (0, 0)),
                  pl.BlockSpec((D, width), lambda i: (0, 0)),
                  pl.BlockSpec((mem_len, 2 * width), lambda i: (i // nt, 0)),
                  pl.BlockSpec((1, HEAD_DIM), lambda i: (0, 0)),
                  pl.BlockSpec((1, HEAD_DIM), lambda i: (0, 0)),
                  pl.BlockSpec((width, D), lambda i: (0, 0))],
        out_specs=pl.BlockSpec((tm, D), lambda i: (i, 0)),
        compiler_params=_cparams(("arbitrary",)),
        name="cross_attention",
    )(x, g.reshape(1, D), wq, kv, q_gain_scaled.reshape(1, HEAD_DIM), k_gain.reshape(1, HEAD_DIM), wo)


def _pack_bf16_pairs(x_bf16):
    n = x_bf16.shape[1] // 2
    lo = pltpu.bitcast(x_bf16[:, :n].astype(F32), jnp.uint32)
    hi = pltpu.bitcast(x_bf16[:, n:].astype(F32), jnp.uint32)
    return (hi & jnp.uint32(0xFFFF0000)) | (lo >> jnp.uint32(16))


def _unpack_bf16_pairs(words):
    lo = pltpu.bitcast(words << jnp.uint32(16), F32).astype(BF16)
    hi = pltpu.bitcast(words & jnp.uint32(0xFFFF0000), F32).astype(BF16)
    return jnp.concatenate([lo, hi], axis=1)


def _moe_router_kernel(x_ref, g_ref, wr_ref, br_ref, hn_ref, route_ref, onehot_ref):
    hn = _rms(x_ref[...], g_ref[...])
    h_hi = hn.astype(BF16)
    hn_ref[...] = _pack_bf16_pairs(h_hi)
    w = wr_ref[...]
    w_hi = w.astype(BF16)
    w_lo = (w - w_hi.astype(F32)).astype(BF16)
    h_lo = (hn - h_hi.astype(F32)).astype(BF16)
    logits = _dot(h_hi, w_hi) + _dot(h_lo, w_hi) + _dot(h_hi, w_lo) + br_ref[...]
    tm = logits.shape[0]
    lane = lax.broadcasted_iota(jnp.int32, (tm, LANES), 1)
    is_grp = lane < N_GROUPS
    lg = jnp.where(is_grp, logits, -jnp.inf)
    eg = jnp.exp(lg - jnp.max(lg, axis=1, keepdims=True))
    p_grp = eg / jnp.sum(eg, axis=1, keepdims=True)
    p_top = jnp.max(p_grp, axis=1, keepdims=True)
    grp = jnp.min(jnp.where(is_grp & (p_grp == p_top), lane, LANES), axis=1, keepdims=True)
    lo = N_GROUPS + grp * EXPERTS_PER_GROUP
    in_grp = (lane >= lo) & (lane < lo + EXPERTS_PER_GROUP)
    le = jnp.where(in_grp, logits, -jnp.inf)
    ee = jnp.exp(le - jnp.max(le, axis=1, keepdims=True))
    p_in = ee / jnp.sum(ee, axis=1, keepdims=True)
    m1 = jnp.max(p_in, axis=1, keepdims=True)
    i1 = jnp.min(jnp.where(in_grp & (p_in == m1), lane, LANES), axis=1, keepdims=True)
    rest = jnp.where(in_grp & (lane != i1), p_in, -jnp.inf)
    m2 = jnp.max(rest, axis=1, keepdims=True)
    i2 = jnp.min(jnp.where(rest == m2, lane, LANES), axis=1, keepdims=True)
    denom = m1 + m2
    w1 = p_top * m1 / denom
    w2 = p_top * m2 / denom
    e1 = (i1 - N_GROUPS).astype(F32)
    e2 = (i2 - N_GROUPS).astype(F32)
    route_ref[...] = jnp.where(lane == 0, e1, jnp.where(lane == 1, e2, jnp.where(lane == 2, w1, jnp.where(lane == 3, w2, 0.0))))
    onehot_ref[...] = ((lane == i1 - N_GROUPS) | (lane == i2 - N_GROUPS)).astype(BF16)


def moe_router(x, g, w_router, b_router, *, tm=512):
    M, D = x.shape
    tm = min(tm, M)
    return pl.pallas_call(
        _moe_router_kernel,
        out_shape=(jax.ShapeDtypeStruct((M, D // 2), jnp.uint32),
                   jax.ShapeDtypeStruct((M, LANES), F32),
                   jax.ShapeDtypeStruct((M, LANES), BF16)),
        grid=(M // tm,),
        in_specs=[pl.BlockSpec((tm, D), lambda i: (i, 0)),
                  pl.BlockSpec((1, D), lambda i: (0, 0)),
                  pl.BlockSpec((D, LANES), lambda i: (0, 0)),
                  pl.BlockSpec((1, LANES), lambda i: (0, 0))],
        out_specs=(pl.BlockSpec((tm, D // 2), lambda i: (i, 0)),
                   pl.BlockSpec((tm, LANES), lambda i: (i, 0)),
                   pl.BlockSpec((tm, LANES), lambda i: (i, 0))),
        compiler_params=_cparams(("arbitrary",)),
        name="moe_router",
    )(x, g.reshape(1, D), w_router, b_router)


def _moe_rank_kernel(onehot_ref, rank_ref, count_ref, carry_ref, *, tr):
    i = pl.program_id(0)

    @pl.when(i == 0)
    def _():
        carry_ref[...] = jnp.zeros_like(carry_ref)

    oh = onehot_ref[...]
    tri = (lax.broadcasted_iota(jnp.int32, (tr, tr), 1) < lax.broadcasted_iota(jnp.int32, (tr, tr), 0)).astype(BF16)
    rank_ref[...] = _dot(tri, oh) + carry_ref[...]
    carry_ref[...] = carry_ref[...] + jnp.sum(oh.astype(F32), axis=0, keepdims=True)
    count_ref[...] = carry_ref[...]


def moe_rank(onehot, *, tr=512):
    M = onehot.shape[0]
    tr = min(tr, M)
    kern = functools.partial(_moe_rank_kernel, tr=tr)
    return pl.pallas_call(
        kern,
        out_shape=(jax.ShapeDtypeStruct((M, LANES), F32), jax.ShapeDtypeStruct((1, LANES), F32)),
        grid=(M // tr,),
        in_specs=[pl.BlockSpec((tr, LANES), lambda i: (i, 0))],
        out_specs=(pl.BlockSpec((tr, LANES), lambda i: (i, 0)), pl.BlockSpec((1, LANES), lambda i: (0, 0))),
        scratch_shapes=[pltpu.VMEM((1, LANES), F32)],
        compiler_params=_cparams(("arbitrary",)),
        name="moe_rank",
    )(onehot)


def _moe_slot_kernel(route_ref, rank_ref, pstart_ref, o_ref):
    route = route_ref[...]
    slot = rank_ref[...] + pstart_ref[...]
    lane = lax.broadcasted_iota(jnp.int32, route.shape, 1)
    e1 = route[:, 0:1].astype(jnp.int32)
    e2 = route[:, 1:2].astype(jnp.int32)
    d1 = jnp.sum(jnp.where(lane == e1, slot, 0.0), axis=1, keepdims=True)
    d2 = jnp.sum(jnp.where(lane == e2, slot, 0.0), axis=1, keepdims=True)
    o_ref[...] = jnp.where(lane == 0, d1, jnp.where(lane == 1, d2, 0.0)).astype(jnp.int32)


def moe_slots(route, rank, pstart_row, *, tm=1024):
    M = route.shape[0]
    tm = min(tm, M)
    return pl.pallas_call(
        _moe_slot_kernel,
        out_shape=jax.ShapeDtypeStruct((M, LANES), jnp.int32),
        grid=(M // tm,),
        in_specs=[pl.BlockSpec((tm, LANES), lambda i: (i, 0)),
                  pl.BlockSpec((tm, LANES), lambda i: (i, 0)),
                  pl.BlockSpec((1, LANES), lambda i: (0, 0))],
        out_specs=pl.BlockSpec((tm, LANES), lambda i: (i, 0)),
        compiler_params=_cparams(("arbitrary",)),
        name="moe_slots",
    )(route, rank, pstart_row)


def _moe_dispatch_kernel(lastblk_ref, d1_hbm, d2_hbm, hn_ref, xb_hbm, zero_ref, s1_ref, s2_ref,
                         idx_sem, row_sem, zero_sem, *, tm, n_blocks):
    i = pl.program_id(0)

    def zero_copy(blk):
        return pltpu.make_async_copy(zero_ref, xb_hbm.at[pl.ds(blk * MOE_ROWS, MOE_ROWS)], zero_sem)

    @pl.when(i == 0)
    def _():
        zero_ref[...] = jnp.zeros_like(zero_ref)
        n_used = lastblk_ref[N_EXPERTS]
        for e in range(N_EXPERTS):
            @pl.when(lastblk_ref[e] >= 0)
            def _():
                zero_copy(lastblk_ref[e]).start()

        def start_tail(blk, _):
            zero_copy(blk).start()
            return 0

        def wait_tail(blk, _):
            zero_copy(blk).wait()
            return 0

        lax.fori_loop(n_used, n_blocks, start_tail, 0)
        for e in range(N_EXPERTS):
            @pl.when(lastblk_ref[e] >= 0)
            def _():
                zero_copy(lastblk_ref[e]).wait()
        lax.fori_loop(n_used, n_blocks, wait_tail, 0)

    c1 = pltpu.make_async_copy(d1_hbm.at[i], s1_ref, idx_sem.at[0])
    c2 = pltpu.make_async_copy(d2_hbm.at[i], s2_ref, idx_sem.at[1])
    c1.start()
    c2.start()
    c1.wait()
    c2.wait()

    def row_copy(r, slot):
        return pltpu.make_async_copy(hn_ref.at[pl.ds(r, 1)], xb_hbm.at[pl.ds(slot, 1)], row_sem)

    def issue(blk, _):
        for u in range(ROW_DMA_UNROLL):
            r = blk * ROW_DMA_UNROLL + u
            row_copy(r, s1_ref[r]).start(priority=0)
            row_copy(r, s2_ref[r]).start(priority=1)
        return 0

    lax.fori_loop(0, tm // ROW_DMA_UNROLL, issue, 0)

    def drain(r, _):
        row_copy(r, 0).wait()
        row_copy(r, 0).wait()
        return 0

    lax.fori_loop(0, tm, drain, 0, unroll=8)


def moe_dispatch(hn, d1, d2, lastblk, n_slots, *, tm=512):
    M, D = hn.shape
    assert hn.dtype.itemsize == 4
    tm = min(tm, M)
    assert n_slots % MOE_ROWS == 0
    kern = functools.partial(_moe_dispatch_kernel, tm=tm, n_blocks=n_slots // MOE_ROWS)
    grid_spec = pltpu.PrefetchScalarGridSpec(
        num_scalar_prefetch=1,
        grid=(M // tm,),
        in_specs=[pl.BlockSpec(memory_space=pl.ANY), pl.BlockSpec(memory_space=pl.ANY),
                  pl.BlockSpec((tm, D), lambda i, *_: (i, 0))],
        out_specs=pl.BlockSpec(memory_space=pl.ANY),
        scratch_shapes=[pltpu.VMEM((MOE_ROWS, D), hn.dtype),
                        pltpu.SMEM((tm,), jnp.int32), pltpu.SMEM((tm,), jnp.int32),
                        pltpu.SemaphoreType.DMA((2,)), pltpu.SemaphoreType.DMA, pltpu.SemaphoreType.DMA],
    )
    return pl.pallas_call(
        kern,
        out_shape=jax.ShapeDtypeStruct((n_slots, D), hn.dtype),
        grid_spec=grid_spec,
        compiler_params=_cparams(("arbitrary",)),
        name="moe_dispatch",
    )(lastblk, d1.reshape(M // tm, tm), d2.reshape(M // tm, tm), hn)


def _moe_expert_kernel(blk_e_ref, n_used_ref, x_ref, w1_ref, w3_ref, w2_ref, o_ref, w1b, w3b, w2b):
    i = pl.program_id(0)

    @pl.when(i < n_used_ref[0])
    def _():
        prev = blk_e_ref[jnp.maximum(i - 1, 0)]

        @pl.when((i == 0) | (blk_e_ref[i] != prev))
        def _():
            w1b[...] = w1_ref[...].astype(BF16)
            w3b[...] = w3_ref[...].astype(BF16)
            w2b[...] = w2_ref[...].astype(BF16)

        x = _unpack_bf16_pairs(x_ref[...])
        h = (jax.nn.silu(_dot(x, w1b[...])) * _dot(x, w3b[...])).astype(BF16)
        o_ref[...] = _dot(h, w2b[...])

    @pl.when(i >= n_used_ref[0])
    def _():
        o_ref[...] = jnp.zeros_like(o_ref)


def moe_experts(xb, blk_e, n_used, w1, w3, w2, layer):
    P = xb.shape[0]
    D = w1.shape[-2]
    FF = w1.shape[-1]
    n_blk = P // MOE_ROWS

    def row_map(i, blk_e_ref, n_used_ref):
        return (jnp.minimum(i, n_used_ref[0] - 1), 0)

    def out_map(i, blk_e_ref, n_used_ref):
        return (i, 0)

    def w_map(i, blk_e_ref, n_used_ref):
        return (layer, blk_e_ref[i], 0, 0)

    grid_spec = pltpu.PrefetchScalarGridSpec(
        num_scalar_prefetch=2,
        grid=(n_blk,),
        in_specs=[pl.BlockSpec((MOE_ROWS, D // 2), row_map),
                  pl.BlockSpec((None, None, D, FF), w_map),
                  pl.BlockSpec((None, None, D, FF), w_map),
                  pl.BlockSpec((None, None, FF, D), w_map)],
        out_specs=pl.BlockSpec((MOE_ROWS, D), out_map),
        scratch_shapes=[pltpu.VMEM((D, FF), BF16), pltpu.VMEM((D, FF), BF16), pltpu.VMEM((FF, D), BF16)],
    )
    return pl.pallas_call(
        _moe_expert_kernel,
        out_shape=jax.ShapeDtypeStruct((P, D), F32),
        grid_spec=grid_spec,
        compiler_params=_cparams(("arbitrary",)),
        name="moe_experts",
    )(blk_e, n_used, xb, w1, w3, w2)


def _moe_combine_kernel(x_ref, route_ref, d1_hbm, d2_hbm, yb_hbm, o_ref, y1_ref, y2_ref, s1_ref, s2_ref,
                        idx_sem, row_sem, *, tm):
    i = pl.program_id(0)
    c1 = pltpu.make_async_copy(d1_hbm.at[i], s1_ref, idx_sem.at[0])
    c2 = pltpu.make_async_copy(d2_hbm.at[i], s2_ref, idx_sem.at[1])
    c1.start()
    c2.start()
    c1.wait()
    c2.wait()

    def row_copy(slot, dst, r):
        return pltpu.make_async_copy(yb_hbm.at[pl.ds(slot, 1)], dst.at[pl.ds(r, 1)], row_sem)

    def issue(blk, _):
        for u in range(ROW_DMA_UNROLL):
            r = blk * ROW_DMA_UNROLL + u
            row_copy(s1_ref[r], y1_ref, r).start(priority=0)
            row_copy(s2_ref[r], y2_ref, r).start(priority=1)
        return 0

    lax.fori_loop(0, tm // ROW_DMA_UNROLL, issue, 0)

    def drain(r, _):
        row_copy(0, y1_ref, r).wait()
        row_copy(0, y2_ref, r).wait()
        return 0

    lax.fori_loop(0, tm, drain, 0, unroll=8)
    route = route_ref[...]
    o_ref[...] = x_ref[...] + (y1_ref[...] * route[:, 2:3] + y2_ref[...] * route[:, 3:4])


def moe_combine(x, route, d1, d2, yb, *, tm=256):
    M, D = x.shape
    tm = min(tm, M)
    kern = functools.partial(_moe_combine_kernel, tm=tm)
    return pl.pallas_call(
        kern,
        out_shape=jax.ShapeDtypeStruct((M, D), F32),
        grid=(M // tm,),
        in_specs=[pl.BlockSpec((tm, D), lambda i: (i, 0)),
                  pl.BlockSpec((tm, LANES), lambda i: (i, 0)),
                  pl.BlockSpec(memory_space=pl.ANY),
                  pl.BlockSpec(memory_space=pl.ANY),
                  pl.BlockSpec(memory_space=pl.ANY)],
        out_specs=pl.BlockSpec((tm, D), lambda i: (i, 0)),
        scratch_shapes=[pltpu.VMEM((tm, D), F32), pltpu.VMEM((tm, D), F32),
                        pltpu.SMEM((tm,), jnp.int32), pltpu.SMEM((tm,), jnp.int32),
                        pltpu.SemaphoreType.DMA((2,)), pltpu.SemaphoreType.DMA],
        compiler_params=_cparams(("arbitrary",)),
        name="moe_combine",
    )(x, route, d1.reshape(M // tm, tm), d2.reshape(M // tm, tm), yb)


def hier_moe(x, g, wg, bg, we, be, w1, w3, w2, layer):
    M, D = x.shape
    n_route = N_GROUPS + N_EXPERTS
    w_router = jnp.pad(jnp.concatenate([wg, we], axis=1), ((0, 0), (0, LANES - n_route)))
    b_router = jnp.pad(jnp.concatenate([bg, be]), (0, LANES - n_route)).reshape(1, LANES)
    hn, route, onehot = moe_router(x, g, w_router, b_router)
    rank, counts = moe_rank(onehot)
    cnt = counts[0, :N_EXPERTS].astype(jnp.int32)
    nblk = (cnt + MOE_ROWS - 1) // MOE_ROWS
    bend = jnp.cumsum(nblk)
    bstart = bend - nblk
    n_blk_max = (2 * M) // MOE_ROWS + N_EXPERTS
    pstart_row = jnp.pad((bstart * MOE_ROWS).astype(F32), (0, LANES - N_EXPERTS)).reshape(1, LANES)
    slots = moe_slots(route, rank, pstart_row)
    d1, d2 = slots[:, 0], slots[:, 1]
    n_used = bend[-1:].astype(jnp.int32)
    lastblk = jnp.concatenate([jnp.where(nblk > 0, bend - 1, -1).astype(jnp.int32), n_used])
    blk_ids = jnp.arange(n_blk_max, dtype=jnp.int32)
    blk_e = jnp.minimum(jnp.sum(bend[None, :] <= blk_ids[:, None], axis=1), N_EXPERTS - 1).astype(jnp.int32)
    xb = moe_dispatch(hn, d1, d2, lastblk, n_blk_max * MOE_ROWS)
    yb = moe_experts(xb, blk_e, n_used, w1, w3, w2, layer)
    return moe_combine(x, route, d1, d2, yb)


def _even_mixer(x, g, w_in, w_out, nsa_qk_g, cmp_pe, cmp_w1, cmp_w2, diff_qk_g, diff_lam, diff_sub_g,
                layer, batch, seq, cos, sin):
    M, D = x.shape
    scale = HEAD_DIM ** -0.5
    nq_w = NSA_HEADS * HEAD_DIM
    kv_w = 6 * NSA_GROUPS * HEAD_DIM
    gate_w = 3 * NSA_HEADS
    dq_w = 2 * DIFF_HEADS * HEAD_DIM
    dv_w = DIFF_HEADS * 2 * HEAD_DIM
    c_gate = nq_w + kv_w
    c_diff = c_gate + gate_w
    main_w = nq_w + kv_w + 2 * dq_w + dv_w
    G, R3 = NSA_GROUPS, 3 * NSA_REP
    gate_cols = []
    for grp in range(G):
        gate_cols += [w_in[:, c_gate + grp * R3:c_gate + (grp + 1) * R3], jnp.zeros((D, NSA_GATE_ROWS - R3), w_in.dtype)]
    gate_cols.append(jnp.zeros((D, LANES - G * NSA_GATE_ROWS), w_in.dtype))
    w_cat = jnp.concatenate([w_in[:, :c_gate], w_in[:, c_diff:]] + gate_cols, axis=1).astype(BF16)
    b_kv, b_dq, b_dk, b_dv, b_gate = nq_w // LANES, (nq_w + kv_w) // LANES, (nq_w + kv_w + dq_w) // LANES, \
        (nq_w + kv_w + 2 * dq_w) // LANES, main_w // LANES
    qs = scale * LOG2E
    gains = jnp.stack([nsa_qk_g[0] * qs, nsa_qk_g[2], nsa_qk_g[3], diff_qk_g[0] * qs, diff_qk_g[1]])
    modes = ([("row", 0, True)] * (b_kv - 0)
             + [("raw", None, False)] * (2 * G)
             + [("row", 1, True)] * G + [("t", None, False)] * G
             + [("row", 2, True)] * G + [("t", None, False)] * G
             + [("row", 3, True)] * (b_dk - b_dq) + [("row", 4, True)] * (b_dv - b_dk)
             + [("t", None, False)] * (b_gate - b_dv)
             + [("t32", None, False)])
    proj = head_projection(x, g, w_cat, modes, gains, cos, sin, batch, seq, name="even_in_proj")
    zb, zt, gt, zc = proj["row"], proj["t"], proj["t32"], proj["raw"]
    r_ks = b_kv
    r_kw, r_dq = r_ks + G, r_ks + 2 * G
    r_dk = r_dq + (b_dk - b_dq)
    cmp_end = np.arange(seq // NSA_CMP_STRIDE) * NSA_CMP_STRIDE + NSA_CMP_LEN - 1
    cos_c, sin_c = rope_tables(cmp_end)
    kc, vc = nsa_compress(zc, 0, batch, seq, cmp_pe, cmp_w1.astype(BF16), cmp_w2.astype(BF16), nsa_qk_g[1],
                          cos_c, sin_c)
    o_nsa = nsa_attention(zb, zt, gt, kc, vc, batch, seq, r_ks, r_kw, 0, G)
    o_diff = diff_attention(zb, zt, batch, seq, r_dq // 2, r_dk // 2, (2 * G) // 2, diff_lam, diff_sub_g, layer)
    return matmul_residual([o_nsa, o_diff], w_out.astype(BF16), x, name="even_out_proj")


def _odd_mixer(x, g, w_in, w_out, f_b, qk_g, batch, seq, cos, sin):
    M, D = x.shape
    H = FOX_HEADS
    scale = HEAD_DIM ** -0.5
    width = H * HEAD_DIM
    w_cat = jnp.pad(w_in, ((0, 0), (0, LANES - H))).astype(BF16)
    modes = [("row", 0, False)] * H + [("row", 1, False)] * H + [("t", None, False)] * H + [("raw", None, False)]
    gains = jnp.stack([qk_g[0] * (scale * LOG2E), qk_g[1]])
    proj = head_projection(x, g, w_cat, modes, gains, cos, sin, batch, seq, name="odd_in_proj")
    bias_row = jnp.pad(f_b, (0, LANES - H)).reshape(1, LANES)
    qaug, kaug = fox_gate_bias(proj["raw"], 0, bias_row, batch, seq)
    o = fox_attention(proj["row"], proj["t"], qaug, kaug, batch, seq)
    return matmul_residual([o], w_out.astype(BF16), x, name="odd_out_proj")


def kernel(x, mem, norm_g, ev_w_in, ev_w_out, nsa_qk_g, nsa_cmp_pe, nsa_cmp_w1, nsa_cmp_w2, diff_qk_g, diff_lam, diff_sub_g, od_w_in, od_w_out, fox_f_b, fox_qk_g, ca_wq, ca_wkv, ca_qk_g, ca_wo, moe_wg, moe_bg, moe_we, moe_be, moe_w1, moe_w3, moe_w2):
    B, T, D = x.shape
    mem_len = mem.shape[1]
    depth = norm_g.shape[0]
    scale = HEAD_DIM ** -0.5
    cos, sin = rope_tables(np.arange(T))
    xt = x.reshape(B * T, D)
    mt = mem.reshape(B * mem_len, D)
    for layer in range(depth):
        i = layer // 2
        if layer % 2 == 0:
            xt = _even_mixer(xt, norm_g[layer, 0], ev_w_in[i], ev_w_out[i], nsa_qk_g[i], nsa_cmp_pe[i],
                             nsa_cmp_w1[i], nsa_cmp_w2[i], diff_qk_g[i], diff_lam[i], diff_sub_g[i], layer,
                             B, T, cos, sin)
        else:
            xt = _odd_mixer(xt, norm_g[layer, 0], od_w_in[i], od_w_out[i], fox_f_b[i], fox_qk_g[i], B, T, cos, sin)
        kv = rms_matmul(mt, norm_g[layer, 2], ca_wkv[layer].astype(BF16), name="mem_kv_proj")
        xt = cross_attention(xt, norm_g[layer, 1], ca_wq[layer].astype(BF16), kv, ca_qk_g[layer, 0] * scale,
                             ca_qk_g[layer, 1], ca_wo[layer].astype(BF16), T, mem_len)
        xt = hier_moe(xt, norm_g[layer, 3], moe_wg[layer], moe_bg[layer], moe_we[layer], moe_be[layer],
                      moe_w1, moe_w3, moe_w2, layer)
    return xt.reshape(B, T, D)
```

```python
import functools
import math

import numpy as np
import jax
import jax.numpy as jnp
from jax import lax
from jax.experimental import pallas as pl
from jax.experimental.pallas import tpu as pltpu

F32 = jnp.float32
BF16 = jnp.bfloat16

HEAD_DIM = 128
ROPE_THETA = 10000.0
EPS = 1e-6
NEG = -1e30
LOG2E = math.log2(math.e)

NSA_HEADS = 8
NSA_GROUPS = 2
NSA_REP = NSA_HEADS // NSA_GROUPS
NSA_CMP_LEN = 32
NSA_CMP_STRIDE = 16
NSA_CMP_HIDDEN = 256
NSA_SLC_LEN = 64
NSA_SLC_TOPK = 8
NSA_WINDOW = 512
NSA_QBLOCK = 128
DIFF_HEADS = 4
FOX_HEADS = 16
MEM_HEADS = 4
N_GROUPS = 4
EXPERTS_PER_GROUP = 8
N_EXPERTS = N_GROUPS * EXPERTS_PER_GROUP
MOE_ROWS = 256
ROW_DMA_UNROLL = 8
LANES = 128
VMEM_LIMIT = 56 * 1024 * 1024


def _cparams(sem):
    return pltpu.CompilerParams(dimension_semantics=sem, vmem_limit_bytes=VMEM_LIMIT)


def _dot(a, b):
    return jnp.dot(a, b, preferred_element_type=F32)


def _dot_nt(a, b):
    return lax.dot_general(a, b, (((1,), (1,)), ((), ())), preferred_element_type=F32)


def _rms(x, g):
    ms = jnp.mean(x * x, axis=-1, keepdims=True)
    return x * lax.rsqrt(ms + EPS) * g


def _rope(y, cos, sin_signed):
    return y * cos + pltpu.roll(y, HEAD_DIM // 2, 1) * sin_signed


def _rms_matmul_kernel(x_ref, g_ref, w_ref, o_ref, xn_ref):
    @pl.when(pl.program_id(1) == 0)
    def _():
        xn_ref[...] = _rms(x_ref[...], g_ref[...]).astype(BF16)

    o_ref[...] = _dot(xn_ref[...], w_ref[...])


def rms_matmul(x, g, w, *, tm=1024, tn=512, name="rms_matmul"):
    M, K = x.shape
    N = w.shape[1]
    tm, tn = min(tm, M), min(tn, N)
    assert M % tm == 0 and N % tn == 0
    return pl.pallas_call(
        _rms_matmul_kernel,
        out_shape=jax.ShapeDtypeStruct((M, N), F32),
        grid=(M // tm, N // tn),
        in_specs=[pl.BlockSpec((tm, K), lambda i, j: (i, 0)),
                  pl.BlockSpec((1, K), lambda i, j: (0, 0)),
                  pl.BlockSpec((K, tn), lambda i, j: (0, j))],
        out_specs=pl.BlockSpec((tm, tn), lambda i, j: (i, j)),
        scratch_shapes=[pltpu.VMEM((tm, K), BF16)],
        compiler_params=_cparams(("arbitrary", "arbitrary")),
        name=name,
    )(x, g.reshape(1, K), w)


def _matmul_res_kernel(*refs, tn):
    *x_refs, w_ref, r_ref, o_ref = refs
    for n0 in range(0, o_ref.shape[1], tn):
        acc = r_ref[:, n0:n0 + tn]
        k0 = 0
        for x_ref in x_refs:
            k = x_ref.shape[1]
            acc = acc + _dot(x_ref[...], w_ref[k0:k0 + k, n0:n0 + tn])
            k0 += k
        o_ref[:, n0:n0 + tn] = acc


def matmul_residual(xs, w, res, *, tm=512, tn=512, name="matmul_residual"):
    M = xs[0].shape[0]
    N = w.shape[1]
    tm, tn = min(tm, M), min(tn, N)
    assert M % tm == 0 and N % tn == 0 and sum(x.shape[1] for x in xs) == w.shape[0]
    return pl.pallas_call(
        functools.partial(_matmul_res_kernel, tn=tn),
        out_shape=jax.ShapeDtypeStruct((M, N), F32),
        grid=(M // tm,),
        in_specs=[pl.BlockSpec((tm, x.shape[1]), lambda i: (i, 0)) for x in xs]
        + [pl.BlockSpec(w.shape, lambda i: (0, 0), pipeline_mode=pl.Buffered(1)),
           pl.BlockSpec((tm, N), lambda i: (i, 0))],
        out_specs=pl.BlockSpec((tm, N), lambda i: (i, 0)),
        compiler_params=_cparams(("arbitrary",)),
        name=name,
    )(*xs, w, res)


HEAD_KINDS = ("row", "t", "t32", "raw")


def _head_proj_kernel(x_ref, g_ref, w_ref, gains_ref, cos_ref, sin_ref, *out_refs, groups, kinds):
    refs = dict(zip(kinds, out_refs))
    xn = _rms(x_ref[...], g_ref[...]).astype(BF16)
    col = 0
    for modes in groups:
        z = _dot(xn, w_ref[:, col:col + len(modes) * LANES])
        col += len(modes) * LANES
        for c, (kind, gain_row, rotary, dst) in enumerate(modes):
            if kind == "skip":
                continue
            y = z[:, c * LANES:(c + 1) * LANES]
            if gain_row is not None:
                y = _rms(y, gains_ref[gain_row:gain_row + 1, :])
            if rotary:
                y = _rope(y, cos_ref[...], sin_ref[...])
            span = slice(dst * LANES, (dst + 1) * LANES)
            if kind == "row":
                refs[kind][:, span] = y.astype(BF16)
            elif kind == "raw":
                refs[kind][:, span] = y
            elif kind == "t":
                refs[kind][span, :] = y.T.astype(BF16)
            else:
                refs[kind][span, :] = y.T


def head_projection(x, g, w, modes, gains, cos, sin, batch, seq, *, group=8, tm=512, name="head_projection"):
    M, K = x.shape
    n_blocks = w.shape[1] // LANES
    assert len(modes) == n_blocks
    tm = min(tm, seq)
    assert seq % tm == 0 and M % tm == 0 and tm % LANES == 0
    nt = seq // tm
    counts = {k: 0 for k in HEAD_KINDS}
    placed = []
    for kind, gain_row, rotary in modes:
        placed.append((kind, gain_row, rotary, counts.get(kind, 0)))
        if kind in counts:
            counts[kind] += 1
    groups = tuple(tuple(placed[i:i + group]) for i in range(0, n_blocks, group))
    kinds = tuple(k for k in HEAD_KINDS if counts[k])
    out_shape, out_specs = [], []
    for kind in kinds:
        width = counts[kind] * LANES
        dt = BF16 if kind in ("row", "t") else F32
        if kind in ("row", "raw"):
            out_shape.append(jax.ShapeDtypeStruct((M, width), dt))
            out_specs.append(pl.BlockSpec((tm, width), lambda i: (i, 0)))
        else:
            out_shape.append(jax.ShapeDtypeStruct((batch, width, seq), dt))
            out_specs.append(pl.BlockSpec((None, width, tm), lambda i: (i // nt, 0, i % nt)))
    kern = functools.partial(_head_proj_kernel, groups=groups, kinds=kinds)
    outs = pl.pallas_call(
        kern,
        out_shape=tuple(out_shape),
        grid=(M // tm,),
        in_specs=[pl.BlockSpec((tm, K), lambda i: (i, 0)),
                  pl.BlockSpec((1, K), lambda i: (0, 0)),
                  pl.BlockSpec(w.shape, lambda i: (0, 0), pipeline_mode=pl.Buffered(1)),
                  pl.BlockSpec(gains.shape, lambda i: (0, 0)),
                  pl.BlockSpec((tm, LANES), lambda i: (i % nt, 0)),
                  pl.BlockSpec((tm, LANES), lambda i: (i % nt, 0))],
        out_specs=tuple(out_specs),
        compiler_params=_cparams(("arbitrary",)),
        name=name,
    )(x, g.reshape(1, K), w, gains, cos, sin)
    return dict(zip(kinds, outs))


def rope_tables(pos):
    half = HEAD_DIM // 2
    inv = ROPE_THETA ** (-jnp.arange(half, dtype=F32) / half)
    ang = jnp.asarray(pos).astype(F32)[:, None] * inv[None, :]
    cos, sin = jnp.cos(ang), jnp.sin(ang)
    return jnp.concatenate([cos, cos], axis=-1), jnp.concatenate([-sin, sin], axis=-1)


def _nsa_compress_kernel(tk_ref, tv_ref, pek_ref, pev_ref, w1k_ref, w1v_ref, w2k_ref, w2v_ref,
                         g_ref, cos_ref, sin_ref, kc_ref, vc_ref, *, n_rows):
    half = NSA_CMP_LEN // 2

    def compress(t_ref, pe_ref, w1_ref, w2_ref):
        hid = w1_ref.shape[-1]
        lo = jnp.zeros((n_rows, hid), F32)
        hi = jnp.zeros((n_rows, hid), F32)
        for l in range(NSA_CMP_LEN):
            rows = t_ref[pl.ds(l % half, n_rows, stride=NSA_CMP_STRIDE), :]
            a = (rows + pe_ref[l:l + 1, :]).astype(BF16)
            part = _dot(a, w1_ref[l * HEAD_DIM:(l + 1) * HEAD_DIM, :])
            if l < half:
                lo = lo + part
            else:
                hi = hi + part
        pre = lo + pltpu.roll(hi, n_rows - 1, 0)
        return _dot(jax.nn.gelu(pre).astype(BF16), w2_ref[...])

    kc = compress(tk_ref, pek_ref, w1k_ref, w2k_ref)
    kc_ref[...] = _rope(_rms(kc, g_ref[...]), cos_ref[...], sin_ref[...]).astype(BF16)
    vc_ref[...] = compress(tv_ref, pev_ref, w1v_ref, w2v_ref).astype(BF16)


def nsa_compress(z, col0, batch, seq, pe, w1, w2, g_kc, cos_c, sin_c):
    n_rows = seq // NSA_CMP_STRIDE
    G = NSA_GROUPS
    kern = functools.partial(_nsa_compress_kernel, n_rows=n_rows)
    full = lambda a: pl.BlockSpec(a.shape, lambda b, g: (0,) * a.ndim)
    part = lambda a, kv: pl.BlockSpec((None,) + a.shape[1:], lambda b, g: (kv,) + (0,) * (a.ndim - 1))
    g_kc = g_kc.reshape(1, HEAD_DIM)
    return pl.pallas_call(
        kern,
        out_shape=(jax.ShapeDtypeStruct((batch, G, n_rows, HEAD_DIM), BF16),
                   jax.ShapeDtypeStruct((batch, G, n_rows, HEAD_DIM), BF16)),
        grid=(batch, G),
        in_specs=[pl.BlockSpec((seq, LANES), lambda b, g: (b, col0 + g)),
                  pl.BlockSpec((seq, LANES), lambda b, g: (b, col0 + G + g)),
                  part(pe, 0), part(pe, 1), part(w1, 0), part(w1, 1), part(w2, 0), part(w2, 1),
                  full(g_kc), full(cos_c), full(sin_c)],
        out_specs=(pl.BlockSpec((None, None, n_rows, HEAD_DIM), lambda b, g: (b, g, 0, 0)),
                   pl.BlockSpec((None, None, n_rows, HEAD_DIM), lambda b, g: (b, g, 0, 0))),
        compiler_params=_cparams(("arbitrary", "arbitrary")),
        name="nsa_compress",
    )(z, z, pe, pe, w1, w1, w2, w2, g_kc, cos_c, sin_c)


def _softmax_rows(s):
    m = jnp.max(s, axis=1, keepdims=True)
    e = jnp.exp(s - m)
    return e / jnp.maximum(jnp.sum(e, axis=1, keepdims=True), 1e-30)


def _flash_update_t(carry, s, vt):
    m, l, acc = carry
    m_new = jnp.maximum(m, jnp.max(s, axis=0, keepdims=True))
    alpha = jnp.exp2(m - m_new)
    p = jnp.exp2(s - m_new)
    l = alpha * l + jnp.sum(p, axis=0, keepdims=True)
    acc = alpha * acc + _dot(vt, p.astype(BF16))
    return m_new, l, acc


def _flash_init_t(tq, width):
    return (jnp.full((1, tq), NEG, F32), jnp.zeros((1, tq), F32), jnp.zeros((width, tq), F32))


def _causal_keep_t(tk, tq):
    return lax.broadcasted_iota(jnp.int32, (tk, tq), 0) <= lax.broadcasted_iota(jnp.int32, (tk, tq), 1)


Q_CHUNK = 512


def _flash_streams(carries_list, k_tiles, chunk_lists, vt, masks):
    scores = []
    for k_tile, q_chunks in zip(k_tiles, chunk_lists):
        row = []
        for qc, mk in zip(q_chunks, masks):
            if mk is False:
                row.append(None)
                continue
            s = _dot_nt(k_tile, qc)
            row.append(s if mk is None else jnp.where(mk, s, NEG))
        scores.append(row)
    return [tuple(c if s is None else _flash_update_t(c, s, vt) for c, s in zip(carries, row))
            for carries, row in zip(carries_list, scores)]


def _causal_tail(carries_list, q0, n_ch, load_kv, chunk_lists):
    keep = _causal_keep_t(Q_CHUNK, Q_CHUNK)
    for kb in range(n_ch):
        k0 = q0 + kb * Q_CHUNK
        k_tiles, vt = load_kv(k0 if isinstance(k0, int) else pl.multiple_of(k0, Q_CHUNK), Q_CHUNK)
        masks = [False if kb > c else (keep if kb == c else None) for c in range(n_ch)]
        carries_list = _flash_streams(carries_list, k_tiles, chunk_lists, vt, masks)
    return carries_list


def _nsa_attn_kernel(q_ref, kc_ref, vc_ref, ks_ref, kw_ref, vst_ref, vwt_ref, gt_ref,
                     ovt_ref, expt_ref, o_ref, m_ref, l_ref, acc_ref, *, tq, tk, seq):
    R = NSA_REP
    qi = pl.program_id(2)
    q0 = qi * tq
    n_cmp = kc_ref.shape[0]
    n_slc = seq // NSA_SLC_LEN
    q = q_ref[...]
    qs = jnp.concatenate([q[:, r * HEAD_DIM:(r + 1) * HEAD_DIM] for r in range(R)], axis=0)
    qpos1 = q0 + lax.broadcasted_iota(jnp.int32, (1, tq), 1)
    hp = Q_CHUNK // tq
    n_ch = R // hp
    q_chunks = [qs[c * Q_CHUNK:(c + 1) * Q_CHUNK] for c in range(n_ch)]
    qpos = jnp.concatenate([qpos1] * hp, axis=1)

    cmp_end = lax.broadcasted_iota(jnp.int32, (n_cmp, 1), 0) * NSA_CMP_STRIDE + (NSA_CMP_LEN - 1)
    vis = cmp_end <= qpos
    kc = kc_ref[...]
    vct = vc_ref[...].astype(F32).T.astype(BF16)
    p_cs, o_cs = [], []
    for qc in q_chunks:
        s_c = jnp.where(vis, _dot_nt(kc, qc), NEG)
        e_c = jnp.where(vis, jnp.exp2(s_c - jnp.max(s_c, axis=0, keepdims=True)), 0.0)
        p_c = e_c / jnp.maximum(jnp.sum(e_c, axis=0, keepdims=True), 1e-30)
        p_cs.append(p_c)
        o_cs.append(_dot(vct, p_c.astype(BF16)))

    p_sum = None
    for r in range(R):
        part = p_cs[r // hp][:, (r % hp) * tq:(r % hp + 1) * tq]
        p_sum = part if p_sum is None else p_sum + part
    p_hi = p_sum.astype(BF16)
    p_lo = (p_sum - p_hi.astype(F32)).astype(BF16)
    imp = _dot(ovt_ref[...], p_hi) + _dot(ovt_ref[...], p_lo)
    jblk = lax.broadcasted_iota(jnp.int32, (n_slc, tq), 0)
    qblk = qpos1 // NSA_SLC_LEN
    forced = (jblk == 0) | (jblk == qblk) | (jblk == qblk - 1)
    imp = jnp.where(jblk > qblk, -jnp.inf, jnp.where(forced, jnp.inf, imp))
    picked = jnp.zeros((n_slc, tq), jnp.int32)
    for _ in range(min(NSA_SLC_TOPK, n_slc)):
        cand = jnp.where(picked > 0, -jnp.inf, imp)
        best = jnp.max(cand, axis=0, keepdims=True)
        first = jnp.min(jnp.where(cand == best, jblk + picked * n_slc, n_slc), axis=0, keepdims=True)
        picked = jnp.where(jblk == first, 1, picked)
    sel_bias = jnp.where(picked > 0, 0.0, NEG).astype(BF16)

    def slc_tile(j, carries, causal):
        k0 = j * tk if isinstance(j, int) else pl.multiple_of(j * tk, tk)
        ks = ks_ref[pl.ds(k0, tk), :]
        vst = vst_ref[:, pl.ds(k0, tk)]
        bias = _dot(expt_ref[pl.ds(k0, tk), :], sel_bias)
        if causal:
            kpos = k0 + lax.broadcasted_iota(jnp.int32, (tk, 1), 0)
            bias = jnp.where(kpos <= qpos1, bias, NEG)
        bias = jnp.concatenate([bias] * hp, axis=1)
        scores = [_dot_nt(ks, qc) + bias for qc in q_chunks]
        return tuple(_flash_update_t(carry, s, vst) for carry, s in zip(carries, scores))

    last = q0 // tk
    n_tiles = seq // tk
    st_refs = (m_ref, l_ref, acc_ref)
    for c in range(n_ch):
        for ref, val in zip(st_refs, _flash_init_t(Q_CHUNK, HEAD_DIM)):
            ref[c] = val
    for count in range(1, n_tiles):
        @pl.when(last == count)
        def _():
            carries = tuple(_flash_init_t(Q_CHUNK, HEAD_DIM) for _ in range(n_ch))
            for j in range(count):
                carries = slc_tile(j, carries, False)
            for c in range(n_ch):
                for ref, val in zip(st_refs, carries[c]):
                    ref[c] = val
    carries = tuple(tuple(ref[c] for ref in st_refs) for c in range(n_ch))
    o_ss = [acc / jnp.maximum(l, 1e-30) for _, l, acc in slc_tile(last, carries, True)]

    span = NSA_WINDOW + tq
    w0 = pl.multiple_of(jnp.maximum(q0 - NSA_WINDOW, 0), tq)
    rel = qpos - (w0 + lax.broadcasted_iota(jnp.int32, (span, 1), 0))
    kw = kw_ref[pl.ds(w0, span), :]
    vwt = vwt_ref[:, pl.ds(w0, span)]
    o_ws = []
    for qc in q_chunks:
        s_w = jnp.where(rel >= 0, jnp.where(rel < NSA_WINDOW, _dot_nt(kw, qc), NEG), NEG)
        e_w = jnp.exp2(s_w - jnp.max(s_w, axis=0, keepdims=True))
        p_w = e_w / jnp.maximum(jnp.sum(e_w, axis=0, keepdims=True), 1e-30)
        o_ws.append(_dot(vwt, p_w.astype(BF16)))

    gates = jax.nn.sigmoid(gt_ref[...])
    outs = []
    for r in range(R):
        c, cols = r // hp, slice((r % hp) * tq, (r % hp + 1) * tq)
        o_r = (gates[3 * r:3 * r + 1] * o_cs[c][:, cols] + gates[3 * r + 1:3 * r + 2] * o_ss[c][:, cols]
               + gates[3 * r + 2:3 * r + 3] * o_ws[c][:, cols])
        outs.append(o_r.T)
    o_ref[...] = jnp.concatenate(outs, axis=1).astype(BF16)


NSA_GATE_ROWS = 16


def nsa_attention(zb, zt, gt, kc, vc, batch, seq, ks_col, kw_col, vs_row, vw_row):
    tq, tk = 2 * NSA_QBLOCK, 512
    G, R = NSA_GROUPS, NSA_REP
    tk = min(tk, seq)
    assert seq % tk == 0 and tk % tq == 0 and seq >= NSA_WINDOW + tq and 3 * R <= NSA_GATE_ROWS
    assert Q_CHUNK % tq == 0 and R % (Q_CHUNK // tq) == 0
    n_chains = R * tq // Q_CHUNK
    nq = seq // tq
    n_cmp = seq // NSA_CMP_STRIDE
    n_slc = seq // NSA_SLC_LEN
    starts = np.arange(n_cmp) * NSA_CMP_STRIDE
    sb = np.arange(n_slc) * NSA_SLC_LEN
    overlap = np.clip(np.minimum(starts[:, None] + NSA_CMP_LEN, sb[None, :] + NSA_SLC_LEN)
                      - np.maximum(starts[:, None], sb[None, :]), 0, None) / NSA_CMP_LEN
    expand_t = (np.arange(seq)[:, None] // NSA_SLC_LEN == np.arange(n_slc)[None, :]).astype(np.float32)
    kern = functools.partial(_nsa_attn_kernel, tq=tq, tk=tk, seq=seq)
    return pl.pallas_call(
        kern,
        out_shape=jax.ShapeDtypeStruct((batch * seq, G * R * HEAD_DIM), BF16),
        grid=(batch, G, nq),
        in_specs=[pl.BlockSpec((tq, R * HEAD_DIM), lambda b, g, i: (b * nq + i, g)),
                  pl.BlockSpec((None, None, n_cmp, HEAD_DIM), lambda b, g, i: (b, g, 0, 0)),
                  pl.BlockSpec((None, None, n_cmp, HEAD_DIM), lambda b, g, i: (b, g, 0, 0)),
                  pl.BlockSpec((seq, LANES), lambda b, g, i: (b, ks_col + g)),
                  pl.BlockSpec((seq, LANES), lambda b, g, i: (b, kw_col + g)),
                  pl.BlockSpec((None, HEAD_DIM, seq), lambda b, g, i: (b, vs_row + g, 0)),
                  pl.BlockSpec((None, HEAD_DIM, seq), lambda b, g, i: (b, vw_row + g, 0)),
                  pl.BlockSpec((None, NSA_GATE_ROWS, tq), lambda b, g, i: (b, g, i)),
                  pl.BlockSpec((n_slc, n_cmp), lambda b, g, i: (0, 0)),
                  pl.BlockSpec((seq, n_slc), lambda b, g, i: (0, 0))],
        out_specs=pl.BlockSpec((tq, R * HEAD_DIM), lambda b, g, i: (b * nq + i, g)),
        scratch_shapes=[pltpu.VMEM((n_chains, 1, Q_CHUNK), F32), pltpu.VMEM((n_chains, 1, Q_CHUNK), F32),
                        pltpu.VMEM((n_chains, HEAD_DIM, Q_CHUNK), F32)],
        compiler_params=_cparams(("arbitrary", "arbitrary", "arbitrary")),
        name="nsa_attention",
    )(zb, kc, vc, zb, zb, zt, zt, gt, jnp.asarray(overlap.T, BF16), jnp.asarray(expand_t, BF16))


def _diff_attn_kernel(q_ref, k_ref, vt_ref, lam_ref, subg_ref, o_ref, *, tq, tk, nq, lam_init):
    qi = pl.program_id(2)
    n_ch = tq // Q_CHUNK
    q = q_ref[...]
    q1 = [q[c * Q_CHUNK:(c + 1) * Q_CHUNK, :HEAD_DIM] for c in range(n_ch)]
    q2 = [q[c * Q_CHUNK:(c + 1) * Q_CHUNK, HEAD_DIM:] for c in range(n_ch)]

    def load_kv(k0, size):
        k = k_ref[pl.ds(k0, size), :]
        return [k[:, :HEAD_DIM], k[:, HEAD_DIM:]], vt_ref[:, pl.ds(k0, size)]

    width = vt_ref.shape[0]
    lp = lam_ref[...]
    lam = (jnp.exp(jnp.sum(lp[0:1] * lp[1:2], axis=1, keepdims=True))
           - jnp.exp(jnp.sum(lp[2:3] * lp[3:4], axis=1, keepdims=True)) + lam_init)

    for qv in range(nq):
        @pl.when(qi == qv)
        def _():
            carries = [tuple(_flash_init_t(Q_CHUNK, width) for _ in range(n_ch)) for _ in range(2)]
            for j in range(qv * (tq // tk)):
                k_tiles, vt = load_kv(j * tk, tk)
                carries = _flash_streams(carries, k_tiles, [q1, q2], vt, [None] * n_ch)
            c1, c2 = _causal_tail(carries, qv * tq, n_ch, load_kv, [q1, q2])
            o = jnp.concatenate([(a1 / jnp.maximum(l1, 1e-30) - lam * (a2 / jnp.maximum(l2, 1e-30))).T
                                 for (_, l1, a1), (_, l2, a2) in zip(c1, c2)], axis=0)
            o_ref[...] = (_rms(o, subg_ref[...]) * (1.0 - lam_init)).astype(BF16)


def diff_attention(zb, zt, batch, seq, q_col, k_col, vt_row, diff_lam, sub_g, layer, *, tq=1024, tk=512):
    H = DIFF_HEADS
    W2 = 2 * HEAD_DIM
    tq = min(tq, seq)
    tk = min(tk, tq)
    assert seq % tq == 0 and tq % tk == 0 and tq % Q_CHUNK == 0
    nq = seq // tq
    lam_init = 0.8 - 0.6 * math.exp(-0.3 * layer)
    kern = functools.partial(_diff_attn_kernel, tq=tq, tk=tk, nq=nq, lam_init=lam_init)
    return pl.pallas_call(
        kern,
        out_shape=jax.ShapeDtypeStruct((batch * seq, H * W2), BF16),
        grid=(batch, H, nq),
        in_specs=[pl.BlockSpec((tq, W2), lambda b, h, i: (b * nq + i, q_col + h)),
                  pl.BlockSpec((seq, W2), lambda b, h, i: (b, k_col + h)),
                  pl.BlockSpec((None, W2, seq), lambda b, h, i: (b, vt_row + h, 0)),
                  pl.BlockSpec((4, HEAD_DIM), lambda b, h, i: (0, 0)),
                  pl.BlockSpec((1, W2), lambda b, h, i: (0, 0))],
        out_specs=pl.BlockSpec((tq, W2), lambda b, h, i: (b * nq + i, h)),
        compiler_params=_cparams(("arbitrary", "arbitrary", "arbitrary")),
        name="diff_attention",
    )(zb, zb, zt, diff_lam, sub_g.reshape(1, W2))


FOX_AUG = 6


def _fox_attn_kernel(q_ref, qaug_ref, k_ref, kaug_ref, vt_ref, o_ref, *, tq, tk, nq):
    h = pl.program_id(1)
    qi = pl.program_id(2)
    n_ch = tq // Q_CHUNK
    lane = lax.broadcasted_iota(jnp.int32, (tq, LANES), 1)
    mine = (lane >= h * FOX_AUG) & (lane < (h + 1) * FOX_AUG)
    q2 = jnp.concatenate([q_ref[...], jnp.where(mine, qaug_ref[...], jnp.zeros_like(qaug_ref))], axis=1)
    q_chunks = [q2[c * Q_CHUNK:(c + 1) * Q_CHUNK] for c in range(n_ch)]

    def load_kv(k0, size):
        k2 = jnp.concatenate([k_ref[pl.ds(k0, size), :], kaug_ref[pl.ds(k0, size), :]], axis=1)
        return [k2], vt_ref[:, pl.ds(k0, size)]

    for qv in range(nq):
        @pl.when(qi == qv)
        def _():
            carries = tuple(_flash_init_t(Q_CHUNK, HEAD_DIM) for _ in range(n_ch))
            for j in range(qv * (tq // tk)):
                k_tiles, vt = load_kv(j * tk, tk)
                (carries,) = _flash_streams([carries], k_tiles, [q_chunks], vt, [None] * n_ch)
            (carries,) = _causal_tail([carries], qv * tq, n_ch, load_kv, [q_chunks])
            o_ref[...] = jnp.concatenate([(acc / jnp.maximum(l, 1e-30)).T for _, l, acc in carries],
                                         axis=0).astype(BF16)


def fox_attention(zb, zt, qaug, kaug, batch, seq, *, tq=2048, tk=512):
    H = FOX_HEADS
    assert H * FOX_AUG <= LANES
    tq = min(tq, seq)
    tk = min(tk, tq)
    assert seq % tq == 0 and tq % tk == 0 and tq % Q_CHUNK == 0
    nq = seq // tq
    kern = functools.partial(_fox_attn_kernel, tq=tq, tk=tk, nq=nq)
    return pl.pallas_call(
        kern,
        out_shape=jax.ShapeDtypeStruct((batch * seq, H * HEAD_DIM), BF16),
        grid=(batch, H, nq),
        in_specs=[pl.BlockSpec((tq, LANES), lambda b, h, i: (b * nq + i, h)),
                  pl.BlockSpec((tq, LANES), lambda b, h, i: (b * nq + i, 0)),
                  pl.BlockSpec((seq, LANES), lambda b, h, i: (b, H + h)),
                  pl.BlockSpec((seq, LANES), lambda b, h, i: (b, 0)),
                  pl.BlockSpec((None, HEAD_DIM, seq), lambda b, h, i: (b, h, 0))],
        out_specs=pl.BlockSpec((tq, LANES), lambda b, h, i: (b * nq + i, h)),
        compiler_params=_cparams(("arbitrary", "arbitrary", "arbitrary")),
        name="fox_attention",
    )(zb, qaug, zb, kaug, zt)


def _split3(x):
    p1 = x.astype(BF16)
    r1 = x - p1.astype(F32)
    p2 = r1.astype(BF16)
    p3 = (r1 - p2.astype(F32)).astype(BF16)
    return p1, p2, p3


def _fox_gate_kernel(z_ref, b_ref, place_ref, pat_ref, qaug_ref, kaug_ref, carry_ref, *, tr):
    @pl.when(pl.program_id(1) == 0)
    def _():
        carry_ref[...] = jnp.zeros_like(carry_ref)

    logf = jax.nn.log_sigmoid(z_ref[...] + b_ref[...])
    tri = (lax.broadcasted_iota(jnp.int32, (tr, tr), 1) <= lax.broadcasted_iota(jnp.int32, (tr, tr), 0)).astype(BF16)
    p1, p2, p3 = _split3(logf)
    c = _dot(tri, p1) + _dot(tri, p2) + _dot(tri, p3) + carry_ref[...]
    carry_ref[...] = c[tr - 1:tr, :]
    e1, e2, e3 = _split3(c * LOG2E)
    place = place_ref[...]
    cexp = _dot(e1, place) + _dot(e2, place) + _dot(e3, place)
    h1, h2, h3 = _split3(cexp)
    pat = pat_ref[...]
    piece = h1.astype(F32) * pat[0:1] + h2.astype(F32) * pat[1:2] + h3.astype(F32) * pat[2:3]
    kaug_ref[...] = (pat[4:5] - piece * pat[3:4]).astype(BF16)
    qaug_ref[...] = (pat[3:4] + piece * pat[4:5]).astype(BF16)


def fox_gate_bias(z, col_block, bias_row, batch, seq, *, tr=256):
    H = FOX_HEADS
    tr = min(tr, seq)
    nt = seq // tr
    lanes = np.arange(LANES)
    used = lanes < H * FOX_AUG
    place = (lanes[None, :] // FOX_AUG == np.arange(LANES)[:, None]) & used[None, :] & (np.arange(LANES)[:, None] < H)
    j = lanes % FOX_AUG
    pat = np.zeros((8, LANES), np.float32)
    for r in range(3):
        pat[r] = used & (j % 3 == r)
    pat[3] = used & (j < 3)
    pat[4] = used & (j >= 3)
    kern = functools.partial(_fox_gate_kernel, tr=tr)
    out = jax.ShapeDtypeStruct((batch * seq, LANES), BF16)
    return pl.pallas_call(
        kern,
        out_shape=(out, out),
        grid=(batch, nt),
        in_specs=[pl.BlockSpec((tr, LANES), lambda b, i: (b * nt + i, col_block)),
                  pl.BlockSpec((1, LANES), lambda b, i: (0, 0)),
                  pl.BlockSpec((LANES, LANES), lambda b, i: (0, 0)),
                  pl.BlockSpec((8, LANES), lambda b, i: (0, 0))],
        out_specs=(pl.BlockSpec((tr, LANES), lambda b, i: (b * nt + i, 0)),
                   pl.BlockSpec((tr, LANES), lambda b, i: (b * nt + i, 0))),
        scratch_shapes=[pltpu.VMEM((1, LANES), F32)],
        compiler_params=_cparams(("arbitrary", "arbitrary")),
        name="fox_gate_bias",
    )(z, bias_row, jnp.asarray(place, BF16), jnp.asarray(pat))


def _cross_attn_kernel(x_ref, g_ref, wq_ref, kv_ref, qg_ref, kg_ref, wo_ref, o_ref):
    x = x_ref[...]
    q = _dot(_rms(x, g_ref[...]).astype(BF16), wq_ref[...])
    kv = kv_ref[...]
    width = MEM_HEADS * HEAD_DIM
    outs = []
    for h in range(MEM_HEADS):
        cols = slice(h * HEAD_DIM, (h + 1) * HEAD_DIM)
        qh = _rms(q[:, cols], qg_ref[...]).astype(BF16)
        kh = _rms(kv[:, cols], kg_ref[...]).astype(BF16)
        vh = kv[:, width + h * HEAD_DIM: width + (h + 1) * HEAD_DIM].astype(BF16)
        p = _softmax_rows(_dot_nt(qh, kh))
        outs.append(_dot(p.astype(BF16), vh))
    o = jnp.concatenate(outs, axis=1).astype(BF16)
    o_ref[...] = x + _dot(o, wo_ref[...])


def cross_attention(x, g, wq, kv, q_gain_scaled, k_gain, wo, seq, mem_len, *, tm=512):
    M, D = x.shape
    tm = min(tm, seq)
    nt = seq // tm
    width = MEM_HEADS * HEAD_DIM
    return pl.pallas_call(
        _cross_attn_kernel,
        out_shape=jax.ShapeDtypeStruct((M, D), F32),
        grid=(M // tm,),
        in_specs=[pl.BlockSpec((tm, D), lambda i: (i, 0)),
                  pl.BlockSpec((1, D), lambda i: (0, 0)),
                  pl.BlockSpec((D, width), lambda i: (0, 0)),
                  pl.BlockSpec((mem_len, 2 * width), lambda i: (i // nt, 0)),
                  pl.BlockSpec((1, HEAD_DIM), lambda i: (0, 0)),
                  pl.BlockSpec((1, HEAD_DIM), lambda i: (0, 0)),
                  pl.BlockSpec((width, D), lambda i: (0, 0))],
        out_specs=pl.BlockSpec((tm, D), lambda i: (i, 0)),
        compiler_params=_cparams(("arbitrary",)),
        name="cross_attention",
    )(x, g.reshape(1, D), wq, kv, q_gain_scaled.reshape(1, HEAD_DIM), k_gain.reshape(1, HEAD_DIM), wo)


def _pack_bf16_pairs(x_bf16):
    n = x_bf16.shape[1] // 2
    lo = pltpu.bitcast(x_bf16[:, :n].astype(F32), jnp.uint32)
    hi = pltpu.bitcast(x_bf16[:, n:].astype(F32), jnp.uint32)
    return (hi & jnp.uint32(0xFFFF0000)) | (lo >> jnp.uint32(16))


def _unpack_bf16_pairs(words):
    lo = pltpu.bitcast(words << jnp.uint32(16), F32).astype(BF16)
    hi = pltpu.bitcast(words & jnp.uint32(0xFFFF0000), F32).astype(BF16)
    return jnp.concatenate([lo, hi], axis=1)


def _moe_router_kernel(x_ref, g_ref, wr_ref, br_ref, hn_ref, route_ref, onehot_ref):
    hn = _rms(x_ref[...], g_ref[...])
    h_hi = hn.astype(BF16)
    hn_ref[...] = _pack_bf16_pairs(h_hi)
    w = wr_ref[...]
    w_hi = w.astype(BF16)
    w_lo = (w - w_hi.astype(F32)).astype(BF16)
    h_lo = (hn - h_hi.astype(F32)).astype(BF16)
    logits = _dot(h_hi, w_hi) + _dot(h_lo, w_hi) + _dot(h_hi, w_lo) + br_ref[...]
    tm = logits.shape[0]
    lane = lax.broadcasted_iota(jnp.int32, (tm, LANES), 1)
    is_grp = lane < N_GROUPS
    lg = jnp.where(is_grp, logits, -jnp.inf)
    eg = jnp.exp(lg - jnp.max(lg, axis=1, keepdims=True))
    p_grp = eg / jnp.sum(eg, axis=1, keepdims=True)
    p_top = jnp.max(p_grp, axis=1, keepdims=True)
    grp = jnp.min(jnp.where(is_grp & (p_grp == p_top), lane, LANES), axis=1, keepdims=True)
    lo = N_GROUPS + grp * EXPERTS_PER_GROUP
    in_grp = (lane >= lo) & (lane < lo + EXPERTS_PER_GROUP)
    le = jnp.where(in_grp, logits, -jnp.inf)
    ee = jnp.exp(le - jnp.max(le, axis=1, keepdims=True))
    p_in = ee / jnp.sum(ee, axis=1, keepdims=True)
    m1 = jnp.max(p_in, axis=1, keepdims=True)
    i1 = jnp.min(jnp.where(in_grp & (p_in == m1), lane, LANES), axis=1, keepdims=True)
    rest = jnp.where(in_grp & (lane != i1), p_in, -jnp.inf)
    m2 = jnp.max(rest, axis=1, keepdims=True)
    i2 = jnp.min(jnp.where(rest == m2, lane, LANES), axis=1, keepdims=True)
    denom = m1 + m2
    w1 = p_top * m1 / denom
    w2 = p_top * m2 / denom
    e1 = (i1 - N_GROUPS).astype(F32)
    e2 = (i2 - N_GROUPS).astype(F32)
    route_ref[...] = jnp.where(lane == 0, e1, jnp.where(lane == 1, e2, jnp.where(lane == 2, w1, jnp.where(lane == 3, w2, 0.0))))
    onehot_ref[...] = ((lane == i1 - N_GROUPS) | (lane == i2 - N_GROUPS)).astype(BF16)


def moe_router(x, g, w_router, b_router, *, tm=512):
    M, D = x.shape
    tm = min(tm, M)
    return pl.pallas_call(
        _moe_router_kernel,
        out_shape=(jax.ShapeDtypeStruct((M, D // 2), jnp.uint32),
                   jax.ShapeDtypeStruct((M, LANES), F32),
                   jax.ShapeDtypeStruct((M, LANES), BF16)),
        grid=(M // tm,),
        in_specs=[pl.BlockSpec((tm, D), lambda i: (i, 0)),
                  pl.BlockSpec((1, D), lambda i: (0, 0)),
                  pl.BlockSpec((D, LANES), lambda i: (0, 0)),
                  pl.BlockSpec((1, LANES), lambda i: (0, 0))],
        out_specs=(pl.BlockSpec((tm, D // 2), lambda i: (i, 0)),
                   pl.BlockSpec((tm, LANES), lambda i: (i, 0)),
                   pl.BlockSpec((tm, LANES), lambda i: (i, 0))),
        compiler_params=_cparams(("arbitrary",)),
        name="moe_router",
    )(x, g.reshape(1, D), w_router, b_router)


def _moe_rank_kernel(onehot_ref, rank_ref, count_ref, carry_ref, *, tr):
    i = pl.program_id(0)

    @pl.when(i == 0)
    def _():
        carry_ref[...] = jnp.zeros_like(carry_ref)

    oh = onehot_ref[...]
    tri = (lax.broadcasted_iota(jnp.int32, (tr, tr), 1) < lax.broadcasted_iota(jnp.int32, (tr, tr), 0)).astype(BF16)
    rank_ref[...] = _dot(tri, oh) + carry_ref[...]
    carry_ref[...] = carry_ref[...] + jnp.sum(oh.astype(F32), axis=0, keepdims=True)
    count_ref[...] = carry_ref[...]


def moe_rank(onehot, *, tr=512):
    M = onehot.shape[0]
    tr = min(tr, M)
    kern = functools.partial(_moe_rank_kernel, tr=tr)
    return pl.pallas_call(
        kern,
        out_shape=(jax.ShapeDtypeStruct((M, LANES), F32), jax.ShapeDtypeStruct((1, LANES), F32)),
        grid=(M // tr,),
        in_specs=[pl.BlockSpec((tr, LANES), lambda i: (i, 0))],
        out_specs=(pl.BlockSpec((tr, LANES), lambda i: (i, 0)), pl.BlockSpec((1, LANES), lambda i: (0, 0))),
        scratch_shapes=[pltpu.VMEM((1, LANES), F32)],
        compiler_params=_cparams(("arbitrary",)),
        name="moe_rank",
    )(onehot)


def _moe_slot_kernel(route_ref, rank_ref, pstart_ref, o_ref):
    route = route_ref[...]
    slot = rank_ref[...] + pstart_ref[...]
    lane = lax.broadcasted_iota(jnp.int32, route.shape, 1)
    e1 = route[:, 0:1].astype(jnp.int32)
    e2 = route[:, 1:2].astype(jnp.int32)
    d1 = jnp.sum(jnp.where(lane == e1, slot, 0.0), axis=1, keepdims=True)
    d2 = jnp.sum(jnp.where(lane == e2, slot, 0.0), axis=1, keepdims=True)
    o_ref[...] = jnp.where(lane == 0, d1, jnp.where(lane == 1, d2, 0.0)).astype(jnp.int32)


def moe_slots(route, rank, pstart_row, *, tm=1024):
    M = route.shape[0]
    tm = min(tm, M)
    return pl.pallas_call(
        _moe_slot_kernel,
        out_shape=jax.ShapeDtypeStruct((M, LANES), jnp.int32),
        grid=(M // tm,),
        in_specs=[pl.BlockSpec((tm, LANES), lambda i: (i, 0)),
                  pl.BlockSpec((tm, LANES), lambda i: (i, 0)),
                  pl.BlockSpec((1, LANES), lambda i: (0, 0))],
        out_specs=pl.BlockSpec((tm, LANES), lambda i: (i, 0)),
        compiler_params=_cparams(("arbitrary",)),
        name="moe_slots",
    )(route, rank, pstart_row)


def _moe_dispatch_kernel(lastblk_ref, d1_hbm, d2_hbm, hn_ref, xb_hbm, zero_ref, s1_ref, s2_ref,
                         idx_sem, row_sem, zero_sem, *, tm, n_blocks):
    i = pl.program_id(0)

    def zero_copy(blk):
        return pltpu.make_async_copy(zero_ref, xb_hbm.at[pl.ds(blk * MOE_ROWS, MOE_ROWS)], zero_sem)

    @pl.when(i == 0)
    def _():
        zero_ref[...] = jnp.zeros_like(zero_ref)
        n_used = lastblk_ref[N_EXPERTS]
        for e in range(N_EXPERTS):
            @pl.when(lastblk_ref[e] >= 0)
            def _():
                zero_copy(lastblk_ref[e]).start()

        def start_tail(blk, _):
            zero_copy(blk).start()
            return 0

        def wait_tail(blk, _):
            zero_copy(blk).wait()
            return 0

        lax.fori_loop(n_used, n_blocks, start_tail, 0)
        for e in range(N_EXPERTS):
            @pl.when(lastblk_ref[e] >= 0)
            def _():
                zero_copy(lastblk_ref[e]).wait()
        lax.fori_loop(n_used, n_blocks, wait_tail, 0)

    c1 = pltpu.make_async_copy(d1_hbm.at[i], s1_ref, idx_sem.at[0])
    c2 = pltpu.make_async_copy(d2_hbm.at[i], s2_ref, idx_sem.at[1])
    c1.start()
    c2.start()
    c1.wait()
    c2.wait()

    def row_copy(r, slot):
        return pltpu.make_async_copy(hn_ref.at[pl.ds(r, 1)], xb_hbm.at[pl.ds(slot, 1)], row_sem)

    def issue(blk, _):
        for u in range(ROW_DMA_UNROLL):
            r = blk * ROW_DMA_UNROLL + u
            row_copy(r, s1_ref[r]).start(priority=0)
            row_copy(r, s2_ref[r]).start(priority=1)
        return 0

    lax.fori_loop(0, tm // ROW_DMA_UNROLL, issue, 0)

    def drain(r, _):
        row_copy(r, 0).wait()
        row_copy(r, 0).wait()
        return 0

    lax.fori_loop(0, tm, drain, 0, unroll=8)


def moe_dispatch(hn, d1, d2, lastblk, n_slots, *, tm=1024):
    M, D = hn.shape
    assert hn.dtype.itemsize == 4
    tm = min(tm, M)
    assert n_slots % MOE_ROWS == 0
    kern = functools.partial(_moe_dispatch_kernel, tm=tm, n_blocks=n_slots // MOE_ROWS)
    grid_spec = pltpu.PrefetchScalarGridSpec(
        num_scalar_prefetch=1,
        grid=(M // tm,),
        in_specs=[pl.BlockSpec(memory_space=pl.ANY), pl.BlockSpec(memory_space=pl.ANY),
                  pl.BlockSpec((tm, D), lambda i, *_: (i, 0))],
        out_specs=pl.BlockSpec(memory_space=pl.ANY),
        scratch_shapes=[pltpu.VMEM((MOE_ROWS, D), hn.dtype),
                        pltpu.SMEM((tm,), jnp.int32), pltpu.SMEM((tm,), jnp.int32),
                        pltpu.SemaphoreType.DMA((2,)), pltpu.SemaphoreType.DMA, pltpu.SemaphoreType.DMA],
    )
    return pl.pallas_call(
        kern,
        out_shape=jax.ShapeDtypeStruct((n_slots, D), hn.dtype),
        grid_spec=grid_spec,
        compiler_params=_cparams(("arbitrary",)),
        name="moe_dispatch",
    )(lastblk, d1.reshape(M // tm, tm), d2.reshape(M // tm, tm), hn)


def _moe_expert_kernel(blk_e_ref, n_used_ref, x_ref, w1_ref, w3_ref, w2_ref, o_ref, w1b, w3b, w2b):
    i = pl.program_id(0)

    @pl.when(i < n_used_ref[0])
    def _():
        prev = blk_e_ref[jnp.maximum(i - 1, 0)]

        @pl.when((i == 0) | (blk_e_ref[i] != prev))
        def _():
            w1b[...] = w1_ref[...].astype(BF16)
            w3b[...] = w3_ref[...].astype(BF16)
            w2b[...] = w2_ref[...].astype(BF16)

        x = _unpack_bf16_pairs(x_ref[...])
        h = (jax.nn.silu(_dot(x, w1b[...])) * _dot(x, w3b[...])).astype(BF16)
        o_ref[...] = _dot(h, w2b[...])

    @pl.when(i >= n_used_ref[0])
    def _():
        o_ref[...] = jnp.zeros_like(o_ref)


def moe_experts(xb, blk_e, n_used, w1, w3, w2, layer):
    P = xb.shape[0]
    D = w1.shape[-2]
    FF = w1.shape[-1]
    n_blk = P // MOE_ROWS

    def row_map(i, blk_e_ref, n_used_ref):
        return (jnp.minimum(i, n_used_ref[0] - 1), 0)

    def out_map(i, blk_e_ref, n_used_ref):
        return (i, 0)

    def w_map(i, blk_e_ref, n_used_ref):
        return (layer, blk_e_ref[i], 0, 0)

    grid_spec = pltpu.PrefetchScalarGridSpec(
        num_scalar_prefetch=2,
        grid=(n_blk,),
        in_specs=[pl.BlockSpec((MOE_ROWS, D // 2), row_map),
                  pl.BlockSpec((None, None, D, FF), w_map),
                  pl.BlockSpec((None, None, D, FF), w_map),
                  pl.BlockSpec((None, None, FF, D), w_map)],
        out_specs=pl.BlockSpec((MOE_ROWS, D), out_map),
        scratch_shapes=[pltpu.VMEM((D, FF), BF16), pltpu.VMEM((D, FF), BF16), pltpu.VMEM((FF, D), BF16)],
    )
    return pl.pallas_call(
        _moe_expert_kernel,
        out_shape=jax.ShapeDtypeStruct((P, D), F32),
        grid_spec=grid_spec,
        compiler_params=_cparams(("arbitrary",)),
        name="moe_experts",
    )(blk_e, n_used, xb, w1, w3, w2)


def _moe_combine_kernel(x_ref, route_ref, d1_hbm, d2_hbm, yb_hbm, o_ref, y1_ref, y2_ref, s1_ref, s2_ref,
                        idx_sem, row_sem, *, tm):
    i = pl.program_id(0)
    c1 = pltpu.make_async_copy(d1_hbm.at[i], s1_ref, idx_sem.at[0])
    c2 = pltpu.make_async_copy(d2_hbm.at[i], s2_ref, idx_sem.at[1])
    c1.start()
    c2.start()
    c1.wait()
    c2.wait()

    def row_copy(slot, dst, r):
        return pltpu.make_async_copy(yb_hbm.at[pl.ds(slot, 1)], dst.at[pl.ds(r, 1)], row_sem)

    def issue(blk, _):
        for u in range(ROW_DMA_UNROLL):
            r = blk * ROW_DMA_UNROLL + u
            row_copy(s1_ref[r], y1_ref, r).start(priority=0)
            row_copy(s2_ref[r], y2_ref, r).start(priority=1)
        return 0

    lax.fori_loop(0, tm // ROW_DMA_UNROLL, issue, 0)

    def drain(r, _):
        row_copy(0, y1_ref, r).wait()
        row_copy(0, y2_ref, r).wait()
        return 0

    lax.fori_loop(0, tm, drain, 0, unroll=8)
    route = route_ref[...]
    o_ref[...] = x_ref[...] + (y1_ref[...] * route[:, 2:3] + y2_ref[...] * route[:, 3:4])


def moe_combine(x, route, d1, d2, yb, *, tm=512):
    M, D = x.shape
    tm = min(tm, M)
    kern = functools.partial(_moe_combine_kernel, tm=tm)
    return pl.pallas_call(
        kern,
        out_shape=jax.ShapeDtypeStruct((M, D), F32),
        grid=(M // tm,),
        in_specs=[pl.BlockSpec((tm, D), lambda i: (i, 0)),
                  pl.BlockSpec((tm, LANES), lambda i: (i, 0)),
                  pl.BlockSpec(memory_space=pl.ANY),
                  pl.BlockSpec(memory_space=pl.ANY),
                  pl.BlockSpec(memory_space=pl.ANY)],
        out_specs=pl.BlockSpec((tm, D), lambda i: (i, 0)),
        scratch_shapes=[pltpu.VMEM((tm, D), F32), pltpu.VMEM((tm, D), F32),
                        pltpu.SMEM((tm,), jnp.int32), pltpu.SMEM((tm,), jnp.int32),
                        pltpu.SemaphoreType.DMA((2,)), pltpu.SemaphoreType.DMA],
        compiler_params=_cparams(("arbitrary",)),
        name="moe_combine",
    )(x, route, d1.reshape(M // tm, tm), d2.reshape(M // tm, tm), yb)


def hier_moe(x, g, wg, bg, we, be, w1, w3, w2, layer):
    M, D = x.shape
    n_route = N_GROUPS + N_EXPERTS
    w_router = jnp.pad(jnp.concatenate([wg, we], axis=1), ((0, 0), (0, LANES - n_route)))
    b_router = jnp.pad(jnp.concatenate([bg, be]), (0, LANES - n_route)).reshape(1, LANES)
    hn, route, onehot = moe_router(x, g, w_router, b_router)
    rank, counts = moe_rank(onehot)
    cnt = counts[0, :N_EXPERTS].astype(jnp.int32)
    nblk = (cnt + MOE_ROWS - 1) // MOE_ROWS
    bend = jnp.cumsum(nblk)
    bstart = bend - nblk
    n_blk_max = (2 * M) // MOE_ROWS + N_EXPERTS
    pstart_row = jnp.pad((bstart * MOE_ROWS).astype(F32), (0, LANES - N_EXPERTS)).reshape(1, LANES)
    slots = moe_slots(route, rank, pstart_row)
    d1, d2 = slots[:, 0], slots[:, 1]
    n_used = bend[-1:].astype(jnp.int32)
    lastblk = jnp.concatenate([jnp.where(nblk > 0, bend - 1, -1).astype(jnp.int32), n_used])
    blk_ids = jnp.arange(n_blk_max, dtype=jnp.int32)
    blk_e = jnp.minimum(jnp.sum(bend[None, :] <= blk_ids[:, None], axis=1), N_EXPERTS - 1).astype(jnp.int32)
    xb = moe_dispatch(hn, d1, d2, lastblk, n_blk_max * MOE_ROWS)
    yb = moe_experts(xb, blk_e, n_used, w1, w3, w2, layer)
    return moe_combine(x, route, d1, d2, yb)


def _even_mixer(x, g, w_in, w_out, nsa_qk_g, cmp_pe, cmp_w1, cmp_w2, diff_qk_g, diff_lam, diff_sub_g,
                layer, batch, seq, cos, sin):
    M, D = x.shape
    scale = HEAD_DIM ** -0.5
    nq_w = NSA_HEADS * HEAD_DIM
    kv_w = 6 * NSA_GROUPS * HEAD_DIM
    gate_w = 3 * NSA_HEADS
    dq_w = 2 * DIFF_HEADS * HEAD_DIM
    dv_w = DIFF_HEADS * 2 * HEAD_DIM
    c_gate = nq_w + kv_w
    c_diff = c_gate + gate_w
    main_w = nq_w + kv_w + 2 * dq_w + dv_w
    G, R3 = NSA_GROUPS, 3 * NSA_REP
    gate_cols = []
    for grp in range(G):
        gate_cols += [w_in[:, c_gate + grp * R3:c_gate + (grp + 1) * R3], jnp.zeros((D, NSA_GATE_ROWS - R3), w_in.dtype)]
    gate_cols.append(jnp.zeros((D, LANES - G * NSA_GATE_ROWS), w_in.dtype))
    w_cat = jnp.concatenate([w_in[:, :c_gate], w_in[:, c_diff:]] + gate_cols, axis=1).astype(BF16)
    b_kv, b_dq, b_dk, b_dv, b_gate = nq_w // LANES, (nq_w + kv_w) // LANES, (nq_w + kv_w + dq_w) // LANES, \
        (nq_w + kv_w + 2 * dq_w) // LANES, main_w // LANES
    qs = scale * LOG2E
    gains = jnp.stack([nsa_qk_g[0] * qs, nsa_qk_g[2], nsa_qk_g[3], diff_qk_g[0] * qs, diff_qk_g[1]])
    modes = ([("row", 0, True)] * (b_kv - 0)
             + [("raw", None, False)] * (2 * G)
             + [("row", 1, True)] * G + [("t", None, False)] * G
             + [("row", 2, True)] * G + [("t", None, False)] * G
             + [("row", 3, True)] * (b_dk - b_dq) + [("row", 4, True)] * (b_dv - b_dk)
             + [("t", None, False)] * (b_gate - b_dv)
             + [("t32", None, False)])
    proj = head_projection(x, g, w_cat, modes, gains, cos, sin, batch, seq, name="even_in_proj")
    zb, zt, gt, zc = proj["row"], proj["t"], proj["t32"], proj["raw"]
    r_ks = b_kv
    r_kw, r_dq = r_ks + G, r_ks + 2 * G
    r_dk = r_dq + (b_dk - b_dq)
    cmp_end = np.arange(seq // NSA_CMP_STRIDE) * NSA_CMP_STRIDE + NSA_CMP_LEN - 1
    cos_c, sin_c = rope_tables(cmp_end)
    kc, vc = nsa_compress(zc, 0, batch, seq, cmp_pe, cmp_w1.astype(BF16), cmp_w2.astype(BF16), nsa_qk_g[1],
                          cos_c, sin_c)
    o_nsa = nsa_attention(zb, zt, gt, kc, vc, batch, seq, r_ks, r_kw, 0, G)
    o_diff = diff_attention(zb, zt, batch, seq, r_dq // 2, r_dk // 2, (2 * G) // 2, diff_lam, diff_sub_g, layer)
    return matmul_residual([o_nsa, o_diff], w_out.astype(BF16), x, name="even_out_proj")


def _odd_mixer(x, g, w_in, w_out, f_b, qk_g, batch, seq, cos, sin):
    M, D = x.shape
    H = FOX_HEADS
    scale = HEAD_DIM ** -0.5
    width = H * HEAD_DIM
    w_cat = jnp.pad(w_in, ((0, 0), (0, LANES - H))).astype(BF16)
    modes = [("row", 0, False)] * H + [("row", 1, False)] * H + [("t", None, False)] * H + [("raw", None, False)]
    gains = jnp.stack([qk_g[0] * (scale * LOG2E), qk_g[1]])
    proj = head_projection(x, g, w_cat, modes, gains, cos, sin, batch, seq, name="odd_in_proj")
    bias_row = jnp.pad(f_b, (0, LANES - H)).reshape(1, LANES)
    qaug, kaug = fox_gate_bias(proj["raw"], 0, bias_row, batch, seq)
    o = fox_attention(proj["row"], proj["t"], qaug, kaug, batch, seq)
    return matmul_residual([o], w_out.astype(BF16), x, name="odd_out_proj")


def kernel(x, mem, norm_g, ev_w_in, ev_w_out, nsa_qk_g, nsa_cmp_pe, nsa_cmp_w1, nsa_cmp_w2, diff_qk_g, diff_lam, diff_sub_g, od_w_in, od_w_out, fox_f_b, fox_qk_g, ca_wq, ca_wkv, ca_qk_g, ca_wo, moe_wg, moe_bg, moe_we, moe_be, moe_w1, moe_w3, moe_w2):
    B, T, D = x.shape
    mem_len = mem.shape[1]
    depth = norm_g.shape[0]
    scale = HEAD_DIM ** -0.5
    cos, sin = rope_tables(np.arange(T))
    xt = x.reshape(B * T, D)
    mt = mem.reshape(B * mem_len, D)
    for layer in range(depth):
        i = layer // 2
        if layer % 2 == 0:
            xt = _even_mixer(xt, norm_g[layer, 0], ev_w_in[i], ev_w_out[i], nsa_qk_g[i], nsa_cmp_pe[i],
                             nsa_cmp_w1[i], nsa_cmp_w2[i], diff_qk_g[i], diff_lam[i], diff_sub_g[i], layer,
                             B, T, cos, sin)
        else:
            xt = _odd_mixer(xt, norm_g[layer, 0], od_w_in[i], od_w_out[i], fox_f_b[i], fox_qk_g[i], B, T, cos, sin)
        kv = rms_matmul(mt, norm_g[layer, 2], ca_wkv[layer].astype(BF16), name="mem_kv_proj")
        xt = cross_attention(xt, norm_g[layer, 1], ca_wq[layer].astype(BF16), kv, ca_qk_g[layer, 0] * scale,
                             ca_qk_g[layer, 1], ca_wo[layer].astype(BF16), T, mem_len)
        xt = hier_moe(xt, norm_g[layer, 3], moe_wg[layer], moe_bg[layer], moe_we[layer], moe_be[layer],
                      moe_w1, moe_w3, moe_w2, layer)
    return xt.reshape(B, T, D)
```

```python
import functools
import math

import numpy as np
import jax
import jax.numpy as jnp
from jax import lax
from jax.experimental import pallas as pl
from jax.experimental.pallas import tpu as pltpu

F32 = jnp.float32
BF16 = jnp.bfloat16

HEAD_DIM = 128
ROPE_THETA = 10000.0
EPS = 1e-6
NEG = -1e30
LOG2E = math.log2(math.e)

NSA_HEADS = 8
NSA_GROUPS = 2
NSA_REP = NSA_HEADS // NSA_GROUPS
NSA_CMP_LEN = 32
NSA_CMP_STRIDE = 16
NSA_CMP_HIDDEN = 256
NSA_SLC_LEN = 64
NSA_SLC_TOPK = 8
NSA_WINDOW = 512
NSA_QBLOCK = 128
DIFF_HEADS = 4
FOX_HEADS = 16
MEM_HEADS = 4
N_GROUPS = 4
EXPERTS_PER_GROUP = 8
N_EXPERTS = N_GROUPS * EXPERTS_PER_GROUP
MOE_ROWS = 256
ROW_DMA_UNROLL = 8
LANES = 128
VMEM_LIMIT = 56 * 1024 * 1024


def _cparams(sem):
    return pltpu.CompilerParams(dimension_semantics=sem, vmem_limit_bytes=VMEM_LIMIT)


def _dot(a, b):
    return jnp.dot(a, b, preferred_element_type=F32)


def _dot_nt(a, b):
    return lax.dot_general(a, b, (((1,), (1,)), ((), ())), preferred_element_type=F32)


def _rms(x, g):
    ms = jnp.mean(x * x, axis=-1, keepdims=True)
    return x * lax.rsqrt(ms + EPS) * g


def _rope(y, cos, sin_signed):
    return y * cos + pltpu.roll(y, HEAD_DIM // 2, 1) * sin_signed


def _rms_matmul_kernel(x_ref, g_ref, w_ref, o_ref, xn_ref):
    @pl.when(pl.program_id(1) == 0)
    def _():
        xn_ref[...] = _rms(x_ref[...], g_ref[...]).astype(BF16)

    o_ref[...] = _dot(xn_ref[...], w_ref[...])


def rms_matmul(x, g, w, *, tm=1024, tn=512, name="rms_matmul"):
    M, K = x.shape
    N = w.shape[1]
    tm, tn = min(tm, M), min(tn, N)
    assert M % tm == 0 and N % tn == 0
    return pl.pallas_call(
        _rms_matmul_kernel,
        out_shape=jax.ShapeDtypeStruct((M, N), F32),
        grid=(M // tm, N // tn),
        in_specs=[pl.BlockSpec((tm, K), lambda i, j: (i, 0)),
                  pl.BlockSpec((1, K), lambda i, j: (0, 0)),
                  pl.BlockSpec((K, tn), lambda i, j: (0, j))],
        out_specs=pl.BlockSpec((tm, tn), lambda i, j: (i, j)),
        scratch_shapes=[pltpu.VMEM((tm, K), BF16)],
        compiler_params=_cparams(("arbitrary", "arbitrary")),
        name=name,
    )(x, g.reshape(1, K), w)


def _matmul_res_kernel(*refs, tn):
    *x_refs, w_ref, r_ref, o_ref = refs
    for n0 in range(0, o_ref.shape[1], tn):
        acc = r_ref[:, n0:n0 + tn]
        k0 = 0
        for x_ref in x_refs:
            k = x_ref.shape[1]
            acc = acc + _dot(x_ref[...], w_ref[k0:k0 + k, n0:n0 + tn])
            k0 += k
        o_ref[:, n0:n0 + tn] = acc


def matmul_residual(xs, w, res, *, tm=512, tn=512, name="matmul_residual"):
    M = xs[0].shape[0]
    N = w.shape[1]
    tm, tn = min(tm, M), min(tn, N)
    assert M % tm == 0 and N % tn == 0 and sum(x.shape[1] for x in xs) == w.shape[0]
    return pl.pallas_call(
        functools.partial(_matmul_res_kernel, tn=tn),
        out_shape=jax.ShapeDtypeStruct((M, N), F32),
        grid=(M // tm,),
        in_specs=[pl.BlockSpec((tm, x.shape[1]), lambda i: (i, 0)) for x in xs]
        + [pl.BlockSpec(w.shape, lambda i: (0, 0), pipeline_mode=pl.Buffered(1)),
           pl.BlockSpec((tm, N), lambda i: (i, 0))],
        out_specs=pl.BlockSpec((tm, N), lambda i: (i, 0)),
        compiler_params=_cparams(("arbitrary",)),
        name=name,
    )(*xs, w, res)


HEAD_KINDS = ("row", "t", "t32", "raw")


def _head_proj_kernel(x_ref, g_ref, w_ref, gains_ref, cos_ref, sin_ref, *out_refs, groups, kinds):
    refs = dict(zip(kinds, out_refs))
    xn = _rms(x_ref[...], g_ref[...]).astype(BF16)
    col = 0
    for modes in groups:
        z = _dot(xn, w_ref[:, col:col + len(modes) * LANES])
        col += len(modes) * LANES
        for c, (kind, gain_row, rotary, dst) in enumerate(modes):
            if kind == "skip":
                continue
            y = z[:, c * LANES:(c + 1) * LANES]
            if gain_row is not None:
                y = _rms(y, gains_ref[gain_row:gain_row + 1, :])
            if rotary:
                y = _rope(y, cos_ref[...], sin_ref[...])
            span = slice(dst * LANES, (dst + 1) * LANES)
            if kind == "row":
                refs[kind][:, span] = y.astype(BF16)
            elif kind == "raw":
                refs[kind][:, span] = y
            elif kind == "t":
                refs[kind][span, :] = y.T.astype(BF16)
            else:
                refs[kind][span, :] = y.T


def head_projection(x, g, w, modes, gains, cos, sin, batch, seq, *, group=8, tm=512, name="head_projection"):
    M, K = x.shape
    n_blocks = w.shape[1] // LANES
    assert len(modes) == n_blocks
    tm = min(tm, seq)
    assert seq % tm == 0 and M % tm == 0 and tm % LANES == 0
    nt = seq // tm
    counts = {k: 0 for k in HEAD_KINDS}
    placed = []
    for kind, gain_row, rotary in modes:
        placed.append((kind, gain_row, rotary, counts.get(kind, 0)))
        if kind in counts:
            counts[kind] += 1
    groups = tuple(tuple(placed[i:i + group]) for i in range(0, n_blocks, group))
    kinds = tuple(k for k in HEAD_KINDS if counts[k])
    out_shape, out_specs = [], []
    for kind in kinds:
        width = counts[kind] * LANES
        dt = BF16 if kind in ("row", "t") else F32
        if kind in ("row", "raw"):
            out_shape.append(jax.ShapeDtypeStruct((M, width), dt))
            out_specs.append(pl.BlockSpec((tm, width), lambda i: (i, 0)))
        else:
            out_shape.append(jax.ShapeDtypeStruct((batch, width, seq), dt))
            out_specs.append(pl.BlockSpec((None, width, tm), lambda i: (i // nt, 0, i % nt)))
    kern = functools.partial(_head_proj_kernel, groups=groups, kinds=kinds)
    outs = pl.pallas_call(
        kern,
        out_shape=tuple(out_shape),
        grid=(M // tm,),
        in_specs=[pl.BlockSpec((tm, K), lambda i: (i, 0)),
                  pl.BlockSpec((1, K), lambda i: (0, 0)),
                  pl.BlockSpec(w.shape, lambda i: (0, 0), pipeline_mode=pl.Buffered(1)),
                  pl.BlockSpec(gains.shape, lambda i: (0, 0)),
                  pl.BlockSpec((tm, LANES), lambda i: (i % nt, 0)),
                  pl.BlockSpec((tm, LANES), lambda i: (i % nt, 0))],
        out_specs=tuple(out_specs),
        compiler_params=_cparams(("arbitrary",)),
        name=name,
    )(x, g.reshape(1, K), w, gains, cos, sin)
    return dict(zip(kinds, outs))


def rope_tables(pos):
    half = HEAD_DIM // 2
    inv = ROPE_THETA ** (-jnp.arange(half, dtype=F32) / half)
    ang = jnp.asarray(pos).astype(F32)[:, None] * inv[None, :]
    cos, sin = jnp.cos(ang), jnp.sin(ang)
    return jnp.concatenate([cos, cos], axis=-1), jnp.concatenate([-sin, sin], axis=-1)


def _nsa_compress_kernel(tk_ref, tv_ref, pek_ref, pev_ref, w1k_ref, w1v_ref, w2k_ref, w2v_ref,
                         g_ref, cos_ref, sin_ref, kc_ref, vc_ref, *, n_rows):
    half = NSA_CMP_LEN // 2

    def compress(t_ref, pe_ref, w1_ref, w2_ref):
        hid = w1_ref.shape[-1]
        lo = jnp.zeros((n_rows, hid), F32)
        hi = jnp.zeros((n_rows, hid), F32)
        for l in range(NSA_CMP_LEN):
            rows = t_ref[pl.ds(l % half, n_rows, stride=NSA_CMP_STRIDE), :]
            a = (rows + pe_ref[l:l + 1, :]).astype(BF16)
            part = _dot(a, w1_ref[l * HEAD_DIM:(l + 1) * HEAD_DIM, :])
            if l < half:
                lo = lo + part
            else:
                hi = hi + part
        pre = lo + pltpu.roll(hi, n_rows - 1, 0)
        return _dot(jax.nn.gelu(pre).astype(BF16), w2_ref[...])

    kc = compress(tk_ref, pek_ref, w1k_ref, w2k_ref)
    kc_ref[...] = _rope(_rms(kc, g_ref[...]), cos_ref[...], sin_ref[...]).astype(BF16)
    vc_ref[...] = compress(tv_ref, pev_ref, w1v_ref, w2v_ref).astype(BF16)


def nsa_compress(z, col0, batch, seq, pe, w1, w2, g_kc, cos_c, sin_c):
    n_rows = seq // NSA_CMP_STRIDE
    G = NSA_GROUPS
    kern = functools.partial(_nsa_compress_kernel, n_rows=n_rows)
    full = lambda a: pl.BlockSpec(a.shape, lambda b, g: (0,) * a.ndim)
    part = lambda a, kv: pl.BlockSpec((None,) + a.shape[1:], lambda b, g: (kv,) + (0,) * (a.ndim - 1))
    g_kc = g_kc.reshape(1, HEAD_DIM)
    return pl.pallas_call(
        kern,
        out_shape=(jax.ShapeDtypeStruct((batch, G, n_rows, HEAD_DIM), BF16),
                   jax.ShapeDtypeStruct((batch, G, n_rows, HEAD_DIM), BF16)),
        grid=(batch, G),
        in_specs=[pl.BlockSpec((seq, LANES), lambda b, g: (b, col0 + g)),
                  pl.BlockSpec((seq, LANES), lambda b, g: (b, col0 + G + g)),
                  part(pe, 0), part(pe, 1), part(w1, 0), part(w1, 1), part(w2, 0), part(w2, 1),
                  full(g_kc), full(cos_c), full(sin_c)],
        out_specs=(pl.BlockSpec((None, None, n_rows, HEAD_DIM), lambda b, g: (b, g, 0, 0)),
                   pl.BlockSpec((None, None, n_rows, HEAD_DIM), lambda b, g: (b, g, 0, 0))),
        compiler_params=_cparams(("arbitrary", "arbitrary")),
        name="nsa_compress",
    )(z, z, pe, pe, w1, w1, w2, w2, g_kc, cos_c, sin_c)


def _softmax_rows(s):
    m = jnp.max(s, axis=1, keepdims=True)
    e = jnp.exp(s - m)
    return e / jnp.maximum(jnp.sum(e, axis=1, keepdims=True), 1e-30)


def _flash_update_t(carry, s, vt):
    m, l, acc = carry
    m_new = jnp.maximum(m, jnp.max(s, axis=0, keepdims=True))
    alpha = jnp.exp2(m - m_new)
    p = jnp.exp2(s - m_new)
    l = alpha * l + jnp.sum(p, axis=0, keepdims=True)
    acc = alpha * acc + _dot(vt, p.astype(BF16))
    return m_new, l, acc


def _flash_init_t(tq, width):
    return (jnp.full((1, tq), NEG, F32), jnp.zeros((1, tq), F32), jnp.zeros((width, tq), F32))


def _causal_keep_t(tk, tq):
    return lax.broadcasted_iota(jnp.int32, (tk, tq), 0) <= lax.broadcasted_iota(jnp.int32, (tk, tq), 1)


Q_CHUNK = 512


def _flash_streams(carries_list, k_tiles, chunk_lists, vt, masks):
    scores = []
    for k_tile, q_chunks in zip(k_tiles, chunk_lists):
        row = []
        for qc, mk in zip(q_chunks, masks):
            if mk is False:
                row.append(None)
                continue
            s = _dot_nt(k_tile, qc)
            row.append(s if mk is None else jnp.where(mk, s, NEG))
        scores.append(row)
    return [tuple(c if s is None else _flash_update_t(c, s, vt) for c, s in zip(carries, row))
            for carries, row in zip(carries_list, scores)]


def _causal_tail(carries_list, q0, n_ch, load_kv, chunk_lists):
    keep = _causal_keep_t(Q_CHUNK, Q_CHUNK)
    for kb in range(n_ch):
        k0 = q0 + kb * Q_CHUNK
        k_tiles, vt = load_kv(k0 if isinstance(k0, int) else pl.multiple_of(k0, Q_CHUNK), Q_CHUNK)
        masks = [False if kb > c else (keep if kb == c else None) for c in range(n_ch)]
        carries_list = _flash_streams(carries_list, k_tiles, chunk_lists, vt, masks)
    return carries_list


def _nsa_attn_kernel(q_ref, kc_ref, vc_ref, ks_ref, kw_ref, vst_ref, vwt_ref, gt_ref,
                     ovt_ref, expt_ref, o_ref, m_ref, l_ref, acc_ref, *, tq, tk, seq):
    R = NSA_REP
    qi = pl.program_id(2)
    q0 = qi * tq
    n_cmp = kc_ref.shape[0]
    n_slc = seq // NSA_SLC_LEN
    q = q_ref[...]
    qs = jnp.concatenate([q[:, r * HEAD_DIM:(r + 1) * HEAD_DIM] for r in range(R)], axis=0)
    qpos1 = q0 + lax.broadcasted_iota(jnp.int32, (1, tq), 1)
    hp = Q_CHUNK // tq
    n_ch = R // hp
    q_chunks = [qs[c * Q_CHUNK:(c + 1) * Q_CHUNK] for c in range(n_ch)]
    qpos = jnp.concatenate([qpos1] * hp, axis=1)

    cmp_end = lax.broadcasted_iota(jnp.int32, (n_cmp, 1), 0) * NSA_CMP_STRIDE + (NSA_CMP_LEN - 1)
    vis = cmp_end <= qpos
    kc = kc_ref[...]
    vct = vc_ref[...].astype(F32).T.astype(BF16)
    p_cs, o_cs = [], []
    for qc in q_chunks:
        s_c = jnp.where(vis, _dot_nt(kc, qc), NEG)
        e_c = jnp.where(vis, jnp.exp2(s_c - jnp.max(s_c, axis=0, keepdims=True)), 0.0)
        p_c = e_c / jnp.maximum(jnp.sum(e_c, axis=0, keepdims=True), 1e-30)
        p_cs.append(p_c)
        o_cs.append(_dot(vct, p_c.astype(BF16)))

    p_sum = None
    for r in range(R):
        part = p_cs[r // hp][:, (r % hp) * tq:(r % hp + 1) * tq]
        p_sum = part if p_sum is None else p_sum + part
    p_hi = p_sum.astype(BF16)
    p_lo = (p_sum - p_hi.astype(F32)).astype(BF16)
    imp = _dot(ovt_ref[...], p_hi) + _dot(ovt_ref[...], p_lo)
    jblk = lax.broadcasted_iota(jnp.int32, (n_slc, tq), 0)
    qblk = qpos1 // NSA_SLC_LEN
    forced = (jblk == 0) | (jblk == qblk) | (jblk == qblk - 1)
    imp = jnp.where(jblk > qblk, -jnp.inf, jnp.where(forced, jnp.inf, imp))
    picked = jnp.zeros((n_slc, tq), jnp.int32)
    for _ in range(min(NSA_SLC_TOPK, n_slc)):
        cand = jnp.where(picked > 0, -jnp.inf, imp)
        best = jnp.max(cand, axis=0, keepdims=True)
        first = jnp.min(jnp.where(cand == best, jblk + picked * n_slc, n_slc), axis=0, keepdims=True)
        picked = jnp.where(jblk == first, 1, picked)
    sel_bias = jnp.where(picked > 0, 0.0, NEG).astype(BF16)

    def slc_tile(j, carries, causal):
        k0 = j * tk if isinstance(j, int) else pl.multiple_of(j * tk, tk)
        ks = ks_ref[pl.ds(k0, tk), :]
        vst = vst_ref[:, pl.ds(k0, tk)]
        bias = _dot(expt_ref[pl.ds(k0, tk), :], sel_bias)
        if causal:
            kpos = k0 + lax.broadcasted_iota(jnp.int32, (tk, 1), 0)
            bias = jnp.where(kpos <= qpos1, bias, NEG)
        bias = jnp.concatenate([bias] * hp, axis=1)
        scores = [_dot_nt(ks, qc) + bias for qc in q_chunks]
        return tuple(_flash_update_t(carry, s, vst) for carry, s in zip(carries, scores))

    last = q0 // tk
    n_tiles = seq // tk
    st_refs = (m_ref, l_ref, acc_ref)
    for c in range(n_ch):
        for ref, val in zip(st_refs, _flash_init_t(Q_CHUNK, HEAD_DIM)):
            ref[c] = val
    for count in range(1, n_tiles):
        @pl.when(last == count)
        def _():
            carries = tuple(_flash_init_t(Q_CHUNK, HEAD_DIM) for _ in range(n_ch))
            for j in range(count):
                carries = slc_tile(j, carries, False)
            for c in range(n_ch):
                for ref, val in zip(st_refs, carries[c]):
                    ref[c] = val
    carries = tuple(tuple(ref[c] for ref in st_refs) for c in range(n_ch))
    o_ss = [acc / jnp.maximum(l, 1e-30) for _, l, acc in slc_tile(last, carries, True)]

    span = NSA_WINDOW + tq
    w0 = pl.multiple_of(jnp.maximum(q0 - NSA_WINDOW, 0), tq)
    rel = qpos - (w0 + lax.broadcasted_iota(jnp.int32, (span, 1), 0))
    kw = kw_ref[pl.ds(w0, span), :]
    vwt = vwt_ref[:, pl.ds(w0, span)]
    o_ws = []
    for qc in q_chunks:
        s_w = jnp.where(rel >= 0, jnp.where(rel < NSA_WINDOW, _dot_nt(kw, qc), NEG), NEG)
        e_w = jnp.exp2(s_w - jnp.max(s_w, axis=0, keepdims=True))
        p_w = e_w / jnp.maximum(jnp.sum(e_w, axis=0, keepdims=True), 1e-30)
        o_ws.append(_dot(vwt, p_w.astype(BF16)))

    gates = jax.nn.sigmoid(gt_ref[...])
    outs = []
    for r in range(R):
        c, cols = r // hp, slice((r % hp) * tq, (r % hp + 1) * tq)
        o_r = (gates[3 * r:3 * r + 1] * o_cs[c][:, cols] + gates[3 * r + 1:3 * r + 2] * o_ss[c][:, cols]
               + gates[3 * r + 2:3 * r + 3] * o_ws[c][:, cols])
        outs.append(o_r.T)
    o_ref[...] = jnp.concatenate(outs, axis=1).astype(BF16)


NSA_GATE_ROWS = 16


def nsa_attention(zb, zt, gt, kc, vc, batch, seq, ks_col, kw_col, vs_row, vw_row):
    tq, tk = 2 * NSA_QBLOCK, 512
    G, R = NSA_GROUPS, NSA_REP
    tk = min(tk, seq)
    assert seq % tk == 0 and tk % tq == 0 and seq >= NSA_WINDOW + tq and 3 * R <= NSA_GATE_ROWS
    assert Q_CHUNK % tq == 0 and R % (Q_CHUNK // tq) == 0
    n_chains = R * tq // Q_CHUNK
    nq = seq // tq
    n_cmp = seq // NSA_CMP_STRIDE
    n_slc = seq // NSA_SLC_LEN
    starts = np.arange(n_cmp) * NSA_CMP_STRIDE
    sb = np.arange(n_slc) * NSA_SLC_LEN
    overlap = np.clip(np.minimum(starts[:, None] + NSA_CMP_LEN, sb[None, :] + NSA_SLC_LEN)
                      - np.maximum(starts[:, None], sb[None, :]), 0, None) / NSA_CMP_LEN
    expand_t = (np.arange(seq)[:, None] // NSA_SLC_LEN == np.arange(n_slc)[None, :]).astype(np.float32)
    kern = functools.partial(_nsa_attn_kernel, tq=tq, tk=tk, seq=seq)
    return pl.pallas_call(
        kern,
        out_shape=jax.ShapeDtypeStruct((batch * seq, G * R * HEAD_DIM), BF16),
        grid=(batch, G, nq),
        in_specs=[pl.BlockSpec((tq, R * HEAD_DIM), lambda b, g, i: (b * nq + i, g)),
                  pl.BlockSpec((None, None, n_cmp, HEAD_DIM), lambda b, g, i: (b, g, 0, 0)),
                  pl.BlockSpec((None, None, n_cmp, HEAD_DIM), lambda b, g, i: (b, g, 0, 0)),
                  pl.BlockSpec((seq, LANES), lambda b, g, i: (b, ks_col + g)),
                  pl.BlockSpec((seq, LANES), lambda b, g, i: (b, kw_col + g)),
                  pl.BlockSpec((None, HEAD_DIM, seq), lambda b, g, i: (b, vs_row + g, 0)),
                  pl.BlockSpec((None, HEAD_DIM, seq), lambda b, g, i: (b, vw_row + g, 0)),
                  pl.BlockSpec((None, NSA_GATE_ROWS, tq), lambda b, g, i: (b, g, i)),
                  pl.BlockSpec((n_slc, n_cmp), lambda b, g, i: (0, 0)),
                  pl.BlockSpec((seq, n_slc), lambda b, g, i: (0, 0))],
        out_specs=pl.BlockSpec((tq, R * HEAD_DIM), lambda b, g, i: (b * nq + i, g)),
        scratch_shapes=[pltpu.VMEM((n_chains, 1, Q_CHUNK), F32), pltpu.VMEM((n_chains, 1, Q_CHUNK), F32),
                        pltpu.VMEM((n_chains, HEAD_DIM, Q_CHUNK), F32)],
        compiler_params=_cparams(("arbitrary", "arbitrary", "arbitrary")),
        name="nsa_attention",
    )(zb, kc, vc, zb, zb, zt, zt, gt, jnp.asarray(overlap.T, BF16), jnp.asarray(expand_t, BF16))


def _diff_attn_kernel(q_ref, k_ref, vt_ref, lam_ref, subg_ref, o_ref, *, tq, tk, nq, lam_init):
    qi = pl.program_id(2)
    n_ch = tq // Q_CHUNK
    q = q_ref[...]
    q1 = [q[c * Q_CHUNK:(c + 1) * Q_CHUNK, :HEAD_DIM] for c in range(n_ch)]
    q2 = [q[c * Q_CHUNK:(c + 1) * Q_CHUNK, HEAD_DIM:] for c in range(n_ch)]

    def load_kv(k0, size):
        k = k_ref[pl.ds(k0, size), :]
        return [k[:, :HEAD_DIM], k[:, HEAD_DIM:]], vt_ref[:, pl.ds(k0, size)]

    width = vt_ref.shape[0]
    lp = lam_ref[...]
    lam = (jnp.exp(jnp.sum(lp[0:1] * lp[1:2], axis=1, keepdims=True))
           - jnp.exp(jnp.sum(lp[2:3] * lp[3:4], axis=1, keepdims=True)) + lam_init)

    for qv in range(nq):
        @pl.when(qi == qv)
        def _():
            carries = [tuple(_flash_init_t(Q_CHUNK, width) for _ in range(n_ch)) for _ in range(2)]
            for j in range(qv * (tq // tk)):
                k_tiles, vt = load_kv(j * tk, tk)
                carries = _flash_streams(carries, k_tiles, [q1, q2], vt, [None] * n_ch)
            c1, c2 = _causal_tail(carries, qv * tq, n_ch, load_kv, [q1, q2])
            o = jnp.concatenate([(a1 / jnp.maximum(l1, 1e-30) - lam * (a2 / jnp.maximum(l2, 1e-30))).T
                                 for (_, l1, a1), (_, l2, a2) in zip(c1, c2)], axis=0)
            o_ref[...] = (_rms(o, subg_ref[...]) * (1.0 - lam_init)).astype(BF16)


def diff_attention(zb, zt, batch, seq, q_col, k_col, vt_row, diff_lam, sub_g, layer, *, tq=1024, tk=512):
    H = DIFF_HEADS
    W2 = 2 * HEAD_DIM
    tq = min(tq, seq)
    tk = min(tk, tq)
    assert seq % tq == 0 and tq % tk == 0 and tq % Q_CHUNK == 0
    nq = seq // tq
    lam_init = 0.8 - 0.6 * math.exp(-0.3 * layer)
    kern = functools.partial(_diff_attn_kernel, tq=tq, tk=tk, nq=nq, lam_init=lam_init)
    return pl.pallas_call(
        kern,
        out_shape=jax.ShapeDtypeStruct((batch * seq, H * W2), BF16),
        grid=(batch, H, nq),
        in_specs=[pl.BlockSpec((tq, W2), lambda b, h, i: (b * nq + i, q_col + h)),
                  pl.BlockSpec((seq, W2), lambda b, h, i: (b, k_col + h)),
                  pl.BlockSpec((None, W2, seq), lambda b, h, i: (b, vt_row + h, 0)),
                  pl.BlockSpec((4, HEAD_DIM), lambda b, h, i: (0, 0)),
                  pl.BlockSpec((1, W2), lambda b, h, i: (0, 0))],
        out_specs=pl.BlockSpec((tq, W2), lambda b, h, i: (b * nq + i, h)),
        compiler_params=_cparams(("arbitrary", "arbitrary", "arbitrary")),
        name="diff_attention",
    )(zb, zb, zt, diff_lam, sub_g.reshape(1, W2))


FOX_AUG = 6


def _fox_attn_kernel(q_ref, qaug_ref, k_ref, kaug_ref, vt_ref, o_ref, *, tq, tk, nq):
    h = pl.program_id(1)
    qi = pl.program_id(2)
    n_ch = tq // Q_CHUNK
    lane = lax.broadcasted_iota(jnp.int32, (tq, LANES), 1)
    mine = (lane >= h * FOX_AUG) & (lane < (h + 1) * FOX_AUG)
    q2 = jnp.concatenate([q_ref[...], jnp.where(mine, qaug_ref[...], jnp.zeros_like(qaug_ref))], axis=1)
    q_chunks = [q2[c * Q_CHUNK:(c + 1) * Q_CHUNK] for c in range(n_ch)]

    def load_kv(k0, size):
        k2 = jnp.concatenate([k_ref[pl.ds(k0, size), :], kaug_ref[pl.ds(k0, size), :]], axis=1)
        return [k2], vt_ref[:, pl.ds(k0, size)]

    for qv in range(nq):
        @pl.when(qi == qv)
        def _():
            carries = tuple(_flash_init_t(Q_CHUNK, HEAD_DIM) for _ in range(n_ch))
            for j in range(qv * (tq // tk)):
                k_tiles, vt = load_kv(j * tk, tk)
                (carries,) = _flash_streams([carries], k_tiles, [q_chunks], vt, [None] * n_ch)
            (carries,) = _causal_tail([carries], qv * tq, n_ch, load_kv, [q_chunks])
            o_ref[...] = jnp.concatenate([(acc / jnp.maximum(l, 1e-30)).T for _, l, acc in carries],
                                         axis=0).astype(BF16)


def fox_attention(zb, zt, qaug, kaug, batch, seq, *, tq=2048, tk=512):
    H = FOX_HEADS
    assert H * FOX_AUG <= LANES
    tq = min(tq, seq)
    tk = min(tk, tq)
    assert seq % tq == 0 and tq % tk == 0 and tq % Q_CHUNK == 0
    nq = seq // tq
    kern = functools.partial(_fox_attn_kernel, tq=tq, tk=tk, nq=nq)
    return pl.pallas_call(
        kern,
        out_shape=jax.ShapeDtypeStruct((batch * seq, H * HEAD_DIM), BF16),
        grid=(batch, H, nq),
        in_specs=[pl.BlockSpec((tq, LANES), lambda b, h, i: (b * nq + i, h)),
                  pl.BlockSpec((tq, LANES), lambda b, h, i: (b * nq + i, 0)),
                  pl.BlockSpec((seq, LANES), lambda b, h, i: (b, H + h)),
                  pl.BlockSpec((seq, LANES), lambda b, h, i: (b, 0)),
                  pl.BlockSpec((None, HEAD_DIM, seq), lambda b, h, i: (b, h, 0))],
        out_specs=pl.BlockSpec((tq, LANES), lambda b, h, i: (b * nq + i, h)),
        compiler_params=_cparams(("arbitrary", "arbitrary", "arbitrary")),
        name="fox_attention",
    )(zb, qaug, zb, kaug, zt)


def _split3(x):
    p1 = x.astype(BF16)
    r1 = x - p1.astype(F32)
    p2 = r1.astype(BF16)
    p3 = (r1 - p2.astype(F32)).astype(BF16)
    return p1, p2, p3


def _fox_gate_kernel(z_ref, b_ref, place_ref, pat_ref, qaug_ref, kaug_ref, carry_ref, *, tr):
    @pl.when(pl.program_id(1) == 0)
    def _():
        carry_ref[...] = jnp.zeros_like(carry_ref)

    logf = jax.nn.log_sigmoid(z_ref[...] + b_ref[...])
    tri = (lax.broadcasted_iota(jnp.int32, (tr, tr), 1) <= lax.broadcasted_iota(jnp.int32, (tr, tr), 0)).astype(BF16)
    p1, p2, p3 = _split3(logf)
    c = _dot(tri, p1) + _dot(tri, p2) + _dot(tri, p3) + carry_ref[...]
    carry_ref[...] = c[tr - 1:tr, :]
    e1, e2, e3 = _split3(c * LOG2E)
    place = place_ref[...]
    cexp = _dot(e1, place) + _dot(e2, place) + _dot(e3, place)
    h1, h2, h3 = _split3(cexp)
    pat = pat_ref[...]
    piece = h1.astype(F32) * pat[0:1] + h2.astype(F32) * pat[1:2] + h3.astype(F32) * pat[2:3]
    kaug_ref[...] = (pat[4:5] - piece * pat[3:4]).astype(BF16)
    qaug_ref[...] = (pat[3:4] + piece * pat[4:5]).astype(BF16)


def fox_gate_bias(z, col_block, bias_row, batch, seq, *, tr=256):
    H = FOX_HEADS
    tr = min(tr, seq)
    nt = seq // tr
    lanes = np.arange(LANES)
    used = lanes < H * FOX_AUG
    place = (lanes[None, :] // FOX_AUG == np.arange(LANES)[:, None]) & used[None, :] & (np.arange(LANES)[:, None] < H)
    j = lanes % FOX_AUG
    pat = np.zeros((8, LANES), np.float32)
    for r in range(3):
        pat[r] = used & (j % 3 == r)
    pat[3] = used & (j < 3)
    pat[4] = used & (j >= 3)
    kern = functools.partial(_fox_gate_kernel, tr=tr)
    out = jax.ShapeDtypeStruct((batch * seq, LANES), BF16)
    return pl.pallas_call(
        kern,
        out_shape=(out, out),
        grid=(batch, nt),
        in_specs=[pl.BlockSpec((tr, LANES), lambda b, i: (b * nt + i, col_block)),
                  pl.BlockSpec((1, LANES), lambda b, i: (0, 0)),
                  pl.BlockSpec((LANES, LANES), lambda b, i: (0, 0)),
                  pl.BlockSpec((8, LANES), lambda b, i: (0, 0))],
        out_specs=(pl.BlockSpec((tr, LANES), lambda b, i: (b * nt + i, 0)),
                   pl.BlockSpec((tr, LANES), lambda b, i: (b * nt + i, 0))),
        scratch_shapes=[pltpu.VMEM((1, LANES), F32)],
        compiler_params=_cparams(("arbitrary", "arbitrary")),
        name="fox_gate_bias",
    )(z, bias_row, jnp.asarray(place, BF16), jnp.asarray(pat))


def _cross_attn_kernel(x_ref, g_ref, wq_ref, kv_ref, qg_ref, kg_ref, wo_ref, gr_ref, wr_ref, br_ref,
                       o_ref, hn_ref, route_ref, onehot_ref):
    x = x_ref[...]
    q = _dot(_rms(x, g_ref[...]).astype(BF16), wq_ref[...])
    kv = kv_ref[...]
    width = MEM_HEADS * HEAD_DIM
    outs = []
    for h in range(MEM_HEADS):
        cols = slice(h * HEAD_DIM, (h + 1) * HEAD_DIM)
        qh = _rms(q[:, cols], qg_ref[...]).astype(BF16)
        kh = _rms(kv[:, cols], kg_ref[...]).astype(BF16)
        vh = kv[:, width + h * HEAD_DIM: width + (h + 1) * HEAD_DIM].astype(BF16)
        p = _softmax_rows(_dot_nt(qh, kh))
        outs.append(_dot(p.astype(BF16), vh))
    o = jnp.concatenate(outs, axis=1).astype(BF16)
    x_new = x + _dot(o, wo_ref[...])
    o_ref[...] = x_new
    hn_ref[...], route_ref[...], onehot_ref[...] = _route_tokens(x_new, gr_ref[...], wr_ref[...], br_ref[...])


def cross_attention(x, g, wq, kv, q_gain_scaled, k_gain, wo, g_moe, w_router, b_router, seq, mem_len, *, tm=512):
    M, D = x.shape
    tm = min(tm, seq)
    nt = seq // tm
    width = MEM_HEADS * HEAD_DIM
    const = lambda a: pl.BlockSpec(a.shape, lambda i: (0, 0))
    g, g_moe = g.reshape(1, D), g_moe.reshape(1, D)
    qg, kg = q_gain_scaled.reshape(1, HEAD_DIM), k_gain.reshape(1, HEAD_DIM)
    return pl.pallas_call(
        _cross_attn_kernel,
        out_shape=(jax.ShapeDtypeStruct((M, D), F32),
                   jax.ShapeDtypeStruct((M, D // 2), jnp.uint32),
                   jax.ShapeDtypeStruct((M, LANES), F32),
                   jax.ShapeDtypeStruct((M, LANES), BF16)),
        grid=(M // tm,),
        in_specs=[pl.BlockSpec((tm, D), lambda i: (i, 0)), const(g), const(wq),
                  pl.BlockSpec((mem_len, 2 * width), lambda i: (i // nt, 0)),
                  const(qg), const(kg), const(wo), const(g_moe), const(w_router), const(b_router)],
        out_specs=(pl.BlockSpec((tm, D), lambda i: (i, 0)),
                   pl.BlockSpec((tm, D // 2), lambda i: (i, 0)),
                   pl.BlockSpec((tm, LANES), lambda i: (i, 0)),
                   pl.BlockSpec((tm, LANES), lambda i: (i, 0))),
        compiler_params=_cparams(("arbitrary",)),
        name="cross_attention",
    )(x, g, wq, kv, qg, kg, wo, g_moe, w_router, b_router)


def _pack_bf16_pairs(x_bf16):
    n = x_bf16.shape[1] // 2
    lo = pltpu.bitcast(x_bf16[:, :n].astype(F32), jnp.uint32)
    hi = pltpu.bitcast(x_bf16[:, n:].astype(F32), jnp.uint32)
    return (hi & jnp.uint32(0xFFFF0000)) | (lo >> jnp.uint32(16))


def _unpack_bf16_pairs(words):
    lo = pltpu.bitcast(words << jnp.uint32(16), F32).astype(BF16)
    hi = pltpu.bitcast(words & jnp.uint32(0xFFFF0000), F32).astype(BF16)
    return jnp.concatenate([lo, hi], axis=1)


def _route_tokens(x, g, w, b):
    hn = _rms(x, g)
    h_hi = hn.astype(BF16)
    packed = _pack_bf16_pairs(h_hi)
    w_hi = w.astype(BF16)
    w_lo = (w - w_hi.astype(F32)).astype(BF16)
    h_lo = (hn - h_hi.astype(F32)).astype(BF16)
    logits = _dot(h_hi, w_hi) + _dot(h_lo, w_hi) + _dot(h_hi, w_lo) + b
    tm = logits.shape[0]
    lane = lax.broadcasted_iota(jnp.int32, (tm, LANES), 1)
    is_grp = lane < N_GROUPS
    lg = jnp.where(is_grp, logits, -jnp.inf)
    eg = jnp.exp(lg - jnp.max(lg, axis=1, keepdims=True))
    p_grp = eg / jnp.sum(eg, axis=1, keepdims=True)
    p_top = jnp.max(p_grp, axis=1, keepdims=True)
    grp = jnp.min(jnp.where(is_grp & (p_grp == p_top), lane, LANES), axis=1, keepdims=True)
    lo = N_GROUPS + grp * EXPERTS_PER_GROUP
    in_grp = (lane >= lo) & (lane < lo + EXPERTS_PER_GROUP)
    le = jnp.where(in_grp, logits, -jnp.inf)
    ee = jnp.exp(le - jnp.max(le, axis=1, keepdims=True))
    p_in = ee / jnp.sum(ee, axis=1, keepdims=True)
    m1 = jnp.max(p_in, axis=1, keepdims=True)
    i1 = jnp.min(jnp.where(in_grp & (p_in == m1), lane, LANES), axis=1, keepdims=True)
    rest = jnp.where(in_grp & (lane != i1), p_in, -jnp.inf)
    m2 = jnp.max(rest, axis=1, keepdims=True)
    i2 = jnp.min(jnp.where(rest == m2, lane, LANES), axis=1, keepdims=True)
    denom = m1 + m2
    w1 = p_top * m1 / denom
    w2 = p_top * m2 / denom
    e1 = (i1 - N_GROUPS).astype(F32)
    e2 = (i2 - N_GROUPS).astype(F32)
    route = jnp.where(lane == 0, e1, jnp.where(lane == 1, e2, jnp.where(lane == 2, w1, jnp.where(lane == 3, w2, 0.0))))
    onehot = ((lane == i1 - N_GROUPS) | (lane == i2 - N_GROUPS)).astype(BF16)
    return packed, route, onehot


def _moe_rank_kernel(onehot_ref, rank_ref, count_ref, carry_ref, *, tr):
    i = pl.program_id(0)

    @pl.when(i == 0)
    def _():
        carry_ref[...] = jnp.zeros_like(carry_ref)

    oh = onehot_ref[...]
    tri = (lax.broadcasted_iota(jnp.int32, (tr, tr), 1) < lax.broadcasted_iota(jnp.int32, (tr, tr), 0)).astype(BF16)
    rank_ref[...] = _dot(tri, oh) + carry_ref[...]
    carry_ref[...] = carry_ref[...] + jnp.sum(oh.astype(F32), axis=0, keepdims=True)
    count_ref[...] = carry_ref[...]


def moe_rank(onehot, *, tr=512):
    M = onehot.shape[0]
    tr = min(tr, M)
    kern = functools.partial(_moe_rank_kernel, tr=tr)
    return pl.pallas_call(
        kern,
        out_shape=(jax.ShapeDtypeStruct((M, LANES), F32), jax.ShapeDtypeStruct((1, LANES), F32)),
        grid=(M // tr,),
        in_specs=[pl.BlockSpec((tr, LANES), lambda i: (i, 0))],
        out_specs=(pl.BlockSpec((tr, LANES), lambda i: (i, 0)), pl.BlockSpec((1, LANES), lambda i: (0, 0))),
        scratch_shapes=[pltpu.VMEM((1, LANES), F32)],
        compiler_params=_cparams(("arbitrary",)),
        name="moe_rank",
    )(onehot)


def _moe_slot_kernel(route_ref, rank_ref, pstart_ref, o_ref):
    route = route_ref[...]
    slot = rank_ref[...] + pstart_ref[...]
    lane = lax.broadcasted_iota(jnp.int32, route.shape, 1)
    e1 = route[:, 0:1].astype(jnp.int32)
    e2 = route[:, 1:2].astype(jnp.int32)
    d1 = jnp.sum(jnp.where(lane == e1, slot, 0.0), axis=1, keepdims=True)
    d2 = jnp.sum(jnp.where(lane == e2, slot, 0.0), axis=1, keepdims=True)
    o_ref[...] = jnp.where(lane == 0, d1, jnp.where(lane == 1, d2, 0.0)).astype(jnp.int32)


def moe_slots(route, rank, pstart_row, *, tm=1024):
    M = route.shape[0]
    tm = min(tm, M)
    return pl.pallas_call(
        _moe_slot_kernel,
        out_shape=jax.ShapeDtypeStruct((M, LANES), jnp.int32),
        grid=(M // tm,),
        in_specs=[pl.BlockSpec((tm, LANES), lambda i: (i, 0)),
                  pl.BlockSpec((tm, LANES), lambda i: (i, 0)),
                  pl.BlockSpec((1, LANES), lambda i: (0, 0))],
        out_specs=pl.BlockSpec((tm, LANES), lambda i: (i, 0)),
        compiler_params=_cparams(("arbitrary",)),
        name="moe_slots",
    )(route, rank, pstart_row)


def _moe_dispatch_kernel(lastblk_ref, d1_hbm, d2_hbm, hn_ref, xb_hbm, zero_ref, s1_ref, s2_ref,
                         idx_sem, row_sem, zero_sem, *, tm, n_blocks):
    i = pl.program_id(0)

    def zero_copy(blk):
        return pltpu.make_async_copy(zero_ref, xb_hbm.at[pl.ds(blk * MOE_ROWS, MOE_ROWS)], zero_sem)

    @pl.when(i == 0)
    def _():
        zero_ref[...] = jnp.zeros_like(zero_ref)
        n_used = lastblk_ref[N_EXPERTS]
        for e in range(N_EXPERTS):
            @pl.when(lastblk_ref[e] >= 0)
            def _():
                zero_copy(lastblk_ref[e]).start()

        def start_tail(blk, _):
            zero_copy(blk).start()
            return 0

        def wait_tail(blk, _):
            zero_copy(blk).wait()
            return 0

        lax.fori_loop(n_used, n_blocks, start_tail, 0)
        for e in range(N_EXPERTS):
            @pl.when(lastblk_ref[e] >= 0)
            def _():
                zero_copy(lastblk_ref[e]).wait()
        lax.fori_loop(n_used, n_blocks, wait_tail, 0)

    c1 = pltpu.make_async_copy(d1_hbm.at[i], s1_ref, idx_sem.at[0])
    c2 = pltpu.make_async_copy(d2_hbm.at[i], s2_ref, idx_sem.at[1])
    c1.start()
    c2.start()
    c1.wait()
    c2.wait()

    def row_copy(r, slot):
        return pltpu.make_async_copy(hn_ref.at[pl.ds(r, 1)], xb_hbm.at[pl.ds(slot, 1)], row_sem)

    def issue(blk, _):
        for u in range(ROW_DMA_UNROLL):
            r = blk * ROW_DMA_UNROLL + u
            row_copy(r, s1_ref[r]).start(priority=0)
            row_copy(r, s2_ref[r]).start(priority=1)
        return 0

    lax.fori_loop(0, tm // ROW_DMA_UNROLL, issue, 0)

    def drain(r, _):
        row_copy(r, 0).wait()
        row_copy(r, 0).wait()
        return 0

    lax.fori_loop(0, tm, drain, 0, unroll=8)


def moe_dispatch(hn, d1, d2, lastblk, n_slots, *, tm=1024):
    M, D = hn.shape
    assert hn.dtype.itemsize == 4
    tm = min(tm, M)
    assert n_slots % MOE_ROWS == 0
    kern = functools.partial(_moe_dispatch_kernel, tm=tm, n_blocks=n_slots // MOE_ROWS)
    grid_spec = pltpu.PrefetchScalarGridSpec(
        num_scalar_prefetch=1,
        grid=(M // tm,),
        in_specs=[pl.BlockSpec(memory_space=pl.ANY), pl.BlockSpec(memory_space=pl.ANY),
                  pl.BlockSpec((tm, D), lambda i, *_: (i, 0))],
        out_specs=pl.BlockSpec(memory_space=pl.ANY),
        scratch_shapes=[pltpu.VMEM((MOE_ROWS, D), hn.dtype),
                        pltpu.SMEM((tm,), jnp.int32), pltpu.SMEM((tm,), jnp.int32),
                        pltpu.SemaphoreType.DMA((2,)), pltpu.SemaphoreType.DMA, pltpu.SemaphoreType.DMA],
    )
    return pl.pallas_call(
        kern,
        out_shape=jax.ShapeDtypeStruct((n_slots, D), hn.dtype),
        grid_spec=grid_spec,
        compiler_params=_cparams(("arbitrary",)),
        name="moe_dispatch",
    )(lastblk, d1.reshape(M // tm, tm), d2.reshape(M // tm, tm), hn)


def _moe_expert_kernel(blk_e_ref, n_used_ref, x_ref, w1_ref, w3_ref, w2_ref, o_ref, w1b, w3b, w2b):
    i = pl.program_id(0)

    @pl.when(i < n_used_ref[0])
    def _():
        prev = blk_e_ref[jnp.maximum(i - 1, 0)]

        @pl.when((i == 0) | (blk_e_ref[i] != prev))
        def _():
            w1b[...] = w1_ref[...].astype(BF16)
            w3b[...] = w3_ref[...].astype(BF16)
            w2b[...] = w2_ref[...].astype(BF16)

        x = _unpack_bf16_pairs(x_ref[...])
        h = (jax.nn.silu(_dot(x, w1b[...])) * _dot(x, w3b[...])).astype(BF16)
        o_ref[...] = _dot(h, w2b[...])

    @pl.when(i >= n_used_ref[0])
    def _():
        o_ref[...] = jnp.zeros_like(o_ref)


def moe_experts(xb, blk_e, n_used, w1, w3, w2, layer):
    P = xb.shape[0]
    D = w1.shape[-2]
    FF = w1.shape[-1]
    n_blk = P // MOE_ROWS

    def row_map(i, blk_e_ref, n_used_ref):
        return (jnp.minimum(i, n_used_ref[0] - 1), 0)

    def out_map(i, blk_e_ref, n_used_ref):
        return (i, 0)

    def w_map(i, blk_e_ref, n_used_ref):
        return (layer, blk_e_ref[i], 0, 0)

    grid_spec = pltpu.PrefetchScalarGridSpec(
        num_scalar_prefetch=2,
        grid=(n_blk,),
        in_specs=[pl.BlockSpec((MOE_ROWS, D // 2), row_map),
                  pl.BlockSpec((None, None, D, FF), w_map),
                  pl.BlockSpec((None, None, D, FF), w_map),
                  pl.BlockSpec((None, None, FF, D), w_map)],
        out_specs=pl.BlockSpec((MOE_ROWS, D), out_map),
        scratch_shapes=[pltpu.VMEM((D, FF), BF16), pltpu.VMEM((D, FF), BF16), pltpu.VMEM((FF, D), BF16)],
    )
    return pl.pallas_call(
        _moe_expert_kernel,
        out_shape=jax.ShapeDtypeStruct((P, D), F32),
        grid_spec=grid_spec,
        compiler_params=_cparams(("arbitrary",)),
        name="moe_experts",
    )(blk_e, n_used, xb, w1, w3, w2)


def _moe_combine_kernel(x_ref, route_ref, d1_hbm, d2_hbm, yb_hbm, o_ref, y1_ref, y2_ref, s1_ref, s2_ref,
                        idx_sem, row_sem, *, tm):
    i = pl.program_id(0)
    c1 = pltpu.make_async_copy(d1_hbm.at[i], s1_ref, idx_sem.at[0])
    c2 = pltpu.make_async_copy(d2_hbm.at[i], s2_ref, idx_sem.at[1])
    c1.start()
    c2.start()
    c1.wait()
    c2.wait()

    def row_copy(slot, dst, r):
        return pltpu.make_async_copy(yb_hbm.at[pl.ds(slot, 1)], dst.at[pl.ds(r, 1)], row_sem)

    def issue(blk, _):
        for u in range(ROW_DMA_UNROLL):
            r = blk * ROW_DMA_UNROLL + u
            row_copy(s1_ref[r], y1_ref, r).start(priority=0)
            row_copy(s2_ref[r], y2_ref, r).start(priority=1)
        return 0

    lax.fori_loop(0, tm // ROW_DMA_UNROLL, issue, 0)

    def drain(r, _):
        row_copy(0, y1_ref, r).wait()
        row_copy(0, y2_ref, r).wait()
        return 0

    lax.fori_loop(0, tm, drain, 0, unroll=8)
    route = route_ref[...]
    o_ref[...] = x_ref[...] + (y1_ref[...] * route[:, 2:3] + y2_ref[...] * route[:, 3:4])


def moe_combine(x, route, d1, d2, yb, *, tm=512):
    M, D = x.shape
    tm = min(tm, M)
    kern = functools.partial(_moe_combine_kernel, tm=tm)
    return pl.pallas_call(
        kern,
        out_shape=jax.ShapeDtypeStruct((M, D), F32),
        grid=(M // tm,),
        in_specs=[pl.BlockSpec((tm, D), lambda i: (i, 0)),
                  pl.BlockSpec((tm, LANES), lambda i: (i, 0)),
                  pl.BlockSpec(memory_space=pl.ANY),
                  pl.BlockSpec(memory_space=pl.ANY),
                  pl.BlockSpec(memory_space=pl.ANY)],
        out_specs=pl.BlockSpec((tm, D), lambda i: (i, 0)),
        scratch_shapes=[pltpu.VMEM((tm, D), F32), pltpu.VMEM((tm, D), F32),
                        pltpu.SMEM((tm,), jnp.int32), pltpu.SMEM((tm,), jnp.int32),
                        pltpu.SemaphoreType.DMA((2,)), pltpu.SemaphoreType.DMA],
        compiler_params=_cparams(("arbitrary",)),
        name="moe_combine",
    )(x, route, d1.reshape(M // tm, tm), d2.reshape(M // tm, tm), yb)


def router_params(wg, bg, we, be):
    n_route = N_GROUPS + N_EXPERTS
    w_router = jnp.pad(jnp.concatenate([wg, we], axis=1), ((0, 0), (0, LANES - n_route)))
    b_router = jnp.pad(jnp.concatenate([bg, be]), (0, LANES - n_route)).reshape(1, LANES)
    return w_router, b_router


def hier_moe(x, hn, route, onehot, w1, w3, w2, layer):
    M, D = x.shape
    rank, counts = moe_rank(onehot)
    cnt = counts[0, :N_EXPERTS].astype(jnp.int32)
    nblk = (cnt + MOE_ROWS - 1) // MOE_ROWS
    bend = jnp.cumsum(nblk)
    bstart = bend - nblk
    n_blk_max = (2 * M) // MOE_ROWS + N_EXPERTS
    pstart_row = jnp.pad((bstart * MOE_ROWS).astype(F32), (0, LANES - N_EXPERTS)).reshape(1, LANES)
    slots = moe_slots(route, rank, pstart_row)
    d1, d2 = slots[:, 0], slots[:, 1]
    n_used = bend[-1:].astype(jnp.int32)
    lastblk = jnp.concatenate([jnp.where(nblk > 0, bend - 1, -1).astype(jnp.int32), n_used])
    blk_ids = jnp.arange(n_blk_max, dtype=jnp.int32)
    blk_e = jnp.minimum(jnp.sum(bend[None, :] <= blk_ids[:, None], axis=1), N_EXPERTS - 1).astype(jnp.int32)
    xb = moe_dispatch(hn, d1, d2, lastblk, n_blk_max * MOE_ROWS)
    yb = moe_experts(xb, blk_e, n_used, w1, w3, w2, layer)
    return moe_combine(x, route, d1, d2, yb)


def _even_mixer(x, g, w_in, w_out, nsa_qk_g, cmp_pe, cmp_w1, cmp_w2, diff_qk_g, diff_lam, diff_sub_g,
                layer, batch, seq, cos, sin):
    M, D = x.shape
    scale = HEAD_DIM ** -0.5
    nq_w = NSA_HEADS * HEAD_DIM
    kv_w = 6 * NSA_GROUPS * HEAD_DIM
    gate_w = 3 * NSA_HEADS
    dq_w = 2 * DIFF_HEADS * HEAD_DIM
    dv_w = DIFF_HEADS * 2 * HEAD_DIM
    c_gate = nq_w + kv_w
    c_diff = c_gate + gate_w
    main_w = nq_w + kv_w + 2 * dq_w + dv_w
    G, R3 = NSA_GROUPS, 3 * NSA_REP
    gate_cols = []
    for grp in range(G):
        gate_cols += [w_in[:, c_gate + grp * R3:c_gate + (grp + 1) * R3], jnp.zeros((D, NSA_GATE_ROWS - R3), w_in.dtype)]
    gate_cols.append(jnp.zeros((D, LANES - G * NSA_GATE_ROWS), w_in.dtype))
    w_cat = jnp.concatenate([w_in[:, :c_gate], w_in[:, c_diff:]] + gate_cols, axis=1).astype(BF16)
    b_kv, b_dq, b_dk, b_dv, b_gate = nq_w // LANES, (nq_w + kv_w) // LANES, (nq_w + kv_w + dq_w) // LANES, \
        (nq_w + kv_w + 2 * dq_w) // LANES, main_w // LANES
    qs = scale * LOG2E
    gains = jnp.stack([nsa_qk_g[0] * qs, nsa_qk_g[2], nsa_qk_g[3], diff_qk_g[0] * qs, diff_qk_g[1]])
    modes = ([("row", 0, True)] * (b_kv - 0)
             + [("raw", None, False)] * (2 * G)
             + [("row", 1, True)] * G + [("t", None, False)] * G
             + [("row", 2, True)] * G + [("t", None, False)] * G
             + [("row", 3, True)] * (b_dk - b_dq) + [("row", 4, True)] * (b_dv - b_dk)
             + [("t", None, False)] * (b_gate - b_dv)
             + [("t32", None, False)])
    proj = head_projection(x, g, w_cat, modes, gains, cos, sin, batch, seq, name="even_in_proj")
    zb, zt, gt, zc = proj["row"], proj["t"], proj["t32"], proj["raw"]
    r_ks = b_kv
    r_kw, r_dq = r_ks + G, r_ks + 2 * G
    r_dk = r_dq + (b_dk - b_dq)
    cmp_end = np.arange(seq // NSA_CMP_STRIDE) * NSA_CMP_STRIDE + NSA_CMP_LEN - 1
    cos_c, sin_c = rope_tables(cmp_end)
    kc, vc = nsa_compress(zc, 0, batch, seq, cmp_pe, cmp_w1.astype(BF16), cmp_w2.astype(BF16), nsa_qk_g[1],
                          cos_c, sin_c)
    o_nsa = nsa_attention(zb, zt, gt, kc, vc, batch, seq, r_ks, r_kw, 0, G)
    o_diff = diff_attention(zb, zt, batch, seq, r_dq // 2, r_dk // 2, (2 * G) // 2, diff_lam, diff_sub_g, layer)
    return matmul_residual([o_nsa, o_diff], w_out.astype(BF16), x, name="even_out_proj")


def _odd_mixer(x, g, w_in, w_out, f_b, qk_g, batch, seq, cos, sin):
    M, D = x.shape
    H = FOX_HEADS
    scale = HEAD_DIM ** -0.5
    width = H * HEAD_DIM
    w_cat = jnp.pad(w_in, ((0, 0), (0, LANES - H))).astype(BF16)
    modes = [("row", 0, False)] * H + [("row", 1, False)] * H + [("t", None, False)] * H + [("raw", None, False)]
    gains = jnp.stack([qk_g[0] * (scale * LOG2E), qk_g[1]])
    proj = head_projection(x, g, w_cat, modes, gains, cos, sin, batch, seq, name="odd_in_proj")
    bias_row = jnp.pad(f_b, (0, LANES - H)).reshape(1, LANES)
    qaug, kaug = fox_gate_bias(proj["raw"], 0, bias_row, batch, seq)
    o = fox_attention(proj["row"], proj["t"], qaug, kaug, batch, seq)
    return matmul_residual([o], w_out.astype(BF16), x, name="odd_out_proj")


def kernel(x, mem, norm_g, ev_w_in, ev_w_out, nsa_qk_g, nsa_cmp_pe, nsa_cmp_w1, nsa_cmp_w2, diff_qk_g, diff_lam, diff_sub_g, od_w_in, od_w_out, fox_f_b, fox_qk_g, ca_wq, ca_wkv, ca_qk_g, ca_wo, moe_wg, moe_bg, moe_we, moe_be, moe_w1, moe_w3, moe_w2):
    B, T, D = x.shape
    mem_len = mem.shape[1]
    depth = norm_g.shape[0]
    scale = HEAD_DIM ** -0.5
    cos, sin = rope_tables(np.arange(T))
    xt = x.reshape(B * T, D)
    mt = mem.reshape(B * mem_len, D)
    for layer in range(depth):
        i = layer // 2
        if layer % 2 == 0:
            xt = _even_mixer(xt, norm_g[layer, 0], ev_w_in[i], ev_w_out[i], nsa_qk_g[i], nsa_cmp_pe[i],
                             nsa_cmp_w1[i], nsa_cmp_w2[i], diff_qk_g[i], diff_lam[i], diff_sub_g[i], layer,
                             B, T, cos, sin)
        else:
            xt = _odd_mixer(xt, norm_g[layer, 0], od_w_in[i], od_w_out[i], fox_f_b[i], fox_qk_g[i], B, T, cos, sin)
        kv = rms_matmul(mt, norm_g[layer, 2], ca_wkv[layer].astype(BF16), name="mem_kv_proj")
        w_router, b_router = router_params(moe_wg[layer], moe_bg[layer], moe_we[layer], moe_be[layer])
        xt, hn, route, onehot = cross_attention(xt, norm_g[layer, 1], ca_wq[layer].astype(BF16), kv,
                                                ca_qk_g[layer, 0] * scale, ca_qk_g[layer, 1],
                                                ca_wo[layer].astype(BF16), norm_g[layer, 3], w_router, b_router,
                                                T, mem_len)
        xt = hier_moe(xt, hn, route, onehot, moe_w1, moe_w3, moe_w2, layer)
    return xt.reshape(B, T, D)
```

```python
import functools
import math

import numpy as np
import jax
import jax.numpy as jnp
from jax import lax
from jax.experimental import pallas as pl
from jax.experimental.pallas import tpu as pltpu

F32 = jnp.float32
BF16 = jnp.bfloat16

HEAD_DIM = 128
ROPE_THETA = 10000.0
EPS = 1e-6
NEG = -1e30
LOG2E = math.log2(math.e)

NSA_HEADS = 8
NSA_GROUPS = 2
NSA_REP = NSA_HEADS // NSA_GROUPS
NSA_CMP_LEN = 32
NSA_CMP_STRIDE = 16
NSA_CMP_HIDDEN = 256
NSA_SLC_LEN = 64
NSA_SLC_TOPK = 8
NSA_WINDOW = 512
NSA_QBLOCK = 128
DIFF_HEADS = 4
FOX_HEADS = 16
MEM_HEADS = 4
N_GROUPS = 4
EXPERTS_PER_GROUP = 8
N_EXPERTS = N_GROUPS * EXPERTS_PER_GROUP
MOE_ROWS = 256
ROW_DMA_UNROLL = 8
LANES = 128
VMEM_LIMIT = 56 * 1024 * 1024


def _cparams(sem):
    return pltpu.CompilerParams(dimension_semantics=sem, vmem_limit_bytes=VMEM_LIMIT)


def _dot(a, b):
    return jnp.dot(a, b, preferred_element_type=F32)


def _dot_nt(a, b):
    return lax.dot_general(a, b, (((1,), (1,)), ((), ())), preferred_element_type=F32)


def _rms(x, g):
    ms = jnp.mean(x * x, axis=-1, keepdims=True)
    return x * lax.rsqrt(ms + EPS) * g


def _rope(y, cos, sin_signed):
    return y * cos + pltpu.roll(y, HEAD_DIM // 2, 1) * sin_signed


def _rms_matmul_kernel(x_ref, g_ref, w_ref, o_ref, xn_ref):
    @pl.when(pl.program_id(1) == 0)
    def _():
        xn_ref[...] = _rms(x_ref[...], g_ref[...]).astype(BF16)

    o_ref[...] = _dot(xn_ref[...], w_ref[...])


def rms_matmul(x, g, w, *, tm=1024, tn=512, name="rms_matmul"):
    M, K = x.shape
    N = w.shape[1]
    tm, tn = min(tm, M), min(tn, N)
    assert M % tm == 0 and N % tn == 0
    return pl.pallas_call(
        _rms_matmul_kernel,
        out_shape=jax.ShapeDtypeStruct((M, N), F32),
        grid=(M // tm, N // tn),
        in_specs=[pl.BlockSpec((tm, K), lambda i, j: (i, 0)),
                  pl.BlockSpec((1, K), lambda i, j: (0, 0)),
                  pl.BlockSpec((K, tn), lambda i, j: (0, j))],
        out_specs=pl.BlockSpec((tm, tn), lambda i, j: (i, j)),
        scratch_shapes=[pltpu.VMEM((tm, K), BF16)],
        compiler_params=_cparams(("arbitrary", "arbitrary")),
        name=name,
    )(x, g.reshape(1, K), w)


def _matmul_res_kernel(*refs, tn):
    *x_refs, w_ref, r_ref, o_ref = refs
    for n0 in range(0, o_ref.shape[1], tn):
        acc = r_ref[:, n0:n0 + tn]
        k0 = 0
        for x_ref in x_refs:
            k = x_ref.shape[1]
            acc = acc + _dot(x_ref[...], w_ref[k0:k0 + k, n0:n0 + tn])
            k0 += k
        o_ref[:, n0:n0 + tn] = acc


def matmul_residual(xs, w, res, *, tm=512, tn=512, name="matmul_residual"):
    M = xs[0].shape[0]
    N = w.shape[1]
    tm, tn = min(tm, M), min(tn, N)
    assert M % tm == 0 and N % tn == 0 and sum(x.shape[1] for x in xs) == w.shape[0]
    return pl.pallas_call(
        functools.partial(_matmul_res_kernel, tn=tn),
        out_shape=jax.ShapeDtypeStruct((M, N), F32),
        grid=(M // tm,),
        in_specs=[pl.BlockSpec((tm, x.shape[1]), lambda i: (i, 0)) for x in xs]
        + [pl.BlockSpec(w.shape, lambda i: (0, 0), pipeline_mode=pl.Buffered(1)),
           pl.BlockSpec((tm, N), lambda i: (i, 0))],
        out_specs=pl.BlockSpec((tm, N), lambda i: (i, 0)),
        compiler_params=_cparams(("arbitrary",)),
        name=name,
    )(*xs, w, res)


HEAD_KINDS = ("row", "t", "t32", "raw")


def _head_proj_kernel(x_ref, g_ref, w_ref, gains_ref, cos_ref, sin_ref, *out_refs, groups, kinds):
    refs = dict(zip(kinds, out_refs))
    xn = _rms(x_ref[...], g_ref[...]).astype(BF16)
    col = 0
    for modes in groups:
        z = _dot(xn, w_ref[:, col:col + len(modes) * LANES])
        col += len(modes) * LANES
        for c, (kind, gain_row, rotary, dst) in enumerate(modes):
            if kind == "skip":
                continue
            y = z[:, c * LANES:(c + 1) * LANES]
            if gain_row is not None:
                y = _rms(y, gains_ref[gain_row:gain_row + 1, :])
            if rotary:
                y = _rope(y, cos_ref[...], sin_ref[...])
            span = slice(dst * LANES, (dst + 1) * LANES)
            if kind == "row":
                refs[kind][:, span] = y.astype(BF16)
            elif kind == "raw":
                refs[kind][:, span] = y
            elif kind == "t":
                refs[kind][span, :] = y.T.astype(BF16)
            else:
                refs[kind][span, :] = y.T


def head_projection(x, g, w, modes, gains, cos, sin, batch, seq, *, group=8, tm=512, name="head_projection"):
    M, K = x.shape
    n_blocks = w.shape[1] // LANES
    assert len(modes) == n_blocks
    tm = min(tm, seq)
    assert seq % tm == 0 and M % tm == 0 and tm % LANES == 0
    nt = seq // tm
    counts = {k: 0 for k in HEAD_KINDS}
    placed = []
    for kind, gain_row, rotary in modes:
        placed.append((kind, gain_row, rotary, counts.get(kind, 0)))
        if kind in counts:
            counts[kind] += 1
    groups = tuple(tuple(placed[i:i + group]) for i in range(0, n_blocks, group))
    kinds = tuple(k for k in HEAD_KINDS if counts[k])
    out_shape, out_specs = [], []
    for kind in kinds:
        width = counts[kind] * LANES
        dt = BF16 if kind in ("row", "t") else F32
        if kind in ("row", "raw"):
            out_shape.append(jax.ShapeDtypeStruct((M, width), dt))
            out_specs.append(pl.BlockSpec((tm, width), lambda i: (i, 0)))
        else:
            out_shape.append(jax.ShapeDtypeStruct((batch, width, seq), dt))
            out_specs.append(pl.BlockSpec((None, width, tm), lambda i: (i // nt, 0, i % nt)))
    kern = functools.partial(_head_proj_kernel, groups=groups, kinds=kinds)
    outs = pl.pallas_call(
        kern,
        out_shape=tuple(out_shape),
        grid=(M // tm,),
        in_specs=[pl.BlockSpec((tm, K), lambda i: (i, 0)),
                  pl.BlockSpec((1, K), lambda i: (0, 0)),
                  pl.BlockSpec(w.shape, lambda i: (0, 0), pipeline_mode=pl.Buffered(1)),
                  pl.BlockSpec(gains.shape, lambda i: (0, 0)),
                  pl.BlockSpec((tm, LANES), lambda i: (i % nt, 0)),
                  pl.BlockSpec((tm, LANES), lambda i: (i % nt, 0))],
        out_specs=tuple(out_specs),
        compiler_params=_cparams(("arbitrary",)),
        name=name,
    )(x, g.reshape(1, K), w, gains, cos, sin)
    return dict(zip(kinds, outs))


def rope_tables(pos):
    half = HEAD_DIM // 2
    inv = ROPE_THETA ** (-jnp.arange(half, dtype=F32) / half)
    ang = jnp.asarray(pos).astype(F32)[:, None] * inv[None, :]
    cos, sin = jnp.cos(ang), jnp.sin(ang)
    return jnp.concatenate([cos, cos], axis=-1), jnp.concatenate([-sin, sin], axis=-1)


def _nsa_compress_kernel(tk_ref, tv_ref, pek_ref, pev_ref, w1k_ref, w1v_ref, w2k_ref, w2v_ref,
                         g_ref, cos_ref, sin_ref, kc_ref, vc_ref, *, n_rows):
    half = NSA_CMP_LEN // 2

    def compress(t_ref, pe_ref, w1_ref, w2_ref):
        hid = w1_ref.shape[-1]
        lo = jnp.zeros((n_rows, hid), F32)
        hi = jnp.zeros((n_rows, hid), F32)
        for l in range(NSA_CMP_LEN):
            rows = t_ref[pl.ds(l % half, n_rows, stride=NSA_CMP_STRIDE), :]
            a = (rows + pe_ref[l:l + 1, :]).astype(BF16)
            part = _dot(a, w1_ref[l * HEAD_DIM:(l + 1) * HEAD_DIM, :])
            if l < half:
                lo = lo + part
            else:
                hi = hi + part
        pre = lo + pltpu.roll(hi, n_rows - 1, 0)
        return _dot(jax.nn.gelu(pre).astype(BF16), w2_ref[...])

    kc = compress(tk_ref, pek_ref, w1k_ref, w2k_ref)
    kc_ref[...] = _rope(_rms(kc, g_ref[...]), cos_ref[...], sin_ref[...]).astype(BF16)
    vc_ref[...] = compress(tv_ref, pev_ref, w1v_ref, w2v_ref).astype(BF16)


def nsa_compress(z, col0, batch, seq, pe, w1, w2, g_kc, cos_c, sin_c):
    n_rows = seq // NSA_CMP_STRIDE
    G = NSA_GROUPS
    kern = functools.partial(_nsa_compress_kernel, n_rows=n_rows)
    full = lambda a: pl.BlockSpec(a.shape, lambda b, g: (0,) * a.ndim)
    part = lambda a, kv: pl.BlockSpec((None,) + a.shape[1:], lambda b, g: (kv,) + (0,) * (a.ndim - 1))
    g_kc = g_kc.reshape(1, HEAD_DIM)
    return pl.pallas_call(
        kern,
        out_shape=(jax.ShapeDtypeStruct((batch, G, n_rows, HEAD_DIM), BF16),
                   jax.ShapeDtypeStruct((batch, G, n_rows, HEAD_DIM), BF16)),
        grid=(batch, G),
        in_specs=[pl.BlockSpec((seq, LANES), lambda b, g: (b, col0 + g)),
                  pl.BlockSpec((seq, LANES), lambda b, g: (b, col0 + G + g)),
                  part(pe, 0), part(pe, 1), part(w1, 0), part(w1, 1), part(w2, 0), part(w2, 1),
                  full(g_kc), full(cos_c), full(sin_c)],
        out_specs=(pl.BlockSpec((None, None, n_rows, HEAD_DIM), lambda b, g: (b, g, 0, 0)),
                   pl.BlockSpec((None, None, n_rows, HEAD_DIM), lambda b, g: (b, g, 0, 0))),
        compiler_params=_cparams(("arbitrary", "arbitrary")),
        name="nsa_compress",
    )(z, z, pe, pe, w1, w1, w2, w2, g_kc, cos_c, sin_c)


def _softmax_rows(s):
    m = jnp.max(s, axis=1, keepdims=True)
    e = jnp.exp(s - m)
    return e / jnp.maximum(jnp.sum(e, axis=1, keepdims=True), 1e-30)


def _flash_update_t(carry, s, vt):
    m, l, acc = carry
    m_new = jnp.maximum(m, jnp.max(s, axis=0, keepdims=True))
    alpha = jnp.exp2(m - m_new)
    p = jnp.exp2(s - m_new)
    l = alpha * l + jnp.sum(p, axis=0, keepdims=True)
    acc = alpha * acc + _dot(vt, p.astype(BF16))
    return m_new, l, acc


def _flash_init_t(tq, width):
    return (jnp.full((1, tq), NEG, F32), jnp.zeros((1, tq), F32), jnp.zeros((width, tq), F32))


def _causal_keep_t(tk, tq):
    return lax.broadcasted_iota(jnp.int32, (tk, tq), 0) <= lax.broadcasted_iota(jnp.int32, (tk, tq), 1)


Q_CHUNK = 512


def _flash_streams(carries_list, k_tiles, chunk_lists, vt, masks):
    scores = []
    for k_tile, q_chunks in zip(k_tiles, chunk_lists):
        row = []
        for qc, mk in zip(q_chunks, masks):
            if mk is False:
                row.append(None)
                continue
            s = _dot_nt(k_tile, qc)
            row.append(s if mk is None else jnp.where(mk, s, NEG))
        scores.append(row)
    return [tuple(c if s is None else _flash_update_t(c, s, vt) for c, s in zip(carries, row))
            for carries, row in zip(carries_list, scores)]


def _causal_tail(carries_list, q0, n_ch, load_kv, chunk_lists):
    keep = _causal_keep_t(Q_CHUNK, Q_CHUNK)
    for kb in range(n_ch):
        k0 = q0 + kb * Q_CHUNK
        k_tiles, vt = load_kv(k0 if isinstance(k0, int) else pl.multiple_of(k0, Q_CHUNK), Q_CHUNK)
        masks = [False if kb > c else (keep if kb == c else None) for c in range(n_ch)]
        carries_list = _flash_streams(carries_list, k_tiles, chunk_lists, vt, masks)
    return carries_list


def _nsa_attn_kernel(q_ref, kc_ref, vc_ref, ks_ref, kw_ref, vst_ref, vwt_ref, gt_ref,
                     ovt_ref, expt_ref, o_ref, m_ref, l_ref, acc_ref, *, tq, tk, seq):
    R = NSA_REP
    qi = pl.program_id(2)
    q0 = qi * tq
    n_cmp = kc_ref.shape[0]
    n_slc = seq // NSA_SLC_LEN
    q = q_ref[...]
    qs = jnp.concatenate([q[:, r * HEAD_DIM:(r + 1) * HEAD_DIM] for r in range(R)], axis=0)
    qpos1 = q0 + lax.broadcasted_iota(jnp.int32, (1, tq), 1)
    hp = Q_CHUNK // tq
    n_ch = R // hp
    q_chunks = [qs[c * Q_CHUNK:(c + 1) * Q_CHUNK] for c in range(n_ch)]
    qpos = jnp.concatenate([qpos1] * hp, axis=1)

    cmp_end = lax.broadcasted_iota(jnp.int32, (n_cmp, 1), 0) * NSA_CMP_STRIDE + (NSA_CMP_LEN - 1)
    vis = cmp_end <= qpos
    kc = kc_ref[...]
    vct = vc_ref[...].astype(F32).T.astype(BF16)
    p_cs, o_cs = [], []
    for qc in q_chunks:
        s_c = jnp.where(vis, _dot_nt(kc, qc), NEG)
        e_c = jnp.where(vis, jnp.exp2(s_c - jnp.max(s_c, axis=0, keepdims=True)), 0.0)
        p_c = e_c / jnp.maximum(jnp.sum(e_c, axis=0, keepdims=True), 1e-30)
        p_cs.append(p_c)
        o_cs.append(_dot(vct, p_c.astype(BF16)))

    span = NSA_WINDOW + tq
    w0 = pl.multiple_of(jnp.maximum(q0 - NSA_WINDOW, 0), tq)
    rel = qpos - (w0 + lax.broadcasted_iota(jnp.int32, (span, 1), 0))
    kw = kw_ref[pl.ds(w0, span), :]
    vwt = vwt_ref[:, pl.ds(w0, span)]
    o_ws = []
    for qc in q_chunks:
        s_w = jnp.where(rel >= 0, jnp.where(rel < NSA_WINDOW, _dot_nt(kw, qc), NEG), NEG)
        e_w = jnp.exp2(s_w - jnp.max(s_w, axis=0, keepdims=True))
        p_w = e_w / jnp.maximum(jnp.sum(e_w, axis=0, keepdims=True), 1e-30)
        o_ws.append(_dot(vwt, p_w.astype(BF16)))

    p_sum = None
    for r in range(R):
        part = p_cs[r // hp][:, (r % hp) * tq:(r % hp + 1) * tq]
        p_sum = part if p_sum is None else p_sum + part
    p_hi = p_sum.astype(BF16)
    p_lo = (p_sum - p_hi.astype(F32)).astype(BF16)
    imp = _dot(ovt_ref[...], p_hi) + _dot(ovt_ref[...], p_lo)
    jblk = lax.broadcasted_iota(jnp.int32, (n_slc, tq), 0)
    qblk = qpos1 // NSA_SLC_LEN
    forced = (jblk == 0) | (jblk == qblk) | (jblk == qblk - 1)
    imp = jnp.where(jblk > qblk, -jnp.inf, jnp.where(forced, jnp.inf, imp))
    picked = jnp.zeros((n_slc, tq), jnp.int32)
    for _ in range(min(NSA_SLC_TOPK, n_slc)):
        cand = jnp.where(picked > 0, -jnp.inf, imp)
        best = jnp.max(cand, axis=0, keepdims=True)
        first = jnp.min(jnp.where(cand == best, jblk + picked * n_slc, n_slc), axis=0, keepdims=True)
        picked = jnp.where(jblk == first, 1, picked)
    sel_bias = jnp.where(picked > 0, 0.0, NEG).astype(BF16)

    def slc_tile(j, carries, causal):
        k0 = j * tk if isinstance(j, int) else pl.multiple_of(j * tk, tk)
        ks = ks_ref[pl.ds(k0, tk), :]
        vst = vst_ref[:, pl.ds(k0, tk)]
        bias = _dot(expt_ref[pl.ds(k0, tk), :], sel_bias)
        if causal:
            kpos = k0 + lax.broadcasted_iota(jnp.int32, (tk, 1), 0)
            bias = jnp.where(kpos <= qpos1, bias, NEG)
        bias = jnp.concatenate([bias] * hp, axis=1)
        scores = [_dot_nt(ks, qc) + bias for qc in q_chunks]
        return tuple(_flash_update_t(carry, s, vst) for carry, s in zip(carries, scores))

    last = q0 // tk
    n_tiles = seq // tk
    st_refs = (m_ref, l_ref, acc_ref)
    for c in range(n_ch):
        for ref, val in zip(st_refs, _flash_init_t(Q_CHUNK, HEAD_DIM)):
            ref[c] = val
    for count in range(1, n_tiles):
        @pl.when(last == count)
        def _():
            carries = tuple(_flash_init_t(Q_CHUNK, HEAD_DIM) for _ in range(n_ch))
            for j in range(count):
                carries = slc_tile(j, carries, False)
            for c in range(n_ch):
                for ref, val in zip(st_refs, carries[c]):
                    ref[c] = val
    carries = tuple(tuple(ref[c] for ref in st_refs) for c in range(n_ch))
    o_ss = [acc / jnp.maximum(l, 1e-30) for _, l, acc in slc_tile(last, carries, True)]

    gates = jax.nn.sigmoid(gt_ref[...])
    outs = []
    for r in range(R):
        c, cols = r // hp, slice((r % hp) * tq, (r % hp + 1) * tq)
        o_r = (gates[3 * r:3 * r + 1] * o_cs[c][:, cols] + gates[3 * r + 1:3 * r + 2] * o_ss[c][:, cols]
               + gates[3 * r + 2:3 * r + 3] * o_ws[c][:, cols])
        outs.append(o_r.T)
    o_ref[...] = jnp.concatenate(outs, axis=1).astype(BF16)


NSA_GATE_ROWS = 16


def nsa_attention(zb, zt, gt, kc, vc, batch, seq, ks_col, kw_col, vs_row, vw_row):
    tq, tk = 2 * NSA_QBLOCK, 512
    G, R = NSA_GROUPS, NSA_REP
    tk = min(tk, seq)
    assert seq % tk == 0 and tk % tq == 0 and seq >= NSA_WINDOW + tq and 3 * R <= NSA_GATE_ROWS
    assert Q_CHUNK % tq == 0 and R % (Q_CHUNK // tq) == 0
    n_chains = R * tq // Q_CHUNK
    nq = seq // tq
    n_cmp = seq // NSA_CMP_STRIDE
    n_slc = seq // NSA_SLC_LEN
    starts = np.arange(n_cmp) * NSA_CMP_STRIDE
    sb = np.arange(n_slc) * NSA_SLC_LEN
    overlap = np.clip(np.minimum(starts[:, None] + NSA_CMP_LEN, sb[None, :] + NSA_SLC_LEN)
                      - np.maximum(starts[:, None], sb[None, :]), 0, None) / NSA_CMP_LEN
    expand_t = (np.arange(seq)[:, None] // NSA_SLC_LEN == np.arange(n_slc)[None, :]).astype(np.float32)
    kern = functools.partial(_nsa_attn_kernel, tq=tq, tk=tk, seq=seq)
    return pl.pallas_call(
        kern,
        out_shape=jax.ShapeDtypeStruct((batch * seq, G * R * HEAD_DIM), BF16),
        grid=(batch, G, nq),
        in_specs=[pl.BlockSpec((tq, R * HEAD_DIM), lambda b, g, i: (b * nq + i, g)),
                  pl.BlockSpec((None, None, n_cmp, HEAD_DIM), lambda b, g, i: (b, g, 0, 0)),
                  pl.BlockSpec((None, None, n_cmp, HEAD_DIM), lambda b, g, i: (b, g, 0, 0)),
                  pl.BlockSpec((seq, LANES), lambda b, g, i: (b, ks_col + g)),
                  pl.BlockSpec((seq, LANES), lambda b, g, i: (b, kw_col + g)),
                  pl.BlockSpec((None, HEAD_DIM, seq), lambda b, g, i: (b, vs_row + g, 0)),
                  pl.BlockSpec((None, HEAD_DIM, seq), lambda b, g, i: (b, vw_row + g, 0)),
                  pl.BlockSpec((None, NSA_GATE_ROWS, tq), lambda b, g, i: (b, g, i)),
                  pl.BlockSpec((n_slc, n_cmp), lambda b, g, i: (0, 0)),
                  pl.BlockSpec((seq, n_slc), lambda b, g, i: (0, 0))],
        out_specs=pl.BlockSpec((tq, R * HEAD_DIM), lambda b, g, i: (b * nq + i, g)),
        scratch_shapes=[pltpu.VMEM((n_chains, 1, Q_CHUNK), F32), pltpu.VMEM((n_chains, 1, Q_CHUNK), F32),
                        pltpu.VMEM((n_chains, HEAD_DIM, Q_CHUNK), F32)],
        compiler_params=_cparams(("arbitrary", "arbitrary", "arbitrary")),
        name="nsa_attention",
    )(zb, kc, vc, zb, zb, zt, zt, gt, jnp.asarray(overlap.T, BF16), jnp.asarray(expand_t, BF16))


def _diff_attn_kernel(q_ref, k_ref, vt_ref, lam_ref, subg_ref, o_ref, *, tq, tk, nq, lam_init):
    qi = pl.program_id(2)
    n_ch = tq // Q_CHUNK
    q = q_ref[...]
    q1 = [q[c * Q_CHUNK:(c + 1) * Q_CHUNK, :HEAD_DIM] for c in range(n_ch)]
    q2 = [q[c * Q_CHUNK:(c + 1) * Q_CHUNK, HEAD_DIM:] for c in range(n_ch)]

    def load_kv(k0, size):
        k = k_ref[pl.ds(k0, size), :]
        return [k[:, :HEAD_DIM], k[:, HEAD_DIM:]], vt_ref[:, pl.ds(k0, size)]

    width = vt_ref.shape[0]
    lp = lam_ref[...]
    lam = (jnp.exp(jnp.sum(lp[0:1] * lp[1:2], axis=1, keepdims=True))
           - jnp.exp(jnp.sum(lp[2:3] * lp[3:4], axis=1, keepdims=True)) + lam_init)

    for qv in range(nq):
        @pl.when(qi == qv)
        def _():
            carries = [tuple(_flash_init_t(Q_CHUNK, width) for _ in range(n_ch)) for _ in range(2)]
            for j in range(qv * (tq // tk)):
                k_tiles, vt = load_kv(j * tk, tk)
                carries = _flash_streams(carries, k_tiles, [q1, q2], vt, [None] * n_ch)
            c1, c2 = _causal_tail(carries, qv * tq, n_ch, load_kv, [q1, q2])
            o = jnp.concatenate([(a1 / jnp.maximum(l1, 1e-30) - lam * (a2 / jnp.maximum(l2, 1e-30))).T
                                 for (_, l1, a1), (_, l2, a2) in zip(c1, c2)], axis=0)
            o_ref[...] = (_rms(o, subg_ref[...]) * (1.0 - lam_init)).astype(BF16)


def diff_attention(zb, zt, batch, seq, q_col, k_col, vt_row, diff_lam, sub_g, layer, *, tq=1024, tk=512):
    H = DIFF_HEADS
    W2 = 2 * HEAD_DIM
    tq = min(tq, seq)
    tk = min(tk, tq)
    assert seq % tq == 0 and tq % tk == 0 and tq % Q_CHUNK == 0
    nq = seq // tq
    lam_init = 0.8 - 0.6 * math.exp(-0.3 * layer)
    kern = functools.partial(_diff_attn_kernel, tq=tq, tk=tk, nq=nq, lam_init=lam_init)
    return pl.pallas_call(
        kern,
        out_shape=jax.ShapeDtypeStruct((batch * seq, H * W2), BF16),
        grid=(batch, H, nq),
        in_specs=[pl.BlockSpec((tq, W2), lambda b, h, i: (b * nq + i, q_col + h)),
                  pl.BlockSpec((seq, W2), lambda b, h, i: (b, k_col + h)),
                  pl.BlockSpec((None, W2, seq), lambda b, h, i: (b, vt_row + h, 0)),
                  pl.BlockSpec((4, HEAD_DIM), lambda b, h, i: (0, 0)),
                  pl.BlockSpec((1, W2), lambda b, h, i: (0, 0))],
        out_specs=pl.BlockSpec((tq, W2), lambda b, h, i: (b * nq + i, h)),
        compiler_params=_cparams(("arbitrary", "arbitrary", "arbitrary")),
        name="diff_attention",
    )(zb, zb, zt, diff_lam, sub_g.reshape(1, W2))


FOX_AUG = 6


def _fox_attn_kernel(q_ref, qaug_ref, k_ref, kaug_ref, vt_ref, o_ref, *, tq, tk, nq):
    h = pl.program_id(1)
    qi = pl.program_id(2)
    n_ch = tq // Q_CHUNK
    lane = lax.broadcasted_iota(jnp.int32, (tq, LANES), 1)
    mine = (lane >= h * FOX_AUG) & (lane < (h + 1) * FOX_AUG)
    q2 = jnp.concatenate([q_ref[...], jnp.where(mine, qaug_ref[...], jnp.zeros_like(qaug_ref))], axis=1)
    q_chunks = [q2[c * Q_CHUNK:(c + 1) * Q_CHUNK] for c in range(n_ch)]

    def load_kv(k0, size):
        k2 = jnp.concatenate([k_ref[pl.ds(k0, size), :], kaug_ref[pl.ds(k0, size), :]], axis=1)
        return [k2], vt_ref[:, pl.ds(k0, size)]

    for qv in range(nq):
        @pl.when(qi == qv)
        def _():
            carries = tuple(_flash_init_t(Q_CHUNK, HEAD_DIM) for _ in range(n_ch))
            for j in range(qv * (tq // tk)):
                k_tiles, vt = load_kv(j * tk, tk)
                (carries,) = _flash_streams([carries], k_tiles, [q_chunks], vt, [None] * n_ch)
            (carries,) = _causal_tail([carries], qv * tq, n_ch, load_kv, [q_chunks])
            o_ref[...] = jnp.concatenate([(acc / jnp.maximum(l, 1e-30)).T for _, l, acc in carries],
                                         axis=0).astype(BF16)


def fox_attention(zb, zt, qaug, kaug, batch, seq, *, tq=2048, tk=512):
    H = FOX_HEADS
    assert H * FOX_AUG <= LANES
    tq = min(tq, seq)
    tk = min(tk, tq)
    assert seq % tq == 0 and tq % tk == 0 and tq % Q_CHUNK == 0
    nq = seq // tq
    kern = functools.partial(_fox_attn_kernel, tq=tq, tk=tk, nq=nq)
    return pl.pallas_call(
        kern,
        out_shape=jax.ShapeDtypeStruct((batch * seq, H * HEAD_DIM), BF16),
        grid=(batch, H, nq),
        in_specs=[pl.BlockSpec((tq, LANES), lambda b, h, i: (b * nq + i, h)),
                  pl.BlockSpec((tq, LANES), lambda b, h, i: (b * nq + i, 0)),
                  pl.BlockSpec((seq, LANES), lambda b, h, i: (b, H + h)),
                  pl.BlockSpec((seq, LANES), lambda b, h, i: (b, 0)),
                  pl.BlockSpec((None, HEAD_DIM, seq), lambda b, h, i: (b, h, 0))],
        out_specs=pl.BlockSpec((tq, LANES), lambda b, h, i: (b * nq + i, h)),
        compiler_params=_cparams(("arbitrary", "arbitrary", "arbitrary")),
        name="fox_attention",
    )(zb, qaug, zb, kaug, zt)


def _split3(x):
    p1 = x.astype(BF16)
    r1 = x - p1.astype(F32)
    p2 = r1.astype(BF16)
    p3 = (r1 - p2.astype(F32)).astype(BF16)
    return p1, p2, p3


def _fox_gate_kernel(z_ref, b_ref, place_ref, pat_ref, qaug_ref, kaug_ref, carry_ref, *, tr):
    @pl.when(pl.program_id(1) == 0)
    def _():
        carry_ref[...] = jnp.zeros_like(carry_ref)

    logf = jax.nn.log_sigmoid(z_ref[...] + b_ref[...])
    tri = (lax.broadcasted_iota(jnp.int32, (tr, tr), 1) <= lax.broadcasted_iota(jnp.int32, (tr, tr), 0)).astype(BF16)
    p1, p2, p3 = _split3(logf)
    c = _dot(tri, p1) + _dot(tri, p2) + _dot(tri, p3) + carry_ref[...]
    carry_ref[...] = c[tr - 1:tr, :]
    e1, e2, e3 = _split3(c * LOG2E)
    place = place_ref[...]
    cexp = _dot(e1, place) + _dot(e2, place) + _dot(e3, place)
    h1, h2, h3 = _split3(cexp)
    pat = pat_ref[...]
    piece = h1.astype(F32) * pat[0:1] + h2.astype(F32) * pat[1:2] + h3.astype(F32) * pat[2:3]
    kaug_ref[...] = (pat[4:5] - piece * pat[3:4]).astype(BF16)
    qaug_ref[...] = (pat[3:4] + piece * pat[4:5]).astype(BF16)


def fox_gate_bias(z, col_block, bias_row, batch, seq, *, tr=256):
    H = FOX_HEADS
    tr = min(tr, seq)
    nt = seq // tr
    lanes = np.arange(LANES)
    used = lanes < H * FOX_AUG
    place = (lanes[None, :] // FOX_AUG == np.arange(LANES)[:, None]) & used[None, :] & (np.arange(LANES)[:, None] < H)
    j = lanes % FOX_AUG
    pat = np.zeros((8, LANES), np.float32)
    for r in range(3):
        pat[r] = used & (j % 3 == r)
    pat[3] = used & (j < 3)
    pat[4] = used & (j >= 3)
    kern = functools.partial(_fox_gate_kernel, tr=tr)
    out = jax.ShapeDtypeStruct((batch * seq, LANES), BF16)
    return pl.pallas_call(
        kern,
        out_shape=(out, out),
        grid=(batch, nt),
        in_specs=[pl.BlockSpec((tr, LANES), lambda b, i: (b * nt + i, col_block)),
                  pl.BlockSpec((1, LANES), lambda b, i: (0, 0)),
                  pl.BlockSpec((LANES, LANES), lambda b, i: (0, 0)),
                  pl.BlockSpec((8, LANES), lambda b, i: (0, 0))],
        out_specs=(pl.BlockSpec((tr, LANES), lambda b, i: (b * nt + i, 0)),
                   pl.BlockSpec((tr, LANES), lambda b, i: (b * nt + i, 0))),
        scratch_shapes=[pltpu.VMEM((1, LANES), F32)],
        compiler_params=_cparams(("arbitrary", "arbitrary")),
        name="fox_gate_bias",
    )(z, bias_row, jnp.asarray(place, BF16), jnp.asarray(pat))


def _cross_attn_kernel(x_ref, g_ref, wq_ref, kv_ref, qg_ref, kg_ref, wo_ref, gr_ref, wr_ref, br_ref,
                       o_ref, hn_ref, route_ref, onehot_ref):
    x = x_ref[...]
    q = _dot(_rms(x, g_ref[...]).astype(BF16), wq_ref[...])
    kv = kv_ref[...]
    width = MEM_HEADS * HEAD_DIM
    outs = []
    for h in range(MEM_HEADS):
        cols = slice(h * HEAD_DIM, (h + 1) * HEAD_DIM)
        qh = _rms(q[:, cols], qg_ref[...]).astype(BF16)
        kh = _rms(kv[:, cols], kg_ref[...]).astype(BF16)
        vh = kv[:, width + h * HEAD_DIM: width + (h + 1) * HEAD_DIM].astype(BF16)
        p = _softmax_rows(_dot_nt(qh, kh))
        outs.append(_dot(p.astype(BF16), vh))
    o = jnp.concatenate(outs, axis=1).astype(BF16)
    x_new = x + _dot(o, wo_ref[...])
    o_ref[...] = x_new
    hn_ref[...], route_ref[...], onehot_ref[...] = _route_tokens(x_new, gr_ref[...], wr_ref[...], br_ref[...])


def cross_attention(x, g, wq, kv, q_gain_scaled, k_gain, wo, g_moe, w_router, b_router, seq, mem_len, *, tm=512):
    M, D = x.shape
    tm = min(tm, seq)
    nt = seq // tm
    width = MEM_HEADS * HEAD_DIM
    const = lambda a: pl.BlockSpec(a.shape, lambda i: (0, 0))
    g, g_moe = g.reshape(1, D), g_moe.reshape(1, D)
    qg, kg = q_gain_scaled.reshape(1, HEAD_DIM), k_gain.reshape(1, HEAD_DIM)
    return pl.pallas_call(
        _cross_attn_kernel,
        out_shape=(jax.ShapeDtypeStruct((M, D), F32),
                   jax.ShapeDtypeStruct((M, D // 2), jnp.uint32),
                   jax.ShapeDtypeStruct((M, LANES), F32),
                   jax.ShapeDtypeStruct((M, LANES), BF16)),
        grid=(M // tm,),
        in_specs=[pl.BlockSpec((tm, D), lambda i: (i, 0)), const(g), const(wq),
                  pl.BlockSpec((mem_len, 2 * width), lambda i: (i // nt, 0)),
                  const(qg), const(kg), const(wo), const(g_moe), const(w_router), const(b_router)],
        out_specs=(pl.BlockSpec((tm, D), lambda i: (i, 0)),
                   pl.BlockSpec((tm, D // 2), lambda i: (i, 0)),
                   pl.BlockSpec((tm, LANES), lambda i: (i, 0)),
                   pl.BlockSpec((tm, LANES), lambda i: (i, 0))),
        compiler_params=_cparams(("arbitrary",)),
        name="cross_attention",
    )(x, g, wq, kv, qg, kg, wo, g_moe, w_router, b_router)


def _pack_bf16_pairs(x_bf16):
    n = x_bf16.shape[1] // 2
    lo = pltpu.bitcast(x_bf16[:, :n].astype(F32), jnp.uint32)
    hi = pltpu.bitcast(x_bf16[:, n:].astype(F32), jnp.uint32)
    return (hi & jnp.uint32(0xFFFF0000)) | (lo >> jnp.uint32(16))


def _unpack_bf16_pairs(words):
    lo = pltpu.bitcast(words << jnp.uint32(16), F32).astype(BF16)
    hi = pltpu.bitcast(words & jnp.uint32(0xFFFF0000), F32).astype(BF16)
    return jnp.concatenate([lo, hi], axis=1)


def _route_tokens(x, g, w, b):
    hn = _rms(x, g)
    h_hi = hn.astype(BF16)
    packed = _pack_bf16_pairs(h_hi)
    w_hi = w.astype(BF16)
    w_lo = (w - w_hi.astype(F32)).astype(BF16)
    h_lo = (hn - h_hi.astype(F32)).astype(BF16)
    logits = _dot(h_hi, w_hi) + _dot(h_lo, w_hi) + _dot(h_hi, w_lo) + b
    tm = logits.shape[0]
    lane = lax.broadcasted_iota(jnp.int32, (tm, LANES), 1)
    is_grp = lane < N_GROUPS
    lg = jnp.where(is_grp, logits, -jnp.inf)
    eg = jnp.exp(lg - jnp.max(lg, axis=1, keepdims=True))
    p_grp = eg / jnp.sum(eg, axis=1, keepdims=True)
    p_top = jnp.max(p_grp, axis=1, keepdims=True)
    grp = jnp.min(jnp.where(is_grp & (p_grp == p_top), lane, LANES), axis=1, keepdims=True)
    lo = N_GROUPS + grp * EXPERTS_PER_GROUP
    in_grp = (lane >= lo) & (lane < lo + EXPERTS_PER_GROUP)
    le = jnp.where(in_grp, logits, -jnp.inf)
    ee = jnp.exp(le - jnp.max(le, axis=1, keepdims=True))
    p_in = ee / jnp.sum(ee, axis=1, keepdims=True)
    m1 = jnp.max(p_in, axis=1, keepdims=True)
    i1 = jnp.min(jnp.where(in_grp & (p_in == m1), lane, LANES), axis=1, keepdims=True)
    rest = jnp.where(in_grp & (lane != i1), p_in, -jnp.inf)
    m2 = jnp.max(rest, axis=1, keepdims=True)
    i2 = jnp.min(jnp.where(rest == m2, lane, LANES), axis=1, keepdims=True)
    denom = m1 + m2
    w1 = p_top * m1 / denom
    w2 = p_top * m2 / denom
    e1 = (i1 - N_GROUPS).astype(F32)
    e2 = (i2 - N_GROUPS).astype(F32)
    route = jnp.where(lane == 0, e1, jnp.where(lane == 1, e2, jnp.where(lane == 2, w1, jnp.where(lane == 3, w2, 0.0))))
    onehot = ((lane == i1 - N_GROUPS) | (lane == i2 - N_GROUPS)).astype(BF16)
    return packed, route, onehot


def _moe_rank_kernel(onehot_ref, rank_ref, count_ref, carry_ref, *, tr):
    i = pl.program_id(0)

    @pl.when(i == 0)
    def _():
        carry_ref[...] = jnp.zeros_like(carry_ref)

    oh = onehot_ref[...]
    tri = (lax.broadcasted_iota(jnp.int32, (tr, tr), 1) < lax.broadcasted_iota(jnp.int32, (tr, tr), 0)).astype(BF16)
    rank_ref[...] = _dot(tri, oh) + carry_ref[...]
    carry_ref[...] = carry_ref[...] + jnp.sum(oh.astype(F32), axis=0, keepdims=True)
    count_ref[...] = carry_ref[...]


def moe_rank(onehot, *, tr=512):
    M = onehot.shape[0]
    tr = min(tr, M)
    kern = functools.partial(_moe_rank_kernel, tr=tr)
    return pl.pallas_call(
        kern,
        out_shape=(jax.ShapeDtypeStruct((M, LANES), F32), jax.ShapeDtypeStruct((1, LANES), F32)),
        grid=(M // tr,),
        in_specs=[pl.BlockSpec((tr, LANES), lambda i: (i, 0))],
        out_specs=(pl.BlockSpec((tr, LANES), lambda i: (i, 0)), pl.BlockSpec((1, LANES), lambda i: (0, 0))),
        scratch_shapes=[pltpu.VMEM((1, LANES), F32)],
        compiler_params=_cparams(("arbitrary",)),
        name="moe_rank",
    )(onehot)


def _moe_slot_kernel(route_ref, rank_ref, pstart_ref, o_ref):
    route = route_ref[...]
    slot = rank_ref[...] + pstart_ref[...]
    lane = lax.broadcasted_iota(jnp.int32, route.shape, 1)
    e1 = route[:, 0:1].astype(jnp.int32)
    e2 = route[:, 1:2].astype(jnp.int32)
    d1 = jnp.sum(jnp.where(lane == e1, slot, 0.0), axis=1, keepdims=True)
    d2 = jnp.sum(jnp.where(lane == e2, slot, 0.0), axis=1, keepdims=True)
    o_ref[...] = jnp.where(lane == 0, d1, jnp.where(lane == 1, d2, 0.0)).astype(jnp.int32)


def moe_slots(route, rank, pstart_row, *, tm=1024):
    M = route.shape[0]
    tm = min(tm, M)
    return pl.pallas_call(
        _moe_slot_kernel,
        out_shape=jax.ShapeDtypeStruct((M, LANES), jnp.int32),
        grid=(M // tm,),
        in_specs=[pl.BlockSpec((tm, LANES), lambda i: (i, 0)),
                  pl.BlockSpec((tm, LANES), lambda i: (i, 0)),
                  pl.BlockSpec((1, LANES), lambda i: (0, 0))],
        out_specs=pl.BlockSpec((tm, LANES), lambda i: (i, 0)),
        compiler_params=_cparams(("arbitrary",)),
        name="moe_slots",
    )(route, rank, pstart_row)


def _moe_dispatch_kernel(lastblk_ref, d1_hbm, d2_hbm, hn_ref, xb_hbm, zero_ref, s1_ref, s2_ref,
                         idx_sem, row_sem, zero_sem, *, tm, n_blocks):
    i = pl.program_id(0)

    def zero_copy(blk):
        return pltpu.make_async_copy(zero_ref, xb_hbm.at[pl.ds(blk * MOE_ROWS, MOE_ROWS)], zero_sem)

    @pl.when(i == 0)
    def _():
        zero_ref[...] = jnp.zeros_like(zero_ref)
        n_used = lastblk_ref[N_EXPERTS]
        for e in range(N_EXPERTS):
            @pl.when(lastblk_ref[e] >= 0)
            def _():
                zero_copy(lastblk_ref[e]).start()

        def start_tail(blk, _):
            zero_copy(blk).start()
            return 0

        def wait_tail(blk, _):
            zero_copy(blk).wait()
            return 0

        lax.fori_loop(n_used, n_blocks, start_tail, 0)
        for e in range(N_EXPERTS):
            @pl.when(lastblk_ref[e] >= 0)
            def _():
                zero_copy(lastblk_ref[e]).wait()
        lax.fori_loop(n_used, n_blocks, wait_tail, 0)

    c1 = pltpu.make_async_copy(d1_hbm.at[i], s1_ref, idx_sem.at[0])
    c2 = pltpu.make_async_copy(d2_hbm.at[i], s2_ref, idx_sem.at[1])
    c1.start()
    c2.start()
    c1.wait()
    c2.wait()

    def row_copy(r, slot):
        return pltpu.make_async_copy(hn_ref.at[pl.ds(r, 1)], xb_hbm.at[pl.ds(slot, 1)], row_sem)

    def issue(blk, _):
        for u in range(ROW_DMA_UNROLL):
            r = blk * ROW_DMA_UNROLL + u
            row_copy(r, s1_ref[r]).start(priority=0)
            row_copy(r, s2_ref[r]).start(priority=1)
        return 0

    lax.fori_loop(0, tm // ROW_DMA_UNROLL, issue, 0)

    def drain(r, _):
        row_copy(r, 0).wait()
        row_copy(r, 0).wait()
        return 0

    lax.fori_loop(0, tm, drain, 0, unroll=8)


def moe_dispatch(hn, d1, d2, lastblk, n_slots, *, tm=1024):
    M, D = hn.shape
    assert hn.dtype.itemsize == 4
    tm = min(tm, M)
    assert n_slots % MOE_ROWS == 0
    kern = functools.partial(_moe_dispatch_kernel, tm=tm, n_blocks=n_slots // MOE_ROWS)
    grid_spec = pltpu.PrefetchScalarGridSpec(
        num_scalar_prefetch=1,
        grid=(M // tm,),
        in_specs=[pl.BlockSpec(memory_space=pl.ANY), pl.BlockSpec(memory_space=pl.ANY),
                  pl.BlockSpec((tm, D), lambda i, *_: (i, 0))],
        out_specs=pl.BlockSpec(memory_space=pl.ANY),
        scratch_shapes=[pltpu.VMEM((MOE_ROWS, D), hn.dtype),
                        pltpu.SMEM((tm,), jnp.int32), pltpu.SMEM((tm,), jnp.int32),
                        pltpu.SemaphoreType.DMA((2,)), pltpu.SemaphoreType.DMA, pltpu.SemaphoreType.DMA],
    )
    return pl.pallas_call(
        kern,
        out_shape=jax.ShapeDtypeStruct((n_slots, D), hn.dtype),
        grid_spec=grid_spec,
        compiler_params=_cparams(("arbitrary",)),
        name="moe_dispatch",
    )(lastblk, d1.reshape(M // tm, tm), d2.reshape(M // tm, tm), hn)


def _moe_expert_kernel(blk_e_ref, n_used_ref, x_ref, w1_ref, w3_ref, w2_ref, o_ref, w1b, w3b, w2b):
    i = pl.program_id(0)

    @pl.when(i < n_used_ref[0])
    def _():
        prev = blk_e_ref[jnp.maximum(i - 1, 0)]

        @pl.when((i == 0) | (blk_e_ref[i] != prev))
        def _():
            w1b[...] = w1_ref[...].astype(BF16)
            w3b[...] = w3_ref[...].astype(BF16)
            w2b[...] = w2_ref[...].astype(BF16)

        x = _unpack_bf16_pairs(x_ref[...])
        h = (jax.nn.silu(_dot(x, w1b[...])) * _dot(x, w3b[...])).astype(BF16)
        o_ref[...] = _dot(h, w2b[...])

    @pl.when(i >= n_used_ref[0])
    def _():
        o_ref[...] = jnp.zeros_like(o_ref)


def moe_experts(xb, blk_e, n_used, w1, w3, w2, layer):
    P = xb.shape[0]
    D = w1.shape[-2]
    FF = w1.shape[-1]
    n_blk = P // MOE_ROWS

    def row_map(i, blk_e_ref, n_used_ref):
        return (jnp.minimum(i, n_used_ref[0] - 1), 0)

    def out_map(i, blk_e_ref, n_used_ref):
        return (i, 0)

    def w_map(i, blk_e_ref, n_used_ref):
        return (layer, blk_e_ref[i], 0, 0)

    grid_spec = pltpu.PrefetchScalarGridSpec(
        num_scalar_prefetch=2,
        grid=(n_blk,),
        in_specs=[pl.BlockSpec((MOE_ROWS, D // 2), row_map),
                  pl.BlockSpec((None, None, D, FF), w_map),
                  pl.BlockSpec((None, None, D, FF), w_map),
                  pl.BlockSpec((None, None, FF, D), w_map)],
        out_specs=pl.BlockSpec((MOE_ROWS, D), out_map),
        scratch_shapes=[pltpu.VMEM((D, FF), BF16), pltpu.VMEM((D, FF), BF16), pltpu.VMEM((FF, D), BF16)],
    )
    return pl.pallas_call(
        _moe_expert_kernel,
        out_shape=jax.ShapeDtypeStruct((P, D), F32),
        grid_spec=grid_spec,
        compiler_params=_cparams(("arbitrary",)),
        name="moe_experts",
    )(blk_e, n_used, xb, w1, w3, w2)


def _moe_combine_kernel(x_ref, route_ref, d1_hbm, d2_hbm, yb_hbm, o_ref, y1_ref, y2_ref, s1_ref, s2_ref,
                        idx_sem, row_sem, *, tm):
    i = pl.program_id(0)
    c1 = pltpu.make_async_copy(d1_hbm.at[i], s1_ref, idx_sem.at[0])
    c2 = pltpu.make_async_copy(d2_hbm.at[i], s2_ref, idx_sem.at[1])
    c1.start()
    c2.start()
    c1.wait()
    c2.wait()

    def row_copy(slot, dst, r):
        return pltpu.make_async_copy(yb_hbm.at[pl.ds(slot, 1)], dst.at[pl.ds(r, 1)], row_sem)

    def issue(blk, _):
        for u in range(ROW_DMA_UNROLL):
            r = blk * ROW_DMA_UNROLL + u
            row_copy(s1_ref[r], y1_ref, r).start(priority=0)
            row_copy(s2_ref[r], y2_ref, r).start(priority=1)
        return 0

    lax.fori_loop(0, tm // ROW_DMA_UNROLL, issue, 0)

    def drain(r, _):
        row_copy(0, y1_ref, r).wait()
        row_copy(0, y2_ref, r).wait()
        return 0

    lax.fori_loop(0, tm, drain, 0, unroll=8)
    route = route_ref[...]
    o_ref[...] = x_ref[...] + (y1_ref[...] * route[:, 2:3] + y2_ref[...] * route[:, 3:4])


def moe_combine(x, route, d1, d2, yb, *, tm=512):
    M, D = x.shape
    tm = min(tm, M)
    kern = functools.partial(_moe_combine_kernel, tm=tm)
    return pl.pallas_call(
        kern,
        out_shape=jax.ShapeDtypeStruct((M, D), F32),
        grid=(M // tm,),
        in_specs=[pl.BlockSpec((tm, D), lambda i: (i, 0)),
                  pl.BlockSpec((tm, LANES), lambda i: (i, 0)),
                  pl.BlockSpec(memory_space=pl.ANY),
                  pl.BlockSpec(memory_space=pl.ANY),
                  pl.BlockSpec(memory_space=pl.ANY)],
        out_specs=pl.BlockSpec((tm, D), lambda i: (i, 0)),
        scratch_shapes=[pltpu.VMEM((tm, D), F32), pltpu.VMEM((tm, D), F32),
                        pltpu.SMEM((tm,), jnp.int32), pltpu.SMEM((tm,), jnp.int32),
                        pltpu.SemaphoreType.DMA((2,)), pltpu.SemaphoreType.DMA],
        compiler_params=_cparams(("arbitrary",)),
        name="moe_combine",
    )(x, route, d1.reshape(M // tm, tm), d2.reshape(M // tm, tm), yb)


def router_params(wg, bg, we, be):
    n_route = N_GROUPS + N_EXPERTS
    w_router = jnp.pad(jnp.concatenate([wg, we], axis=1), ((0, 0), (0, LANES - n_route)))
    b_router = jnp.pad(jnp.concatenate([bg, be]), (0, LANES - n_route)).reshape(1, LANES)
    return w_router, b_router


def hier_moe(x, hn, route, onehot, w1, w3, w2, layer):
    M, D = x.shape
    rank, counts = moe_rank(onehot)
    cnt = counts[0, :N_EXPERTS].astype(jnp.int32)
    nblk = (cnt + MOE_ROWS - 1) // MOE_ROWS
    bend = jnp.cumsum(nblk)
    bstart = bend - nblk
    n_blk_max = (2 * M) // MOE_ROWS + N_EXPERTS
    pstart_row = jnp.pad((bstart * MOE_ROWS).astype(F32), (0, LANES - N_EXPERTS)).reshape(1, LANES)
    slots = moe_slots(route, rank, pstart_row)
    d1, d2 = slots[:, 0], slots[:, 1]
    n_used = bend[-1:].astype(jnp.int32)
    lastblk = jnp.concatenate([jnp.where(nblk > 0, bend - 1, -1).astype(jnp.int32), n_used])
    blk_ids = jnp.arange(n_blk_max, dtype=jnp.int32)
    blk_e = jnp.minimum(jnp.sum(bend[None, :] <= blk_ids[:, None], axis=1), N_EXPERTS - 1).astype(jnp.int32)
    xb = moe_dispatch(hn, d1, d2, lastblk, n_blk_max * MOE_ROWS)
    yb = moe_experts(xb, blk_e, n_used, w1, w3, w2, layer)
    return moe_combine(x, route, d1, d2, yb)


def _even_mixer(x, g, w_in, w_out, nsa_qk_g, cmp_pe, cmp_w1, cmp_w2, diff_qk_g, diff_lam, diff_sub_g,
                layer, batch, seq, cos, sin):
    M, D = x.shape
    scale = HEAD_DIM ** -0.5
    nq_w = NSA_HEADS * HEAD_DIM
    kv_w = 6 * NSA_GROUPS * HEAD_DIM
    gate_w = 3 * NSA_HEADS
    dq_w = 2 * DIFF_HEADS * HEAD_DIM
    dv_w = DIFF_HEADS * 2 * HEAD_DIM
    c_gate = nq_w + kv_w
    c_diff = c_gate + gate_w
    main_w = nq_w + kv_w + 2 * dq_w + dv_w
    G, R3 = NSA_GROUPS, 3 * NSA_REP
    gate_cols = []
    for grp in range(G):
        gate_cols += [w_in[:, c_gate + grp * R3:c_gate + (grp + 1) * R3], jnp.zeros((D, NSA_GATE_ROWS - R3), w_in.dtype)]
    gate_cols.append(jnp.zeros((D, LANES - G * NSA_GATE_ROWS), w_in.dtype))
    w_cat = jnp.concatenate([w_in[:, :c_gate], w_in[:, c_diff:]] + gate_cols, axis=1).astype(BF16)
    b_kv, b_dq, b_dk, b_dv, b_gate = nq_w // LANES, (nq_w + kv_w) // LANES, (nq_w + kv_w + dq_w) // LANES, \
        (nq_w + kv_w + 2 * dq_w) // LANES, main_w // LANES
    qs = scale * LOG2E
    gains = jnp.stack([nsa_qk_g[0] * qs, nsa_qk_g[2], nsa_qk_g[3], diff_qk_g[0] * qs, diff_qk_g[1]])
    modes = ([("row", 0, True)] * (b_kv - 0)
             + [("raw", None, False)] * (2 * G)
             + [("row", 1, True)] * G + [("t", None, False)] * G
             + [("row", 2, True)] * G + [("t", None, False)] * G
             + [("row", 3, True)] * (b_dk - b_dq) + [("row", 4, True)] * (b_dv - b_dk)
             + [("t", None, False)] * (b_gate - b_dv)
             + [("t32", None, False)])
    proj = head_projection(x, g, w_cat, modes, gains, cos, sin, batch, seq, name="even_in_proj")
    zb, zt, gt, zc = proj["row"], proj["t"], proj["t32"], proj["raw"]
    r_ks = b_kv
    r_kw, r_dq = r_ks + G, r_ks + 2 * G
    r_dk = r_dq + (b_dk - b_dq)
    cmp_end = np.arange(seq // NSA_CMP_STRIDE) * NSA_CMP_STRIDE + NSA_CMP_LEN - 1
    cos_c, sin_c = rope_tables(cmp_end)
    kc, vc = nsa_compress(zc, 0, batch, seq, cmp_pe, cmp_w1.astype(BF16), cmp_w2.astype(BF16), nsa_qk_g[1],
                          cos_c, sin_c)
    o_nsa = nsa_attention(zb, zt, gt, kc, vc, batch, seq, r_ks, r_kw, 0, G)
    o_diff = diff_attention(zb, zt, batch, seq, r_dq // 2, r_dk // 2, (2 * G) // 2, diff_lam, diff_sub_g, layer)
    return matmul_residual([o_nsa, o_diff], w_out.astype(BF16), x, name="even_out_proj")


def _odd_mixer(x, g, w_in, w_out, f_b, qk_g, batch, seq, cos, sin):
    M, D = x.shape
    H = FOX_HEADS
    scale = HEAD_DIM ** -0.5
    width = H * HEAD_DIM
    w_cat = jnp.pad(w_in, ((0, 0), (0, LANES - H))).astype(BF16)
    modes = [("row", 0, False)] * H + [("row", 1, False)] * H + [("t", None, False)] * H + [("raw", None, False)]
    gains = jnp.stack([qk_g[0] * (scale * LOG2E), qk_g[1]])
    proj = head_projection(x, g, w_cat, modes, gains, cos, sin, batch, seq, name="odd_in_proj")
    bias_row = jnp.pad(f_b, (0, LANES - H)).reshape(1, LANES)
    qaug, kaug = fox_gate_bias(proj["raw"], 0, bias_row, batch, seq)
    o = fox_attention(proj["row"], proj["t"], qaug, kaug, batch, seq)
    return matmul_residual([o], w_out.astype(BF16), x, name="odd_out_proj")


def kernel(x, mem, norm_g, ev_w_in, ev_w_out, nsa_qk_g, nsa_cmp_pe, nsa_cmp_w1, nsa_cmp_w2, diff_qk_g, diff_lam, diff_sub_g, od_w_in, od_w_out, fox_f_b, fox_qk_g, ca_wq, ca_wkv, ca_qk_g, ca_wo, moe_wg, moe_bg, moe_we, moe_be, moe_w1, moe_w3, moe_w2):
    B, T, D = x.shape
    mem_len = mem.shape[1]
    depth = norm_g.shape[0]
    scale = HEAD_DIM ** -0.5
    cos, sin = rope_tables(np.arange(T))
    xt = x.reshape(B * T, D)
    mt = mem.reshape(B * mem_len, D)
    for layer in range(depth):
        i = layer // 2
        if layer % 2 == 0:
            xt = _even_mixer(xt, norm_g[layer, 0], ev_w_in[i], ev_w_out[i], nsa_qk_g[i], nsa_cmp_pe[i],
                             nsa_cmp_w1[i], nsa_cmp_w2[i], diff_qk_g[i], diff_lam[i], diff_sub_g[i], layer,
                             B, T, cos, sin)
        else:
            xt = _odd_mixer(xt, norm_g[layer, 0], od_w_in[i], od_w_out[i], fox_f_b[i], fox_qk_g[i], B, T, cos, sin)
        kv = rms_matmul(mt, norm_g[layer, 2], ca_wkv[layer].astype(BF16), name="mem_kv_proj")
        w_router, b_router = router_params(moe_wg[layer], moe_bg[layer], moe_we[layer], moe_be[layer])
        xt, hn, route, onehot = cross_attention(xt, norm_g[layer, 1], ca_wq[layer].astype(BF16), kv,
                                                ca_qk_g[layer, 0] * scale, ca_qk_g[layer, 1],
                                                ca_wo[layer].astype(BF16), norm_g[layer, 3], w_router, b_router,
                                                T, mem_len)
        xt = hier_moe(xt, hn, route, onehot, moe_w1, moe_w3, moe_w2, layer)
    return xt.reshape(B, T, D)
```
